```python
import math
import jax
import jax.numpy as jnp
from jax import lax
import numpy as np

D_MODEL = 1024
BATCH = 32
SEQ = 256
DEPTH = 4
DEC_BATCH = 2
DEC_SEQ = 2048
PAST_LEN = 512

GRID_W = 64
N_MIXERS = 3
N_GDN = (DEPTH + 2) // 3
N_GQA = (DEPTH + 1) // 3
N_MLA = DEPTH // 3

GDN_HEADS = 8
GDN_DK = 128
GDN_DV = 128
GDN_CONV = 5
GDN_CHUNK = 64
GDN_QKV = GDN_HEADS * (2 * GDN_DK + GDN_DV)
GDN_IN = GDN_QKV + GDN_HEADS * GDN_DV + 4 * GDN_HEADS
DT_MIN = 0.001
DT_MAX = 0.1

GQA_HEADS = 8
GQA_KV_HEADS = 2
GQA_GROUP = GQA_HEADS // GQA_KV_HEADS
GQA_HEAD_DIM = 128
GQA_IN = (GQA_HEADS + 2 * GQA_KV_HEADS) * GQA_HEAD_DIM

MLA_HEADS = 8
MLA_Q_LORA = 512
MLA_KV_LORA = 256
MLA_NOPE = 128
MLA_ROPE = 64
MLA_V = 128
MLA_IN = MLA_Q_LORA + MLA_KV_LORA + MLA_ROPE

N_EXPERTS = 16
N_GROUPS = 4
EXPERTS_PER_GROUP = N_EXPERTS // N_GROUPS
GROUP_SCORE_TOPK = 2
TOP_K = 2
D_EXPERT = 256

ROPE_THETA = 10000.0
Q_BLOCK = 128
ALPHA = (2 * DEPTH) ** 0.25
BETA_INIT = (8 * DEPTH) ** -0.25
NORM_EPS = 1e-6

kernel_name = "hybrid_flow_trunk_ctx_prefix_step"


def _layer_norm(x, g, b):
    xf = x.astype(jnp.float32)
    mu = jnp.mean(xf, axis=-1, keepdims=True)
    var = jnp.mean(jnp.square(xf - mu), axis=-1, keepdims=True)
    return ((xf - mu) * lax.rsqrt(var + NORM_EPS)).astype(x.dtype) * g + b


def _rms_norm(x, g):
    xf = x.astype(jnp.float32)
    y = xf * lax.rsqrt(jnp.mean(jnp.square(xf), axis=-1, keepdims=True) + NORM_EPS)
    return y.astype(x.dtype) * g


def _l2_normalize(x):
    return x * lax.rsqrt(jnp.sum(jnp.square(x), axis=-1, keepdims=True) + NORM_EPS)


def _modulate(x, shift, scale):
    return x * (1 + scale[:, None]) + shift[:, None]


def _grid_angles(n_tokens, rot_dim):
    n_rows = n_tokens // GRID_W
    row = jnp.repeat(jnp.arange(n_rows, dtype=jnp.float32), GRID_W)
    col = jnp.tile(jnp.arange(GRID_W, dtype=jnp.float32), n_rows)
    n_freq = rot_dim // 4
    inv_freq = ROPE_THETA ** (-jnp.arange(n_freq, dtype=jnp.float32) / n_freq)
    return row[:, None] * inv_freq, col[:, None] * inv_freq


def _rope_half(x, ang):
    ang = ang.reshape(ang.shape[0], *([1] * (x.ndim - 3)), ang.shape[1])
    cos, sin = jnp.cos(ang).astype(x.dtype), jnp.sin(ang).astype(x.dtype)
    x1, x2 = jnp.split(x, 2, axis=-1)
    return jnp.concatenate([x1 * cos - x2 * sin, x2 * cos + x1 * sin], axis=-1)


def _axial_rope(x, row_ang, col_ang):
    half = x.shape[-1] // 2
    return jnp.concatenate([_rope_half(x[..., :half], row_ang), _rope_half(x[..., half:], col_ang)], axis=-1)


def _short_conv(x, w):
    pad = GDN_CONV // 2
    return lax.conv_general_dilated(
        x, w[:, None, :].astype(x.dtype), window_strides=(1,), padding=[(pad, pad)],
        dimension_numbers=("NWC", "WIO", "NWC"), feature_group_count=x.shape[-1])


def _chunk_gated_delta(q, k, v, g, beta, s0):
    B, N, H, _ = q.shape
    dv = v.shape[-1]
    C = GDN_CHUNK
    nc = N // C

    def to_chunks(t):
        t = t.reshape(B, nc, C, H, *t.shape[3:])
        return jnp.moveaxis(jnp.swapaxes(t, 2, 3), 1, 0)

    q, k, v, g, beta = (to_chunks(t) for t in (q, k, v, g, beta))
    g = jnp.cumsum(g, axis=-1)
    idx = jnp.arange(C)
    causal = idx[:, None] >= idx[None, :]
    strict = idx[:, None] > idx[None, :]
    decay = jnp.exp(jnp.where(causal, g[..., :, None] - g[..., None, :], -jnp.inf))
    kb = k * beta[..., None]
    a = jnp.where(strict, jnp.einsum("nbhid,nbhjd->nbhij", kb, k) * decay, 0.0)
    lower = a + jnp.eye(C, dtype=a.dtype)
    rhs = jnp.concatenate([v * beta[..., None], kb * jnp.exp(g)[..., None]], axis=-1)
    sol = lax.linalg.triangular_solve(lower, rhs, left_side=True, lower=True, unit_diagonal=True)
    u, w = sol[..., :dv], sol[..., dv:]
    qk = jnp.einsum("nbhid,nbhjd->nbhij", q, k) * decay

    def step(state, xs):
        q_c, k_c, u_c, w_c, g_c, qk_c = xs
        v_new = u_c - jnp.einsum("bhcd,bhde->bhce", w_c, state)
        o_c = (jnp.einsum("bhcd,bhde->bhce", q_c * jnp.exp(g_c)[..., None], state)
               + jnp.einsum("bhij,bhje->bhie", qk_c, v_new))
        g_last = g_c[..., -1:]
        state = (state * jnp.exp(g_last)[..., None]
                 + jnp.einsum("bhcd,bhce->bhde", k_c * jnp.exp(g_last - g_c)[..., None], v_new))
        return state, o_c

    s_final, o = lax.scan(step, s0, (q, k, u, w, g, qk))
    o = jnp.moveaxis(o, 0, 1)
    return jnp.swapaxes(o, 2, 3).reshape(B, N, H, dv), s_final


def _blocked_attention(q, k, v, scale):
    B, Nq = q.shape[:2]
    nb = Nq // Q_BLOCK
    qb = jnp.moveaxis(q.reshape(B, nb, Q_BLOCK, *q.shape[2:]), 1, 0)

    def attend(qi):
        s = jnp.einsum("bqhgd,bkhd->bhgqk", qi, k).astype(jnp.float32) * scale
        p = jax.nn.softmax(s, axis=-1).astype(v.dtype)
        return jnp.einsum("bhgqk,bkhe->bqhge", p, v)

    o = lax.map(attend, qb)
    return jnp.moveaxis(o, 0, 1).reshape(B, Nq, *o.shape[3:])


def _gdn_mixer(h, s0, w_in, conv_w, a_log, dt_bias, norm_g, w_out):
    B, N, _ = h.shape
    proj = h @ w_in
    qkv = jax.nn.silu(_short_conv(proj[..., :GDN_QKV], conv_w))
    nqk = GDN_HEADS * GDN_DK
    q = qkv[..., :nqk].reshape(B, N, GDN_HEADS, GDN_DK)
    k = qkv[..., nqk:2 * nqk].reshape(B, N, GDN_HEADS, GDN_DK)
    v = qkv[..., 2 * nqk:].reshape(B, N, GDN_HEADS, GDN_DV)
    z = proj[..., GDN_QKV:GDN_QKV + GDN_HEADS * GDN_DV].reshape(B, N, GDN_HEADS, GDN_DV)
    ab = proj[..., GDN_QKV + GDN_HEADS * GDN_DV:].astype(jnp.float32).reshape(B, N, 2, 2, GDN_HEADS)
    g = -jnp.exp(a_log.astype(jnp.float32)) * jax.nn.softplus(ab[:, :, 0] + dt_bias.astype(jnp.float32))
    beta = jax.nn.sigmoid(ab[:, :, 1])
    qf = _l2_normalize(q.astype(jnp.float32)) * GDN_DK ** -0.5
    kf = _l2_normalize(k.astype(jnp.float32))
    vf = v.astype(jnp.float32)
    s0f = s0.astype(jnp.float32)
    o_f, s_f = _chunk_gated_delta(qf, kf, vf, g[:, :, 0], beta[:, :, 0], s0f[:, 0])
    rev = lambda t: jnp.flip(t, axis=1)
    o_b, s_b = _chunk_gated_delta(rev(qf), rev(kf), rev(vf), rev(g[:, :, 1]), rev(beta[:, :, 1]), s0f[:, 1])
    o = (o_f + rev(o_b)).astype(h.dtype)
    o = _rms_norm(o, norm_g) * jax.nn.silu(z)
    out = o.reshape(B, N, GDN_HEADS * GDN_DV) @ w_out
    return out, jnp.stack([s_f, s_b], axis=1).astype(h.dtype)


def _gqa_mixer(h, angles, ctx_k, ctx_v, w_in, q_norm, k_norm, w_out):
    B, N, _ = h.shape
    proj = h @ w_in
    nq = GQA_HEADS * GQA_HEAD_DIM
    nkv = GQA_KV_HEADS * GQA_HEAD_DIM
    q = _rms_norm(proj[..., :nq].reshape(B, N, GQA_KV_HEADS, GQA_GROUP, GQA_HEAD_DIM), q_norm)
    k = _rms_norm(proj[..., nq:nq + nkv].reshape(B, N, GQA_KV_HEADS, GQA_HEAD_DIM), k_norm)
    v = proj[..., nq + nkv:].reshape(B, N, GQA_KV_HEADS, GQA_HEAD_DIM)
    if angles is None:
        k_all, v_all = k, v
    else:
        q = _axial_rope(q, *angles)
        k_all = jnp.concatenate([ctx_k, _axial_rope(k, *angles)], axis=1)
        v_all = jnp.concatenate([ctx_v, v], axis=1)
    o = _blocked_attention(q, k_all, v_all, GQA_HEAD_DIM ** -0.5)
    return o.reshape(B, N, nq) @ w_out, k, v


def _mla_mixer(h, angles, ctx_ckv, ctx_krope, w_in, q_norm, kv_norm, w_q_up, w_kv_up, w_out):
    B, N, _ = h.shape
    proj = h @ w_in
    cq = _rms_norm(proj[..., :MLA_Q_LORA], q_norm)
    ckv = _rms_norm(proj[..., MLA_Q_LORA:MLA_Q_LORA + MLA_KV_LORA], kv_norm)
    krope = proj[..., MLA_Q_LORA + MLA_KV_LORA:]
    q = (cq @ w_q_up).reshape(B, N, MLA_HEADS, MLA_NOPE + MLA_ROPE)
    if angles is None:
        ckv_all, krope_all = ckv, krope
    else:
        q = jnp.concatenate([q[..., :MLA_NOPE], _axial_rope(q[..., MLA_NOPE:], *angles)], axis=-1)
        ckv_all = jnp.concatenate([ctx_ckv, ckv], axis=1)
        krope_all = jnp.concatenate([ctx_krope, _axial_rope(krope, *angles)], axis=1)
    nk = ckv_all.shape[1]
    kv = (ckv_all @ w_kv_up).reshape(B, nk, MLA_HEADS, MLA_NOPE + MLA_V)
    k = jnp.concatenate(
        [kv[..., :MLA_NOPE], jnp.broadcast_to(krope_all[:, :, None, :], (B, nk, MLA_HEADS, MLA_ROPE))], axis=-1)
    o = _blocked_attention(q[:, :, :, None, :], k, kv[..., MLA_NOPE:], (MLA_NOPE + MLA_ROPE) ** -0.5)
    return o.reshape(B, N, MLA_HEADS * MLA_V) @ w_out, ckv, krope


def _moe(h, router_w, router_b, w_gate, w_up, w_down):
    B, N, D = h.shape
    t = h.reshape(B * N, D)
    scores = jax.nn.sigmoid((t @ router_w).astype(jnp.float32))
    sel = scores + router_b.astype(jnp.float32)
    group_score = lax.top_k(sel.reshape(-1, N_GROUPS, EXPERTS_PER_GROUP), GROUP_SCORE_TOPK)[0].sum(-1)
    best_group = jnp.argmax(group_score, axis=-1)
    in_group = (jnp.arange(N_EXPERTS) // EXPERTS_PER_GROUP)[None, :] == best_group[:, None]
    _, e_idx = lax.top_k(jnp.where(in_group, sel, -jnp.inf), TOP_K)
    w = jnp.take_along_axis(scores, e_idx, axis=-1)
    w = w / jnp.sum(w, axis=-1, keepdims=True)
    combine = jnp.einsum("tk,tke->te", w, jax.nn.one_hot(e_idx, N_EXPERTS, dtype=jnp.float32)).astype(h.dtype)
    hid = jax.nn.silu(jnp.einsum("td,edf->tef", t, w_gate)) * jnp.einsum("td,edf->tef", t, w_up)
    out = jnp.einsum("tef,efd->td", hid * combine[..., None], w_down)
    return out.reshape(B, N, D)


def setup_inputs(seed: int = 0) -> dict:
    key = jax.random.key(seed)
    ks = iter(jax.random.split(key, 40))
    D = D_MODEL

    def nrm(shape, scale=1.0):
        return jax.random.normal(next(ks), shape, jnp.float32) * scale

    def gain(shape):
        return 1.0 + nrm(shape, 0.01)

    x_prompt = nrm((BATCH, SEQ, D))
    x_sample = nrm((DEC_BATCH, DEC_SEQ, D))
    state_gdn = nrm((DEC_BATCH, N_GDN, 2, GDN_HEADS, GDN_DK, GDN_DV), 0.1)
    cache_gqa_k = nrm((DEC_BATCH, N_GQA, PAST_LEN, GQA_KV_HEADS, GQA_HEAD_DIM))
    cache_gqa_v = nrm((DEC_BATCH, N_GQA, PAST_LEN, GQA_KV_HEADS, GQA_HEAD_DIM))
    cache_mla_ckv = nrm((DEC_BATCH, N_MLA, PAST_LEN, MLA_KV_LORA))
    cache_mla_krope = nrm((DEC_BATCH, N_MLA, PAST_LEN, MLA_ROPE))
    c = nrm((DEC_BATCH, D))
    c_ctx = nrm((D,))
    w_ada = nrm((DEPTH, D, 6 * D), 0.3 * D ** -0.5)
    b_ada = nrm((DEPTH, 6 * D), 0.01)
    ln_mix_g = gain((DEPTH, D))
    ln_mix_b = nrm((DEPTH, D), 0.01)
    ln_ffn_g = gain((DEPTH, D))
    ln_ffn_b = nrm((DEPTH, D), 0.01)
    gdn_w_in = nrm((N_GDN, D, GDN_IN), D ** -0.5)
    gdn_conv = nrm((N_GDN, GDN_CONV, GDN_QKV), GDN_CONV ** -0.5)
    gdn_a_log = jnp.log(jax.random.uniform(next(ks), (N_GDN, 2, GDN_HEADS), jnp.float32, 1.0, 16.0))
    u = jax.random.uniform(next(ks), (N_GDN, 2, GDN_HEADS), jnp.float32)
    dt = jnp.exp(math.log(DT_MIN) + u * (math.log(DT_MAX) - math.log(DT_MIN)))
    gdn_dt_bias = dt + jnp.log(-jnp.expm1(-dt))
    gdn_norm_g = gain((N_GDN, GDN_DV))
    gdn_w_out = nrm((N_GDN, GDN_HEADS * GDN_DV, D), BETA_INIT * (GDN_HEADS * GDN_DV) ** -0.5)
    gqa_w_in = nrm((N_GQA, D, GQA_IN), D ** -0.5)
    gqa_q_norm = gain((N_GQA, GQA_HEAD_DIM))
    gqa_k_norm = gain((N_GQA, GQA_HEAD_DIM))
    gqa_w_out = nrm((N_GQA, GQA_HEADS * GQA_HEAD_DIM, D), BETA_INIT * (GQA_HEADS * GQA_HEAD_DIM) ** -0.5)
    mla_w_in = nrm((N_MLA, D, MLA_IN), D ** -0.5)
    mla_q_norm = gain((N_MLA, MLA_Q_LORA))
    mla_kv_norm = gain((N_MLA, MLA_KV_LORA))
    mla_w_q_up = nrm((N_MLA, MLA_Q_LORA, MLA_HEADS * (MLA_NOPE + MLA_ROPE)), MLA_Q_LORA ** -0.5)
    mla_w_kv_up = nrm((N_MLA, MLA_KV_LORA, MLA_HEADS * (MLA_NOPE + MLA_V)), MLA_KV_LORA ** -0.5)
    mla_w_out = nrm((N_MLA, MLA_HEADS * MLA_V, D), BETA_INIT * (MLA_HEADS * MLA_V) ** -0.5)
    router_w = nrm((D, N_EXPERTS), D ** -0.5)
    router_b = nrm((N_EXPERTS,), 0.01)
    moe_w_gate = nrm((DEPTH, N_EXPERTS, D, D_EXPERT), D ** -0.5)
    moe_w_up = nrm((DEPTH, N_EXPERTS, D, D_EXPERT), D ** -0.5)
    moe_w_down = nrm((DEPTH, N_EXPERTS, D_EXPERT, D), BETA_INIT * D_EXPERT ** -0.5)
    return {
        "x_prompt": x_prompt, "x_sample": x_sample,
        "state_gdn": state_gdn, "cache_gqa_k": cache_gqa_k, "cache_gqa_v": cache_gqa_v,
        "cache_mla_ckv": cache_mla_ckv, "cache_mla_krope": cache_mla_krope,
        "c": c, "c_ctx": c_ctx,
        "w_ada": w_ada, "b_ada": b_ada,
        "ln_mix_g": ln_mix_g, "ln_mix_b": ln_mix_b, "ln_ffn_g": ln_ffn_g, "ln_ffn_b": ln_ffn_b,
        "gdn_w_in": gdn_w_in, "gdn_conv": gdn_conv, "gdn_a_log": gdn_a_log, "gdn_dt_bias": gdn_dt_bias,
        "gdn_norm_g": gdn_norm_g, "gdn_w_out": gdn_w_out,
        "gqa_w_in": gqa_w_in, "gqa_q_norm": gqa_q_norm, "gqa_k_norm": gqa_k_norm, "gqa_w_out": gqa_w_out,
        "mla_w_in": mla_w_in, "mla_q_norm": mla_q_norm, "mla_kv_norm": mla_kv_norm,
        "mla_w_q_up": mla_w_q_up, "mla_w_kv_up": mla_w_kv_up, "mla_w_out": mla_w_out,
        "router_w": router_w, "router_b": router_b,
        "moe_w_gate": moe_w_gate, "moe_w_up": moe_w_up, "moe_w_down": moe_w_down,
    }


def reference(x_prompt, x_sample, state_gdn, cache_gqa_k, cache_gqa_v, cache_mla_ckv, cache_mla_krope,
              c, c_ctx, w_ada, b_ada, ln_mix_g, ln_mix_b, ln_ffn_g, ln_ffn_b,
              gdn_w_in, gdn_conv, gdn_a_log, gdn_dt_bias, gdn_norm_g, gdn_w_out,
              gqa_w_in, gqa_q_norm, gqa_k_norm, gqa_w_out,
              mla_w_in, mla_q_norm, mla_kv_norm, mla_w_q_up, mla_w_kv_up, mla_w_out,
              router_w, router_b, moe_w_gate, moe_w_up, moe_w_down):
    n_lat = x_sample.shape[1]
    angles_gqa = _grid_angles(n_lat, GQA_HEAD_DIM)
    angles_mla = _grid_angles(n_lat, MLA_ROPE)
    cond_ctx = jax.nn.silu(c_ctx)[None]
    cond_lat = jax.nn.silu(c)
    xp, xs = x_prompt, x_sample
    st_gdn, st_k, st_v, st_ckv, st_krope = [], [], [], [], []
    for l in range(DEPTH):
        kind, li = l % N_MIXERS, l // N_MIXERS
        m_p = jnp.split(cond_ctx @ w_ada[l] + b_ada[l], 6, axis=-1)
        m_s = jnp.split(cond_lat @ w_ada[l] + b_ada[l], 6, axis=-1)
        hp = _modulate(xp, m_p[0], m_p[1])
        hs = _modulate(xs, m_s[0], m_s[1])
        if kind == 0:
            w = (gdn_w_in[li], gdn_conv[li], gdn_a_log[li], gdn_dt_bias[li], gdn_norm_g[li], gdn_w_out[li])
            zero_state = jnp.zeros((xp.shape[0], 2, GDN_HEADS, GDN_DK, GDN_DV), xp.dtype)
            op, s_ctx = _gdn_mixer(hp, zero_state, *w)
            os_, _ = _gdn_mixer(hs, state_gdn[:, li], *w)
            st_gdn.append(s_ctx)
        elif kind == 1:
            w = (gqa_w_in[li], gqa_q_norm[li], gqa_k_norm[li], gqa_w_out[li])
            op, k_ctx, v_ctx = _gqa_mixer(hp, None, None, None, *w)
            os_, _, _ = _gqa_mixer(hs, angles_gqa, cache_gqa_k[:, li], cache_gqa_v[:, li], *w)
            st_k.append(k_ctx)
            st_v.append(v_ctx)
        else:
            w = (mla_w_in[li], mla_q_norm[li], mla_kv_norm[li], mla_w_q_up[li], mla_w_kv_up[li], mla_w_out[li])
            op, ckv_ctx, krope_ctx = _mla_mixer(hp, None, None, None, *w)
            os_, _, _ = _mla_mixer(hs, angles_mla, cache_mla_ckv[:, li], cache_mla_krope[:, li], *w)
            st_ckv.append(ckv_ctx)
            st_krope.append(krope_ctx)
        xp = _layer_norm(ALPHA * xp + (1 + m_p[2])[:, None] * op, ln_mix_g[l], ln_mix_b[l])
        xs = _layer_norm(ALPHA * xs + (1 + m_s[2])[:, None] * os_, ln_mix_g[l], ln_mix_b[l])
        moe_w = (router_w, router_b, moe_w_gate[l], moe_w_up[l], moe_w_down[l])
        fp = _moe(_modulate(xp, m_p[3], m_p[4]), *moe_w)
        fs = _moe(_modulate(xs, m_s[3], m_s[4]), *moe_w)
        xp = _layer_norm(ALPHA * xp + (1 + m_p[5])[:, None] * fp, ln_ffn_g[l], ln_ffn_b[l])
        xs = _layer_norm(ALPHA * xs + (1 + m_s[5])[:, None] * fs, ln_ffn_g[l], ln_ffn_b[l])
    new_state_gdn = jnp.stack(st_gdn, axis=1)
    new_cache_gqa_k = jnp.stack(st_k, axis=1)
    new_cache_gqa_v = jnp.stack(st_v, axis=1)
    new_cache_mla_ckv = jnp.stack(st_ckv, axis=1)
    new_cache_mla_krope = jnp.stack(st_krope, axis=1)
    return (xp, xs, new_state_gdn, new_cache_gqa_k, new_cache_gqa_v, new_cache_mla_ckv, new_cache_mla_krope)
```

```python
import functools
import math

import jax
import jax.numpy as jnp
from jax import lax
from jax.experimental import pallas as pl
from jax.experimental.pallas import tpu as pltpu

F32 = jnp.float32
BF16 = jnp.bfloat16

D_MODEL = 1024
BATCH = 32
SEQ = 256
DEPTH = 4
DEC_BATCH = 2
DEC_SEQ = 2048
PAST_LEN = 512
GRID_W = 64
N_MIXERS = 3

GDN_HEADS = 8
GDN_DK = 128
GDN_DV = 128
GDN_CONV = 5
GDN_CHUNK = 64
GDN_QKV = GDN_HEADS * (2 * GDN_DK + GDN_DV)
GDN_QKVZ = GDN_QKV + GDN_HEADS * GDN_DV

GQA_HEADS = 8
GQA_KV_HEADS = 2
GQA_GROUP = GQA_HEADS // GQA_KV_HEADS
GQA_HEAD_DIM = 128
GQA_IN = (GQA_HEADS + 2 * GQA_KV_HEADS) * GQA_HEAD_DIM

MLA_HEADS = 8
MLA_Q_LORA = 512
MLA_KV_LORA = 256
MLA_NOPE = 128
MLA_ROPE = 64
MLA_V = 128
MLA_IN = MLA_Q_LORA + MLA_KV_LORA + MLA_ROPE

N_EXPERTS = 16
N_GROUPS = 4
EXPERTS_PER_GROUP = N_EXPERTS // N_GROUPS
D_EXPERT = 256

ROPE_THETA = 10000.0
ALPHA = (2 * DEPTH) ** 0.25
NORM_EPS = 1e-6

T_CTX = BATCH * SEQ
T_LAT = DEC_BATCH * DEC_SEQ
T_ALL = T_CTX + T_LAT
SEG_ROWS = DEC_SEQ
N_CTX_SEGS = T_CTX // SEG_ROWS
MOD_ROWS = 8

V7X_LANES = 128
V7X_SUBLANES = 8
V7X_VMEM_BYTES = 64 * 1024 * 1024
VMEM_LIMIT = V7X_VMEM_BYTES * 7 // 8

GDN_SUPER = 256
CHUNKS_PER_SUPER = GDN_SUPER // GDN_CHUNK


def _cparams(*sem):
    return pltpu.CompilerParams(dimension_semantics=sem, vmem_limit_bytes=VMEM_LIMIT)


def _mm(a, b):
    return jnp.dot(a.astype(BF16), b.astype(BF16), preferred_element_type=F32)


def _mm_nt(a, b):
    return lax.dot_general(a.astype(BF16), b.astype(BF16), (((1,), (1,)), ((), ())),
                           preferred_element_type=F32)


def _silu(x):
    return x * (1.0 / (1.0 + jnp.exp(-x)))


def _sigmoid(x):
    return 1.0 / (1.0 + jnp.exp(-x))


def _chunk_of(idx):
    return idx >> int(math.log2(GDN_CHUNK))


def _seg_row(i, tiles_per_seg):
    return jnp.maximum(lax.div(i, tiles_per_seg) - (N_CTX_SEGS - 1), 0)


def _layer_norm(y, g, b):
    mu = jnp.mean(y, axis=-1, keepdims=True)
    yc = y - mu
    var = jnp.mean(yc * yc, axis=-1, keepdims=True)
    return yc * lax.rsqrt(var + NORM_EPS) * g + b


def _rms(x, g):
    return x * lax.rsqrt(jnp.mean(x * x, axis=-1, keepdims=True) + NORM_EPS) * g


def _ada_kernel(c_ref, w_ref, b_ref, o_ref):
    o_ref[...] = _mm(_silu(c_ref[...]), w_ref[...]) + b_ref[...]


def _ada_call(cond, w_ada, b_ada):
    tn = 1536
    n = w_ada.shape[-1]
    return pl.pallas_call(
        _ada_kernel,
        grid=(DEPTH, n // tn),
        in_specs=[
            pl.BlockSpec((MOD_ROWS, D_MODEL), lambda l, j: (0, 0)),
            pl.BlockSpec((None, D_MODEL, tn), lambda l, j: (l, 0, j)),
            pl.BlockSpec((None, 1, tn), lambda l, j: (l, 0, j)),
        ],
        out_specs=pl.BlockSpec((None, MOD_ROWS, tn), lambda l, j: (l, 0, j)),
        out_shape=jax.ShapeDtypeStruct((DEPTH, MOD_ROWS, n), F32),
        compiler_params=_cparams("arbitrary", "arbitrary"),
        name="ada_mod",
    )(cond, w_ada, b_ada.reshape(DEPTH, 1, n))


def _modproj_kernel(x_ref, sh_ref, sc_ref, w_ref, *rest, tiles_per_seg, has_extra):
    if has_extra:
        wx_ref, o_ref, ox_ref, h_scr = rest
    else:
        o_ref, h_scr = rest
    i, j = pl.program_id(0), pl.program_id(1)

    @pl.when(j == 0)
    def _():
        r = _seg_row(i, tiles_per_seg)
        sh = sh_ref[pl.ds(r, 1), :]
        sc = sc_ref[pl.ds(r, 1), :]
        h = (x_ref[...] * (1.0 + sc) + sh).astype(BF16)
        h_scr[...] = h
        if has_extra:
            ox_ref[...] = _mm(h, wx_ref[...])

    o_ref[...] = _mm(h_scr[...], w_ref[...])


def _modproj_call(x, mods_l, shift_idx, scale_idx, w, n_out, w_extra=None, *, tm=1024, tn=512, name):
    t = x.shape[0]
    has_extra = w_extra is not None
    in_specs = [
        pl.BlockSpec((tm, D_MODEL), lambda i, j: (i, 0)),
        pl.BlockSpec((MOD_ROWS, D_MODEL), lambda i, j: (0, shift_idx)),
        pl.BlockSpec((MOD_ROWS, D_MODEL), lambda i, j: (0, scale_idx)),
        pl.BlockSpec((D_MODEL, tn), lambda i, j: (0, j)),
    ]
    out_specs = [pl.BlockSpec((tm, tn), lambda i, j: (i, j))]
    out_shape = [jax.ShapeDtypeStruct((t, n_out), F32)]
    args = [x, mods_l, mods_l, w]
    if has_extra:
        nx = w_extra.shape[1]
        in_specs.append(pl.BlockSpec((D_MODEL, nx), lambda i, j: (0, 0)))
        out_specs.append(pl.BlockSpec((tm, nx), lambda i, j: (i, 0)))
        out_shape.append(jax.ShapeDtypeStruct((t, nx), F32))
        args.append(w_extra)
    res = pl.pallas_call(
        functools.partial(_modproj_kernel, tiles_per_seg=SEG_ROWS // tm, has_extra=has_extra),
        grid=(t // tm, n_out // tn),
        in_specs=in_specs,
        out_specs=out_specs,
        out_shape=out_shape,
        scratch_shapes=[pltpu.VMEM((tm, D_MODEL), BF16)],
        compiler_params=_cparams("arbitrary", "arbitrary"),
        name=name,
    )(*args)
    return res if has_extra else res[0]


def _rmsproj_kernel(x_ref, g_ref, w_ref, *rest, normalize, emit_xn):
    if emit_xn:
        o_ref, xn_ref, h_scr = rest
    else:
        o_ref, h_scr = rest
    j = pl.program_id(1)

    @pl.when(j == 0)
    def _():
        x = x_ref[...]
        xn = _rms(x, g_ref[...]) if normalize else x
        h_scr[...] = xn.astype(BF16)
        if emit_xn:
            xn_ref[...] = xn

    o_ref[...] = _mm(h_scr[...], w_ref[...])


def _rmsproj_call(x, col_block, k, g, w, *, normalize=True, emit_xn=False, tm=1024, tn=512, name):
    t = x.shape[0]
    n_out = w.shape[1]
    out_specs = [pl.BlockSpec((tm, tn), lambda i, j: (i, j))]
    out_shape = [jax.ShapeDtypeStruct((t, n_out), F32)]
    if emit_xn:
        out_specs.append(pl.BlockSpec((tm, k), lambda i, j: (i, 0)))
        out_shape.append(jax.ShapeDtypeStruct((t, k), F32))
    res = pl.pallas_call(
        functools.partial(_rmsproj_kernel, normalize=normalize, emit_xn=emit_xn),
        grid=(t // tm, n_out // tn),
        in_specs=[
            pl.BlockSpec((tm, k), lambda i, j: (i, col_block)),
            pl.BlockSpec((1, k), lambda i, j: (0, 0)),
            pl.BlockSpec((k, tn), lambda i, j: (0, j)),
        ],
        out_specs=out_specs,
        out_shape=out_shape,
        scratch_shapes=[pltpu.VMEM((tm, k), BF16)],
        compiler_params=_cparams("arbitrary", "arbitrary"),
        name=name,
    )(x, g.reshape(1, k), w)
    return res if emit_xn else res[0]


def _outln_kernel(o_ref, w_ref, x_ref, gate_ref, g_ref, b_ref, out_ref, wb_scr, *, tiles_per_seg):
    i = pl.program_id(0)

    @pl.when(i == 0)
    def _():
        wb_scr[...] = w_ref[...].astype(BF16)

    gate = gate_ref[pl.ds(_seg_row(i, tiles_per_seg), 1), :]
    f = jnp.dot(o_ref[...].astype(BF16), wb_scr[...], preferred_element_type=F32)
    out_ref[...] = _layer_norm(ALPHA * x_ref[...] + (1.0 + gate) * f, g_ref[...], b_ref[...])


def _outln_call(o, w, x, mods_l, gate_idx, g, b, *, tm=512, name):
    t, k = o.shape
    return pl.pallas_call(
        functools.partial(_outln_kernel, tiles_per_seg=SEG_ROWS // tm),
        grid=(t // tm,),
        in_specs=[
            pl.BlockSpec((tm, k), lambda i: (i, 0)),
            pl.BlockSpec((k, D_MODEL), lambda i: (0, 0)),
            pl.BlockSpec((tm, D_MODEL), lambda i: (i, 0)),
            pl.BlockSpec((MOD_ROWS, D_MODEL), lambda i: (0, gate_idx)),
            pl.BlockSpec((1, D_MODEL), lambda i: (0, 0)),
            pl.BlockSpec((1, D_MODEL), lambda i: (0, 0)),
        ],
        out_specs=pl.BlockSpec((tm, D_MODEL), lambda i: (i, 0)),
        out_shape=jax.ShapeDtypeStruct((t, D_MODEL), F32),
        scratch_shapes=[pltpu.VMEM((k, D_MODEL), BF16)],
        compiler_params=_cparams("arbitrary"),
        name=name,
    )(o, w, x, mods_l, g.reshape(1, D_MODEL), b.reshape(1, D_MODEL))


def _gdn_gates_kernel(ab_ref, alog_ref, dt_ref, o_ref):
    ab = ab_ref[...]
    tm = ab.shape[0]
    x = ab + dt_ref[...]
    softplus = jnp.maximum(x, 0.0) + jnp.log(1.0 + jnp.exp(-jnp.abs(x)))
    g = -jnp.exp(alog_ref[...]) * softplus
    ri = lax.broadcasted_iota(jnp.int32, (tm, tm), 0)
    ci = lax.broadcasted_iota(jnp.int32, (tm, tm), 1)
    same = _chunk_of(ri) == _chunk_of(ci)
    lower = jnp.where(same & (ri >= ci), 1.0, 0.0).astype(F32)
    upper = jnp.where(same & (ri <= ci), 1.0, 0.0).astype(F32)
    pre = jnp.dot(lower, g, precision=lax.Precision.HIGHEST, preferred_element_type=F32)
    suf = jnp.dot(upper, g, precision=lax.Precision.HIGHEST, preferred_element_type=F32)
    lane = lax.broadcasted_iota(jnp.int32, ab.shape, 1)
    o_ref[...] = jnp.where(lane < GDN_HEADS, pre, jnp.where(lane < 2 * GDN_HEADS, suf, _sigmoid(ab)))


def _gdn_gates_call(ab, a_log, dt_bias):
    t, n = ab.shape
    tm = GDN_SUPER
    pad = n - 2 * GDN_HEADS
    alog_row = jnp.pad(a_log.reshape(1, -1), ((0, 0), (0, pad)))
    dt_row = jnp.pad(dt_bias.reshape(1, -1), ((0, 0), (0, pad)))
    return pl.pallas_call(
        _gdn_gates_kernel,
        grid=(t // tm,),
        in_specs=[
            pl.BlockSpec((tm, n), lambda i: (i, 0)),
            pl.BlockSpec((1, n), lambda i: (0, 0)),
            pl.BlockSpec((1, n), lambda i: (0, 0)),
        ],
        out_specs=pl.BlockSpec((tm, n), lambda i: (i, 0)),
        out_shape=jax.ShapeDtypeStruct((t, n), F32),
        compiler_params=_cparams("arbitrary"),
        name="gdn_gates",
    )(ab, alog_row, dt_row)


def _loop(n, body):
    if n <= 4:
        for i in range(n):
            body(i)
    else:
        def wrapped(i, carry):
            body(i)
            return carry
        lax.fori_loop(0, n, wrapped, 0)


def _gdn_core_kernel(q_ref, k_ref, v_ref, z_ref, cq_ref, ck_ref, cv_ref, gc_ref, gr_ref, ng_ref, *rest,
                     n_tok, has_s0):
    if has_s0:
        s0_ref, o_ref, st_ref = rest[:3]
        scr = rest[3:]
    else:
        o_ref, st_ref = rest[:2]
        scr = rest[2:]
    pad_scr, kn_scr, u_scr, w_scr, qe_scr, qk_scr, oo_scr, s_scr = scr
    n_super = n_tok // GDN_SUPER
    n_chunk = n_tok // GDN_CHUNK
    halo = V7X_SUBLANES
    sb = GDN_SUPER
    ck = GDN_CHUNK

    zeros_halo = jnp.zeros((halo, GDN_DK), F32)
    for a, src in enumerate((q_ref, k_ref, v_ref)):
        pad_scr[a, 0:halo, :] = zeros_halo
        pad_scr[a, halo:halo + n_tok, :] = src[...]
        pad_scr[a, halo + n_tok:2 * halo + n_tok, :] = zeros_halo
    for d in range(2):
        s_scr[d] = s0_ref[d] if has_s0 else jnp.zeros((GDN_DK, GDN_DV), F32)

    ri = lax.broadcasted_iota(jnp.int32, (sb, sb), 0)
    ci = lax.broadcasted_iota(jnp.int32, (sb, sb), 1)
    same_chunk = _chunk_of(ri) == _chunk_of(ci)
    eye = jnp.where(ri == ci, 1.0, 0.0).astype(F32)
    n_levels = int(math.log2(ck))

    def level_mask(l):
        return ((ri >> (l + 1)) == (ci >> (l + 1))) & ((ri >> l) != (ci >> l))

    def conv_silu(a, w_ref, r0):
        xp = pad_scr[a, pl.ds(r0, sb + 2 * halo), :]
        acc = None
        for j in range(GDN_CONV):
            shift = (GDN_CONV // 2 - j) % (sb + 2 * halo)
            tap = xp if shift == 0 else pltpu.roll(xp, shift, 0)
            term = w_ref[j:j + 1, :] * tap[halo:halo + sb, :]
            acc = term if acc is None else acc + term
        return _silu(acc)

    def phase_a(s):
        r0 = pl.multiple_of(s * sb, sb)
        q = conv_silu(0, cq_ref, r0)
        k = conv_silu(1, ck_ref, r0)
        v = conv_silu(2, cv_ref, r0)
        q = q * lax.rsqrt(jnp.sum(q * q, axis=-1, keepdims=True) + NORM_EPS) * (GDN_DK ** -0.5)
        k = k * lax.rsqrt(jnp.sum(k * k, axis=-1, keepdims=True) + NORM_EPS)
        kn_scr[pl.ds(r0, sb), :] = k
        kb = k.astype(BF16)
        p = _mm_nt(kb, kb)
        qk = _mm_nt(q, kb)
        for d in range(2):
            gcol = gc_ref[pl.ds(r0, sb), d:d + 1]
            bcol = gc_ref[pl.ds(r0, sb), 2 + d:3 + d]
            grow = gr_ref[s, d:d + 1, :]
            if d == 0:
                mask, strict = same_chunk & (ri >= ci), same_chunk & (ri > ci)
            else:
                mask, strict = same_chunk & (ri <= ci), same_chunk & (ri < ci)
            dec = jnp.exp(jnp.where(mask, gcol - grow, -jnp.inf))
            a = jnp.where(strict, p * bcol * dec, 0.0)
            t = eye - jnp.where(level_mask(0), a, 0.0)
            for l in range(1, n_levels):
                t = t - _mm(t, _mm(jnp.where(level_mask(l), a, 0.0), t))
            eg = jnp.exp(gcol)
            uw = _mm(t, jnp.concatenate([v * bcol, k * (bcol * eg)], axis=1))
            u_scr[d, pl.ds(r0, sb), :] = uw[:, :GDN_DV]
            w_scr[d, pl.ds(r0, sb), :] = uw[:, GDN_DV:]
            qe_scr[d, pl.ds(r0, sb), :] = q * eg
            qkd = qk * dec
            for c in range(CHUNKS_PER_SUPER):
                qk_scr[d, pl.ds(r0 + c * ck, ck), :] = qkd[c * ck:(c + 1) * ck, c * ck:(c + 1) * ck]

    _loop(n_super, phase_a)

    def phase_b(i):
        for d in range(2):
            c = i if d == 0 else n_chunk - 1 - i
            r = pl.multiple_of(c * ck, ck)
            r_last = r + (ck - 1) if d == 0 else r
            s = s_scr[d]
            u = u_scr[d, pl.ds(r, ck), :]
            w = w_scr[d, pl.ds(r, ck), :]
            qe = qe_scr[d, pl.ds(r, ck), :]
            k = kn_scr[pl.ds(r, ck), :]
            qk = qk_scr[d, pl.ds(r, ck), :]
            gcol = gc_ref[pl.ds(r, ck), d:d + 1]
            gl = gc_ref[pl.ds(r_last, 1), d:d + 1]
            ws_qs = _mm(jnp.concatenate([w, qe], axis=0), s)
            v_new = u - ws_qs[:ck]
            oo_scr[d, pl.ds(r, ck), :] = ws_qs[ck:] + _mm(qk, v_new)
            kw = k * jnp.exp(gl - gcol)
            s_scr[d] = s * jnp.exp(gl) + _mm(kw.T, v_new)

    _loop(n_chunk, phase_b)

    def finish(s):
        r0 = pl.multiple_of(s * sb, sb)
        o = oo_scr[0, pl.ds(r0, sb), :] + oo_scr[1, pl.ds(r0, sb), :]
        o_ref[pl.ds(r0, sb), :] = _rms(o, ng_ref[...]) * _silu(z_ref[pl.ds(r0, sb), :])

    _loop(n_super, finish)
    for d in range(2):
        st_ref[d] = s_scr[d]


def _gdn_core_call(qkvz, conv_w, gc, gr, norm_g, s0, *, batch, n_tok, row_block_off, name):
    has_s0 = s0 is not None
    h = GDN_HEADS
    dk = GDN_DK
    n_super = n_tok // GDN_SUPER

    def col(off):
        return pl.BlockSpec((n_tok, dk), lambda b, hh: (b + row_block_off, off + hh))

    def cw(off):
        return pl.BlockSpec((GDN_CONV, dk), lambda b, hh: (0, off + hh))

    in_specs = [col(0), col(h), col(2 * h), col(3 * h), cw(0), cw(h), cw(2 * h),
                pl.BlockSpec((None, None, n_tok, 4), lambda b, hh: (b, hh, 0, 0)),
                pl.BlockSpec((None, None, n_super, 4, GDN_SUPER), lambda b, hh: (b, hh, 0, 0, 0)),
                pl.BlockSpec((1, GDN_DV), lambda b, hh: (0, 0))]
    args = [qkvz, qkvz, qkvz, qkvz, conv_w, conv_w, conv_w, gc, gr, norm_g.reshape(1, GDN_DV)]
    st_spec = pl.BlockSpec((None, 2, None, dk, GDN_DV), lambda b, hh: (b, 0, hh, 0, 0))
    if has_s0:
        in_specs.append(st_spec)
        args.append(s0)
    return pl.pallas_call(
        functools.partial(_gdn_core_kernel, n_tok=n_tok, has_s0=has_s0),
        grid=(batch, h),
        in_specs=in_specs,
        out_specs=[pl.BlockSpec((n_tok, GDN_DV), lambda b, hh: (b, hh)), st_spec],
        out_shape=[jax.ShapeDtypeStruct((batch * n_tok, h * GDN_DV), F32),
                   jax.ShapeDtypeStruct((batch, 2, h, dk, GDN_DV), F32)],
        scratch_shapes=[
            pltpu.VMEM((3, n_tok + 2 * V7X_SUBLANES, dk), F32),
            pltpu.VMEM((n_tok, dk), F32),
            pltpu.VMEM((2, n_tok, GDN_DV), F32),
            pltpu.VMEM((2, n_tok, dk), F32),
            pltpu.VMEM((2, n_tok, dk), F32),
            pltpu.VMEM((2, n_tok, GDN_CHUNK), F32),
            pltpu.VMEM((2, n_tok, GDN_DV), F32),
            pltpu.VMEM((2, dk, GDN_DV), F32),
        ],
        compiler_params=_cparams("arbitrary", "arbitrary"),
        name=name,
    )(*args)


def _gdn_layer(x, mods_l, li, state_gdn, gdn_w_in, gdn_conv, gdn_a_log, gdn_dt_bias, gdn_norm_g):
    w_in = gdn_w_in[li]
    w_ab = jnp.pad(w_in[:, GDN_QKVZ:], ((0, 0), (0, V7X_LANES - 4 * GDN_HEADS)))
    qkvz, ab = _modproj_call(x, mods_l, 0, 1, w_in, GDN_QKVZ, w_ab, name="gdn_in")
    gates = _gdn_gates_call(ab, gdn_a_log[li], gdn_dt_bias[li])[:, :4 * GDN_HEADS]

    def head_layouts(g, batch, n_tok):
        g = g.reshape(batch, n_tok, 4, GDN_HEADS)
        gc = jnp.transpose(g, (0, 3, 1, 2))
        gr = jnp.transpose(g, (0, 3, 2, 1)).reshape(batch, GDN_HEADS, 4, n_tok // GDN_SUPER, GDN_SUPER)
        return gc, jnp.transpose(gr, (0, 1, 3, 2, 4))

    gc_p, gr_p = head_layouts(gates[:T_CTX], BATCH, SEQ)
    gc_s, gr_s = head_layouts(gates[T_CTX:], DEC_BATCH, DEC_SEQ)
    o_p, st_p = _gdn_core_call(qkvz, gdn_conv[li], gc_p, gr_p, gdn_norm_g[li], None,
                               batch=BATCH, n_tok=SEQ, row_block_off=0, name="gdn_core_ctx")
    o_s, _ = _gdn_core_call(qkvz, gdn_conv[li], gc_s, gr_s, gdn_norm_g[li], state_gdn[:, li],
                            batch=DEC_BATCH, n_tok=DEC_SEQ, row_block_off=T_CTX // DEC_SEQ, name="gdn_core_lat")
    return jnp.concatenate([o_p, o_s], axis=0), st_p


def _rope_tables(n_tokens, rot_dim, reps):
    n_rows = n_tokens // GRID_W
    row = jnp.repeat(jnp.arange(n_rows, dtype=F32), GRID_W)
    colp = jnp.tile(jnp.arange(GRID_W, dtype=F32), n_rows)
    n_freq = rot_dim // 4
    inv_freq = ROPE_THETA ** (-jnp.arange(n_freq, dtype=F32) / n_freq)
    ra, ca = row[:, None] * inv_freq, colp[:, None] * inv_freq
    cos = jnp.concatenate([jnp.cos(ra), jnp.cos(ra), jnp.cos(ca), jnp.cos(ca)], axis=-1)
    sin = jnp.concatenate([-jnp.sin(ra), jnp.sin(ra), -jnp.sin(ca), jnp.sin(ca)], axis=-1)
    return jnp.tile(cos, (1, reps)), jnp.tile(sin, (1, reps))


def _rope(x, cos, sin, quarter):
    n = x.shape[-1]
    lane = lax.broadcasted_iota(jnp.int32, x.shape, x.ndim - 1)
    first = (lane & (2 * quarter - 1)) < quarter
    partner = jnp.where(first, pltpu.roll(x, n - quarter, x.ndim - 1), pltpu.roll(x, quarter, x.ndim - 1))
    return x * cos + partner * sin


def _gqa_prep_kernel(kv_ref, g_ref, *rest, rope):
    if rope:
        cos_ref, sin_ref, k_ref, v_ref = rest
    else:
        k_ref, v_ref = rest
    kv = kv_ref[...]
    nk = GQA_KV_HEADS * GQA_HEAD_DIM
    for hh in range(GQA_KV_HEADS):
        k = _rms(kv[:, hh * GQA_HEAD_DIM:(hh + 1) * GQA_HEAD_DIM], g_ref[...])
        if rope:
            k = _rope(k, cos_ref[...], sin_ref[...], GQA_HEAD_DIM // 4)
        k_ref[:, hh * GQA_HEAD_DIM:(hh + 1) * GQA_HEAD_DIM] = k
    v_ref[...] = kv[:, nk:]


def _gqa_prep_call(proj, k_norm, row_off, n_rows, tables, *, tm=512, name):
    rope = tables is not None
    nkv = 2 * GQA_KV_HEADS * GQA_HEAD_DIM
    kv_col = GQA_HEADS * GQA_HEAD_DIM // nkv
    off = row_off // tm
    in_specs = [pl.BlockSpec((tm, nkv), lambda i: (i + off, kv_col)),
                pl.BlockSpec((1, GQA_HEAD_DIM), lambda i: (0, 0))]
    args = [proj, k_norm.reshape(1, GQA_HEAD_DIM)]
    if rope:
        per_seq = DEC_SEQ // tm
        in_specs += [pl.BlockSpec((tm, GQA_HEAD_DIM), lambda i: (i % per_seq, 0))] * 2
        args += list(tables)
    half = nkv // 2
    return pl.pallas_call(
        functools.partial(_gqa_prep_kernel, rope=rope),
        grid=(n_rows // tm,),
        in_specs=in_specs,
        out_specs=[pl.BlockSpec((tm, half), lambda i: (i, 0))] * 2,
        out_shape=[jax.ShapeDtypeStruct((n_rows, half), F32)] * 2,
        compiler_params=_cparams("arbitrary"),
        name=name,
    )(*args)


def _softmax_av(scores, values):
    m = None
    for s in scores:
        sm = jnp.max(s, axis=-1, keepdims=True)
        m = sm if m is None else jnp.maximum(m, sm)
    den = None
    acc = None
    for s, v in zip(scores, values):
        p = jnp.exp(s - m)
        ps = jnp.sum(p, axis=-1, keepdims=True)
        pv = _mm(p, v)
        den = ps if den is None else den + ps
        acc = pv if acc is None else acc + pv
    return acc / den


def _gqa_attn_kernel(q_ref, qg_ref, *rest, n_seg, rope, kv_per_step):
    if rope:
        cos_ref, sin_ref = rest[:2]
        rest = rest[2:]
    seg_refs = rest[:2 * n_seg]
    o_ref = rest[2 * n_seg]
    dh = GQA_HEAD_DIM
    scale = dh ** -0.5
    tq = q_ref.shape[0]
    for kh in range(kv_per_step):
        qs = []
        for g in range(GQA_GROUP):
            c0 = (kh * GQA_GROUP + g) * dh
            qh = _rms(q_ref[:, c0:c0 + dh], qg_ref[...])
            if rope:
                qh = _rope(qh, cos_ref[...], sin_ref[...], dh // 4)
            qs.append(qh.astype(BF16))
        qst = jnp.concatenate(qs, axis=0)
        scores, values = [], []
        for sgi in range(n_seg):
            kk = seg_refs[2 * sgi][:, kh * dh:(kh + 1) * dh]
            scores.append(_mm_nt(qst, kk) * scale)
            values.append(seg_refs[2 * sgi + 1][:, kh * dh:(kh + 1) * dh])
        o = _softmax_av(scores, values)
        for g in range(GQA_GROUP):
            c0 = (kh * GQA_GROUP + g) * dh
            o_ref[:, c0:c0 + dh] = o[g * tq:(g + 1) * tq]


def _gqa_attn_call(proj, q_norm, segs, tables, *, batch, n_q, row_off, tq, kv_per_step, name):
    rope = tables is not None
    dh = GQA_HEAD_DIM
    qw = GQA_GROUP * dh * kv_per_step
    kw = dh * kv_per_step
    nqb = n_q // tq
    off = row_off // tq
    in_specs = [pl.BlockSpec((tq, qw), lambda b, h, i: (b * nqb + i + off, h)),
                pl.BlockSpec((1, dh), lambda b, h, i: (0, 0))]
    args = [proj, q_norm.reshape(1, dh)]
    if rope:
        in_specs += [pl.BlockSpec((tq, dh), lambda b, h, i: (i, 0))] * 2
        args += list(tables)
    for k2d, v2d, rows in segs:
        in_specs += [pl.BlockSpec((rows, kw), lambda b, h, i: (b, h))] * 2
        args += [k2d, v2d]
    return pl.pallas_call(
        functools.partial(_gqa_attn_kernel, n_seg=len(segs), rope=rope, kv_per_step=kv_per_step),
        grid=(batch, GQA_KV_HEADS // kv_per_step, nqb),
        in_specs=in_specs,
        out_specs=pl.BlockSpec((tq, qw), lambda b, h, i: (b * nqb + i, h)),
        out_shape=jax.ShapeDtypeStruct((batch * n_q, GQA_HEADS * dh), F32),
        compiler_params=_cparams("arbitrary", "arbitrary", "arbitrary"),
        name=name,
    )(*args)


def _gqa_layer(x, mods_l, li, cache_k, cache_v, gqa_w_in, gqa_q_norm, gqa_k_norm):
    proj = _modproj_call(x, mods_l, 0, 1, gqa_w_in[li], GQA_IN, name="gqa_in")
    tables = _rope_tables(DEC_SEQ, GQA_HEAD_DIM, 1)
    nkv = GQA_KV_HEADS * GQA_HEAD_DIM
    k_p, v_p = _gqa_prep_call(proj, gqa_k_norm[li], 0, T_CTX, None, name="gqa_prep_ctx")
    k_s, v_s = _gqa_prep_call(proj, gqa_k_norm[li], T_CTX, T_LAT, tables, name="gqa_prep_lat")
    o_p = _gqa_attn_call(proj, gqa_q_norm[li], [(k_p, v_p, SEQ)], None, batch=BATCH, n_q=SEQ, row_off=0,
                         tq=SEQ, kv_per_step=GQA_KV_HEADS, name="gqa_attn_ctx")
    ck = cache_k[:, li].reshape(DEC_BATCH * PAST_LEN, nkv)
    cv = cache_v[:, li].reshape(DEC_BATCH * PAST_LEN, nkv)
    o_s = _gqa_attn_call(proj, gqa_q_norm[li], [(ck, cv, PAST_LEN), (k_s, v_s, DEC_SEQ)], tables,
                         batch=DEC_BATCH, n_q=DEC_SEQ, row_off=T_CTX, tq=128, kv_per_step=1, name="gqa_attn_lat")
    new_k = k_p.reshape(BATCH, SEQ, GQA_KV_HEADS, GQA_HEAD_DIM)
    new_v = v_p.reshape(BATCH, SEQ, GQA_KV_HEADS, GQA_HEAD_DIM)
    return jnp.concatenate([o_p, o_s], axis=0), new_k, new_v


def _krope_kernel(x_ref, *rest, rope):
    if rope:
        cos_ref, sin_ref, o_ref = rest
    else:
        (o_ref,) = rest
    x = x_ref[...]
    x2 = jnp.concatenate([x, x], axis=1)
    if rope:
        x2 = _rope(x2, cos_ref[...], sin_ref[...], MLA_ROPE // 4)
    o_ref[...] = x2


def _krope_call(krope, tables, *, tm=512, name):
    rope = tables is not None
    n = krope.shape[0]
    in_specs = [pl.BlockSpec((tm, MLA_ROPE), lambda i: (i, 0))]
    args = [krope]
    if rope:
        per_seq = DEC_SEQ // tm
        in_specs += [pl.BlockSpec((tm, 2 * MLA_ROPE), lambda i: (i % per_seq, 0))] * 2
        args += list(tables)
    return pl.pallas_call(
        functools.partial(_krope_kernel, rope=rope),
        grid=(n // tm,),
        in_specs=in_specs,
        out_specs=pl.BlockSpec((tm, 2 * MLA_ROPE), lambda i: (i, 0)),
        out_shape=jax.ShapeDtypeStruct((n, 2 * MLA_ROPE), F32),
        compiler_params=_cparams("arbitrary"),
        name=name,
    )(*args)


def _mla_attn_kernel(qn_ref, qr_ref, *rest, n_seg, rope, heads_per_step):
    if rope:
        cos_ref, sin_ref = rest[:2]
        rest = rest[2:]
    seg_refs = rest[:2 * n_seg]
    o_ref = rest[2 * n_seg]
    scale = (MLA_NOPE + MLA_ROPE) ** -0.5
    h0 = pl.program_id(1) * heads_per_step
    pair_w = 2 * MLA_ROPE
    for j in range(heads_per_step):
        qn = qn_ref[:, j * MLA_NOPE:(j + 1) * MLA_NOPE]
        pair = qr_ref[:, (j // 2) * pair_w:(j // 2 + 1) * pair_w]
        if rope:
            pair = _rope(pair, cos_ref[...], sin_ref[...], MLA_ROPE // 4)
        lane = lax.broadcasted_iota(jnp.int32, pair.shape, 1)
        mine = (lane >> int(math.log2(MLA_ROPE))) == ((h0 + j) & 1)
        q = jnp.concatenate([qn, jnp.where(mine, pair, 0.0)], axis=1).astype(BF16)
        scores, values = [], []
        for sgi in range(n_seg):
            kv = seg_refs[2 * sgi][:, j * (MLA_NOPE + MLA_V):(j + 1) * (MLA_NOPE + MLA_V)]
            kmat = jnp.concatenate([kv[:, :MLA_NOPE], seg_refs[2 * sgi + 1][...]], axis=1)
            scores.append(_mm_nt(q, kmat) * scale)
            values.append(kv[:, MLA_NOPE:])
        o_ref[:, j * MLA_V:(j + 1) * MLA_V] = _softmax_av(scores, values)


def _mla_attn_call(q, segs, tables, *, batch, n_q, row_off, tq, heads_per_step, name):
    rope = tables is not None
    hps = heads_per_step
    nqb = n_q // tq
    off = row_off // tq
    rope_w = max(2 * MLA_ROPE, MLA_ROPE * hps)
    rope_base = MLA_HEADS * MLA_NOPE // rope_w

    def rope_idx(b, h, i):
        return (b * nqb + i + off, rope_base + (h * hps * MLA_ROPE) // rope_w)

    in_specs = [pl.BlockSpec((tq, MLA_NOPE * hps), lambda b, h, i: (b * nqb + i + off, h)),
                pl.BlockSpec((tq, rope_w), rope_idx)]
    args = [q, q]
    if rope:
        in_specs += [pl.BlockSpec((tq, 2 * MLA_ROPE), lambda b, h, i: (i, 0))] * 2
        args += list(tables)
    for kv2d, kr2, rows, kv_row_off in segs:
        kv_off = kv_row_off // rows
        in_specs += [pl.BlockSpec((rows, (MLA_NOPE + MLA_V) * hps), lambda b, h, i, kv_off=kv_off: (b + kv_off, h)),
                     pl.BlockSpec((rows, 2 * MLA_ROPE), lambda b, h, i: (b, 0))]
        args += [kv2d, kr2]
    return pl.pallas_call(
        functools.partial(_mla_attn_kernel, n_seg=len(segs), rope=rope, heads_per_step=hps),
        grid=(batch, MLA_HEADS // hps, nqb),
        in_specs=in_specs,
        out_specs=pl.BlockSpec((tq, MLA_V * hps), lambda b, h, i: (b * nqb + i, h)),
        out_shape=jax.ShapeDtypeStruct((batch * n_q, MLA_HEADS * MLA_V), F32),
        compiler_params=_cparams("arbitrary", "arbitrary", "arbitrary"),
        name=name,
    )(*args)


def _mla_layer(x, mods_l, li, cache_ckv, cache_krope, mla_w_in, mla_q_norm, mla_kv_norm, mla_w_q_up, mla_w_kv_up):
    w_in = jnp.pad(mla_w_in[li], ((0, 0), (0, 7 * V7X_LANES - MLA_IN)))
    proj = _modproj_call(x, mods_l, 0, 1, w_in, 7 * V7X_LANES, tn=7 * V7X_LANES, name="mla_in")
    wq = mla_w_q_up[li].reshape(MLA_Q_LORA, MLA_HEADS, MLA_NOPE + MLA_ROPE)
    wq = jnp.concatenate([wq[:, :, :MLA_NOPE].reshape(MLA_Q_LORA, -1), wq[:, :, MLA_NOPE:].reshape(MLA_Q_LORA, -1)], axis=1)
    q = _rmsproj_call(proj, 0, MLA_Q_LORA, mla_q_norm[li], wq, name="mla_q_up")
    kv, ckv = _rmsproj_call(proj, MLA_Q_LORA // MLA_KV_LORA, MLA_KV_LORA, mla_kv_norm[li], mla_w_kv_up[li],
                            emit_xn=True, name="mla_kv_up")
    krope = proj[:, MLA_Q_LORA + MLA_KV_LORA:MLA_IN]
    tables = _rope_tables(DEC_SEQ, MLA_ROPE, 2)
    kr2_p = _krope_call(krope[:T_CTX], None, name="mla_krope_ctx")
    kr2_s = _krope_call(krope[T_CTX:], tables, name="mla_krope_lat")
    ctx_ckv = cache_ckv[:, li].reshape(DEC_BATCH * PAST_LEN, MLA_KV_LORA)
    kv_c = _rmsproj_call(ctx_ckv, 0, MLA_KV_LORA, jnp.ones((MLA_KV_LORA,), F32), mla_w_kv_up[li], normalize=False,
                         name="mla_kv_up_cache")
    kr2_c = _krope_call(cache_krope[:, li].reshape(DEC_BATCH * PAST_LEN, MLA_ROPE), None, name="mla_krope_cache")
    o_p = _mla_attn_call(q, [(kv, kr2_p, SEQ, 0)], None, batch=BATCH, n_q=SEQ, row_off=0, tq=SEQ,
                         heads_per_step=MLA_HEADS, name="mla_attn_ctx")
    o_s = _mla_attn_call(q, [(kv_c, kr2_c, PAST_LEN, 0), (kv, kr2_s, DEC_SEQ, T_CTX)], tables, batch=DEC_BATCH,
                         n_q=DEC_SEQ, row_off=T_CTX, tq=512, heads_per_step=1, name="mla_attn_lat")
    new_ckv = ckv[:T_CTX].reshape(BATCH, SEQ, MLA_KV_LORA)
    new_krope = krope[:T_CTX].reshape(BATCH, SEQ, MLA_ROPE)
    return jnp.concatenate([o_p, o_s], axis=0), new_ckv, new_krope


def _router_kernel(x_ref, sh_ref, sc_ref, rw_ref, rb_ref, o_ref, *, tiles_per_seg):
    i = pl.program_id(0)
    r = _seg_row(i, tiles_per_seg)
    h = x_ref[...] * (1.0 + sc_ref[pl.ds(r, 1), :]) + sh_ref[pl.ds(r, 1), :]
    logits = lax.dot_general(rw_ref[...], h, (((1,), (1,)), ((), ())), precision=lax.Precision.HIGHEST,
                             preferred_element_type=F32)[:N_EXPERTS]
    scores = _sigmoid(logits)
    sel = scores + rb_ref[0:N_EXPERTS, :]
    tm = sel.shape[1]
    neg = -jnp.inf
    gs = []
    for g in range(N_GROUPS):
        a, b, c, d = (sel[EXPERTS_PER_GROUP * g + j:EXPERTS_PER_GROUP * g + j + 1, :] for j in range(4))
        gs.append(jnp.maximum(jnp.maximum(jnp.maximum(a + b, a + c), jnp.maximum(a + d, b + c)),
                              jnp.maximum(b + d, c + d)))
    gmax = jnp.maximum(jnp.maximum(gs[0], gs[1]), jnp.maximum(gs[2], gs[3]))
    best = jnp.where(gs[0] == gmax, 0, jnp.where(gs[1] == gmax, 1, jnp.where(gs[2] == gmax, 2, 3)))
    eid = lax.broadcasted_iota(jnp.int32, (N_EXPERTS, tm), 0)
    masked = jnp.where((eid >> int(math.log2(EXPERTS_PER_GROUP))) == best, sel, neg)
    m1 = jnp.max(masked, axis=0, keepdims=True)
    i1 = jnp.min(jnp.where(masked == m1, eid, N_EXPERTS), axis=0, keepdims=True)
    masked2 = jnp.where(eid == i1, neg, masked)
    m2 = jnp.max(masked2, axis=0, keepdims=True)
    i2 = jnp.min(jnp.where(masked2 == m2, eid, N_EXPERTS), axis=0, keepdims=True)
    w1 = jnp.sum(jnp.where(eid == i1, scores, 0.0), axis=0, keepdims=True)
    w2 = jnp.sum(jnp.where(eid == i2, scores, 0.0), axis=0, keepdims=True)
    tot = w1 + w2
    comb = jnp.where(eid == i1, w1 / tot, 0.0) + jnp.where(eid == i2, w2 / tot, 0.0)
    comb = jnp.concatenate([comb, jnp.zeros((V7X_LANES - N_EXPERTS, tm), F32)], axis=0)
    o_ref[...] = comb.T


def _router_call(x, mods_l, router_w, router_b, *, tm=512):
    t = x.shape[0]
    rw = jnp.pad(router_w.T, ((0, V7X_LANES - N_EXPERTS), (0, 0)))
    rb = jnp.pad(router_b.reshape(N_EXPERTS, 1), ((0, V7X_LANES - N_EXPERTS), (0, 0)))
    return pl.pallas_call(
        functools.partial(_router_kernel, tiles_per_seg=SEG_ROWS // tm),
        grid=(t // tm,),
        in_specs=[
            pl.BlockSpec((tm, D_MODEL), lambda i: (i, 0)),
            pl.BlockSpec((MOD_ROWS, D_MODEL), lambda i: (0, 3)),
            pl.BlockSpec((MOD_ROWS, D_MODEL), lambda i: (0, 4)),
            pl.BlockSpec((V7X_LANES, D_MODEL), lambda i: (0, 0)),
            pl.BlockSpec((V7X_LANES, 1), lambda i: (0, 0)),
        ],
        out_specs=pl.BlockSpec((tm, V7X_LANES), lambda i: (i, 0)),
        out_shape=jax.ShapeDtypeStruct((t, V7X_LANES), F32),
        compiler_params=_cparams("arbitrary"),
        name="moe_router",
    )(x, mods_l, mods_l, rw, rb)


def _moe_kernel(x_ref, sh_ref, sc_ref, gate_ref, comb_ref, wg_ref, wu_ref, wd_ref, g_ref, b_ref, o_ref,
                h_scr, acc_scr, *, tiles_per_seg):
    i, e = pl.program_id(0), pl.program_id(1)
    r = _seg_row(i, tiles_per_seg)

    @pl.when(e == 0)
    def _():
        h_scr[...] = (x_ref[...] * (1.0 + sc_ref[pl.ds(r, 1), :]) + sh_ref[pl.ds(r, 1), :]).astype(BF16)
        acc_scr[...] = jnp.zeros_like(acc_scr)

    comb = comb_ref[...]
    lane = lax.broadcasted_iota(jnp.int32, comb.shape, 1)
    c_e = jnp.sum(jnp.where(lane == e, comb, 0.0), axis=1, keepdims=True)
    h = h_scr[...]
    hid = _silu(_mm(h, wg_ref[...])) * _mm(h, wu_ref[...]) * c_e
    acc_scr[...] += _mm(hid, wd_ref[...])

    @pl.when(e == N_EXPERTS - 1)
    def _():
        gate = gate_ref[pl.ds(r, 1), :]
        o_ref[...] = _layer_norm(ALPHA * x_ref[...] + (1.0 + gate) * acc_scr[...], g_ref[...], b_ref[...])


def _moe_call(x, mods_l, comb, w_gate, w_up, w_down, g, b, *, tm=1024):
    t = x.shape[0]
    return pl.pallas_call(
        functools.partial(_moe_kernel, tiles_per_seg=SEG_ROWS // tm),
        grid=(t // tm, N_EXPERTS),
        in_specs=[
            pl.BlockSpec((tm, D_MODEL), lambda i, e: (i, 0)),
            pl.BlockSpec((MOD_ROWS, D_MODEL), lambda i, e: (0, 3)),
            pl.BlockSpec((MOD_ROWS, D_MODEL), lambda i, e: (0, 4)),
            pl.BlockSpec((MOD_ROWS, D_MODEL), lambda i, e: (0, 5)),
            pl.BlockSpec((tm, V7X_LANES), lambda i, e: (i, 0)),
            pl.BlockSpec((None, D_MODEL, D_EXPERT), lambda i, e: (e, 0, 0)),
            pl.BlockSpec((None, D_MODEL, D_EXPERT), lambda i, e: (e, 0, 0)),
            pl.BlockSpec((None, D_EXPERT, D_MODEL), lambda i, e: (e, 0, 0)),
            pl.BlockSpec((1, D_MODEL), lambda i, e: (0, 0)),
            pl.BlockSpec((1, D_MODEL), lambda i, e: (0, 0)),
        ],
        out_specs=pl.BlockSpec((tm, D_MODEL), lambda i, e: (i, 0)),
        out_shape=jax.ShapeDtypeStruct((t, D_MODEL), F32),
        scratch_shapes=[pltpu.VMEM((tm, D_MODEL), BF16), pltpu.VMEM((tm, D_MODEL), F32)],
        compiler_params=_cparams("arbitrary", "arbitrary"),
        name="moe_experts",
    )(x, mods_l, mods_l, mods_l, comb, w_gate, w_up, w_down, g.reshape(1, D_MODEL), b.reshape(1, D_MODEL))


def kernel(x_prompt, x_sample, state_gdn, cache_gqa_k, cache_gqa_v, cache_mla_ckv, cache_mla_krope, c, c_ctx, w_ada, b_ada, ln_mix_g, ln_mix_b, ln_ffn_g, ln_ffn_b, gdn_w_in, gdn_conv, gdn_a_log, gdn_dt_bias, gdn_norm_g, gdn_w_out, gqa_w_in, gqa_q_norm, gqa_k_norm, gqa_w_out, mla_w_in, mla_q_norm, mla_kv_norm, mla_w_q_up, mla_w_kv_up, mla_w_out, router_w, router_b, moe_w_gate, moe_w_up, moe_w_down):
    x = jnp.concatenate([x_prompt.reshape(T_CTX, D_MODEL), x_sample.reshape(T_LAT, D_MODEL)], axis=0)
    cond = jnp.concatenate([c_ctx[None], c, jnp.zeros((MOD_ROWS - 1 - DEC_BATCH, D_MODEL), F32)], axis=0)
    mods = _ada_call(cond, w_ada, b_ada)

    st_gdn, st_k, st_v, st_ckv, st_krope = [], [], [], [], []
    for l in range(DEPTH):
        kind, li = l % N_MIXERS, l // N_MIXERS
        mods_l = mods[l]
        if kind == 0:
            o, st = _gdn_layer(x, mods_l, li, state_gdn, gdn_w_in, gdn_conv, gdn_a_log, gdn_dt_bias, gdn_norm_g)
            st_gdn.append(st)
            w_out = gdn_w_out[li]
        elif kind == 1:
            o, nk, nv = _gqa_layer(x, mods_l, li, cache_gqa_k, cache_gqa_v, gqa_w_in, gqa_q_norm, gqa_k_norm)
            st_k.append(nk)
            st_v.append(nv)
            w_out = gqa_w_out[li]
        else:
            o, nckv, nkr = _mla_layer(x, mods_l, li, cache_mla_ckv, cache_mla_krope, mla_w_in, mla_q_norm,
                                      mla_kv_norm, mla_w_q_up, mla_w_kv_up)
            st_ckv.append(nckv)
            st_krope.append(nkr)
            w_out = mla_w_out[li]
        x = _outln_call(o, w_out, x, mods_l, 2, ln_mix_g[l], ln_mix_b[l], name="mix_out_ln")
        comb = _router_call(x, mods_l, router_w, router_b)
        x = _moe_call(x, mods_l, comb, moe_w_gate[l], moe_w_up[l], moe_w_down[l], ln_ffn_g[l], ln_ffn_b[l])

    y_prompt = x[:T_CTX].reshape(BATCH, SEQ, D_MODEL)
    y_sample = x[T_CTX:].reshape(DEC_BATCH, DEC_SEQ, D_MODEL)
    return (y_prompt, y_sample, jnp.stack(st_gdn, axis=1), jnp.stack(st_k, axis=1), jnp.stack(st_v, axis=1),
            jnp.stack(st_ckv, axis=1), jnp.stack(st_krope, axis=1))
```

```python
import functools
import math

import jax
import jax.numpy as jnp
from jax import lax
from jax.experimental import pallas as pl
from jax.experimental.pallas import tpu as pltpu

F32 = jnp.float32
BF16 = jnp.bfloat16

D_MODEL = 1024
BATCH = 32
SEQ = 256
DEPTH = 4
DEC_BATCH = 2
DEC_SEQ = 2048
PAST_LEN = 512
GRID_W = 64
N_MIXERS = 3

GDN_HEADS = 8
GDN_DK = 128
GDN_DV = 128
GDN_CONV = 5
GDN_CHUNK = 64
GDN_QKV = GDN_HEADS * (2 * GDN_DK + GDN_DV)
GDN_QKVZ = GDN_QKV + GDN_HEADS * GDN_DV

GQA_HEADS = 8
GQA_KV_HEADS = 2
GQA_GROUP = GQA_HEADS // GQA_KV_HEADS
GQA_HEAD_DIM = 128
GQA_IN = (GQA_HEADS + 2 * GQA_KV_HEADS) * GQA_HEAD_DIM

MLA_HEADS = 8
MLA_Q_LORA = 512
MLA_KV_LORA = 256
MLA_NOPE = 128
MLA_ROPE = 64
MLA_V = 128
MLA_IN = MLA_Q_LORA + MLA_KV_LORA + MLA_ROPE

N_EXPERTS = 16
N_GROUPS = 4
EXPERTS_PER_GROUP = N_EXPERTS // N_GROUPS
D_EXPERT = 256

ROPE_THETA = 10000.0
ALPHA = (2 * DEPTH) ** 0.25
NORM_EPS = 1e-6

T_CTX = BATCH * SEQ
T_LAT = DEC_BATCH * DEC_SEQ
T_ALL = T_CTX + T_LAT
SEG_ROWS = DEC_SEQ
N_CTX_SEGS = T_CTX // SEG_ROWS
MOD_ROWS = 8

V7X_LANES = 128
V7X_SUBLANES = 8
V7X_VMEM_BYTES = 64 * 1024 * 1024
VMEM_LIMIT = V7X_VMEM_BYTES * 7 // 8

GDN_SUPER = 256
CHUNKS_PER_SUPER = GDN_SUPER // GDN_CHUNK


def _cparams(*sem):
    return pltpu.CompilerParams(dimension_semantics=sem, vmem_limit_bytes=VMEM_LIMIT)


def _mm(a, b):
    return jnp.dot(a.astype(BF16), b.astype(BF16), preferred_element_type=F32)


def _mm_nt(a, b):
    return lax.dot_general(a.astype(BF16), b.astype(BF16), (((1,), (1,)), ((), ())),
                           preferred_element_type=F32)


def _silu(x):
    return x * (1.0 / (1.0 + jnp.exp(-x)))


def _sigmoid(x):
    return 1.0 / (1.0 + jnp.exp(-x))


def _chunk_of(idx):
    return idx >> int(math.log2(GDN_CHUNK))


def _seg_row(i, tiles_per_seg):
    return jnp.maximum(lax.div(i, tiles_per_seg) - (N_CTX_SEGS - 1), 0)


def _layer_norm(y, g, b):
    mu = jnp.mean(y, axis=-1, keepdims=True)
    yc = y - mu
    var = jnp.mean(yc * yc, axis=-1, keepdims=True)
    return yc * lax.rsqrt(var + NORM_EPS) * g + b


def _rms(x, g):
    return x * lax.rsqrt(jnp.mean(x * x, axis=-1, keepdims=True) + NORM_EPS) * g


def _ada_kernel(c_ref, w_ref, b_ref, o_ref):
    o_ref[...] = _mm(_silu(c_ref[...]), w_ref[...]) + b_ref[...]


def _ada_call(cond, w_ada, b_ada):
    tn = 1536
    n = w_ada.shape[-1]
    return pl.pallas_call(
        _ada_kernel,
        grid=(DEPTH, n // tn),
        in_specs=[
            pl.BlockSpec((MOD_ROWS, D_MODEL), lambda l, j: (0, 0)),
            pl.BlockSpec((None, D_MODEL, tn), lambda l, j: (l, 0, j)),
            pl.BlockSpec((None, 1, tn), lambda l, j: (l, 0, j)),
        ],
        out_specs=pl.BlockSpec((None, MOD_ROWS, tn), lambda l, j: (l, 0, j)),
        out_shape=jax.ShapeDtypeStruct((DEPTH, MOD_ROWS, n), F32),
        compiler_params=_cparams("arbitrary", "arbitrary"),
        name="ada_mod",
    )(cond, w_ada, b_ada.reshape(DEPTH, 1, n))


def _modproj_kernel(x_ref, sh_ref, sc_ref, w_ref, *rest, tiles_per_seg, has_extra):
    if has_extra:
        wx_ref, o_ref, ox_ref, h_scr = rest
    else:
        o_ref, h_scr = rest
    i, j = pl.program_id(0), pl.program_id(1)

    @pl.when(j == 0)
    def _():
        r = _seg_row(i, tiles_per_seg)
        sh = sh_ref[pl.ds(r, 1), :]
        sc = sc_ref[pl.ds(r, 1), :]
        h = (x_ref[...] * (1.0 + sc) + sh).astype(BF16)
        h_scr[...] = h
        if has_extra:
            ox_ref[...] = _mm(h, wx_ref[...])

    o_ref[...] = _mm(h_scr[...], w_ref[...])


def _modproj_call(x, mods_l, shift_idx, scale_idx, w, n_out, w_extra=None, *, tm=1024, tn=512, name):
    t = x.shape[0]
    has_extra = w_extra is not None
    in_specs = [
        pl.BlockSpec((tm, D_MODEL), lambda i, j: (i, 0)),
        pl.BlockSpec((MOD_ROWS, D_MODEL), lambda i, j: (0, shift_idx)),
        pl.BlockSpec((MOD_ROWS, D_MODEL), lambda i, j: (0, scale_idx)),
        pl.BlockSpec((D_MODEL, tn), lambda i, j: (0, j)),
    ]
    out_specs = [pl.BlockSpec((tm, tn), lambda i, j: (i, j))]
    out_shape = [jax.ShapeDtypeStruct((t, n_out), F32)]
    args = [x, mods_l, mods_l, w]
    if has_extra:
        nx = w_extra.shape[1]
        in_specs.append(pl.BlockSpec((D_MODEL, nx), lambda i, j: (0, 0)))
        out_specs.append(pl.BlockSpec((tm, nx), lambda i, j: (i, 0)))
        out_shape.append(jax.ShapeDtypeStruct((t, nx), F32))
        args.append(w_extra)
    res = pl.pallas_call(
        functools.partial(_modproj_kernel, tiles_per_seg=SEG_ROWS // tm, has_extra=has_extra),
        grid=(t // tm, n_out // tn),
        in_specs=in_specs,
        out_specs=out_specs,
        out_shape=out_shape,
        scratch_shapes=[pltpu.VMEM((tm, D_MODEL), BF16)],
        compiler_params=_cparams("arbitrary", "arbitrary"),
        name=name,
    )(*args)
    return res if has_extra else res[0]


def _rmsproj_kernel(x_ref, g_ref, w_ref, *rest, normalize, emit_xn):
    if emit_xn:
        o_ref, xn_ref, h_scr = rest
    else:
        o_ref, h_scr = rest
    j = pl.program_id(1)

    @pl.when(j == 0)
    def _():
        x = x_ref[...]
        xn = _rms(x, g_ref[...]) if normalize else x
        h_scr[...] = xn.astype(BF16)
        if emit_xn:
            xn_ref[...] = xn

    o_ref[...] = _mm(h_scr[...], w_ref[...])


def _rmsproj_call(x, col_block, k, g, w, *, normalize=True, emit_xn=False, tm=1024, tn=512, name):
    t = x.shape[0]
    n_out = w.shape[1]
    out_specs = [pl.BlockSpec((tm, tn), lambda i, j: (i, j))]
    out_shape = [jax.ShapeDtypeStruct((t, n_out), F32)]
    if emit_xn:
        out_specs.append(pl.BlockSpec((tm, k), lambda i, j: (i, 0)))
        out_shape.append(jax.ShapeDtypeStruct((t, k), F32))
    res = pl.pallas_call(
        functools.partial(_rmsproj_kernel, normalize=normalize, emit_xn=emit_xn),
        grid=(t // tm, n_out // tn),
        in_specs=[
            pl.BlockSpec((tm, k), lambda i, j: (i, col_block)),
            pl.BlockSpec((1, k), lambda i, j: (0, 0)),
            pl.BlockSpec((k, tn), lambda i, j: (0, j)),
        ],
        out_specs=out_specs,
        out_shape=out_shape,
        scratch_shapes=[pltpu.VMEM((tm, k), BF16)],
        compiler_params=_cparams("arbitrary", "arbitrary"),
        name=name,
    )(x, g.reshape(1, k), w)
    return res if emit_xn else res[0]


def _outln_kernel(o_ref, w_ref, x_ref, gate_ref, g_ref, b_ref, out_ref, wb_scr, *, tiles_per_seg):
    i = pl.program_id(0)

    @pl.when(i == 0)
    def _():
        wb_scr[...] = w_ref[...].astype(BF16)

    gate = gate_ref[pl.ds(_seg_row(i, tiles_per_seg), 1), :]
    f = jnp.dot(o_ref[...].astype(BF16), wb_scr[...], preferred_element_type=F32)
    out_ref[...] = _layer_norm(ALPHA * x_ref[...] + (1.0 + gate) * f, g_ref[...], b_ref[...])


def _outln_call(o, w, x, mods_l, gate_idx, g, b, *, tm=512, name):
    t, k = o.shape
    return pl.pallas_call(
        functools.partial(_outln_kernel, tiles_per_seg=SEG_ROWS // tm),
        grid=(t // tm,),
        in_specs=[
            pl.BlockSpec((tm, k), lambda i: (i, 0)),
            pl.BlockSpec((k, D_MODEL), lambda i: (0, 0)),
            pl.BlockSpec((tm, D_MODEL), lambda i: (i, 0)),
            pl.BlockSpec((MOD_ROWS, D_MODEL), lambda i: (0, gate_idx)),
            pl.BlockSpec((1, D_MODEL), lambda i: (0, 0)),
            pl.BlockSpec((1, D_MODEL), lambda i: (0, 0)),
        ],
        out_specs=pl.BlockSpec((tm, D_MODEL), lambda i: (i, 0)),
        out_shape=jax.ShapeDtypeStruct((t, D_MODEL), F32),
        scratch_shapes=[pltpu.VMEM((k, D_MODEL), BF16)],
        compiler_params=_cparams("arbitrary"),
        name=name,
    )(o, w, x, mods_l, g.reshape(1, D_MODEL), b.reshape(1, D_MODEL))


def _gdn_gates_kernel(ab_ref, alog_ref, dt_ref, o_ref):
    ab = ab_ref[...]
    tm = ab.shape[0]
    x = ab + dt_ref[...]
    softplus = jnp.maximum(x, 0.0) + jnp.log(1.0 + jnp.exp(-jnp.abs(x)))
    g = -jnp.exp(alog_ref[...]) * softplus
    ri = lax.broadcasted_iota(jnp.int32, (tm, tm), 0)
    ci = lax.broadcasted_iota(jnp.int32, (tm, tm), 1)
    same = _chunk_of(ri) == _chunk_of(ci)
    lower = jnp.where(same & (ri >= ci), 1.0, 0.0).astype(F32)
    upper = jnp.where(same & (ri <= ci), 1.0, 0.0).astype(F32)
    pre = jnp.dot(lower, g, precision=lax.Precision.HIGHEST, preferred_element_type=F32)
    suf = jnp.dot(upper, g, precision=lax.Precision.HIGHEST, preferred_element_type=F32)
    lane = lax.broadcasted_iota(jnp.int32, ab.shape, 1)
    o_ref[...] = jnp.where(lane < GDN_HEADS, pre, jnp.where(lane < 2 * GDN_HEADS, suf, _sigmoid(ab)))


def _gdn_gates_call(ab, a_log, dt_bias):
    t, n = ab.shape
    tm = GDN_SUPER
    pad = n - 2 * GDN_HEADS
    alog_row = jnp.pad(a_log.reshape(1, -1), ((0, 0), (0, pad)))
    dt_row = jnp.pad(dt_bias.reshape(1, -1), ((0, 0), (0, pad)))
    return pl.pallas_call(
        _gdn_gates_kernel,
        grid=(t // tm,),
        in_specs=[
            pl.BlockSpec((tm, n), lambda i: (i, 0)),
            pl.BlockSpec((1, n), lambda i: (0, 0)),
            pl.BlockSpec((1, n), lambda i: (0, 0)),
        ],
        out_specs=pl.BlockSpec((tm, n), lambda i: (i, 0)),
        out_shape=jax.ShapeDtypeStruct((t, n), F32),
        compiler_params=_cparams("arbitrary"),
        name="gdn_gates",
    )(ab, alog_row, dt_row)


def _loop(n, body):
    if n <= 4:
        for i in range(n):
            body(i)
    else:
        def wrapped(i, carry):
            body(i)
            return carry
        lax.fori_loop(0, n, wrapped, 0)


def _gdn_core_kernel(q_ref, k_ref, v_ref, z_ref, cq_ref, ck_ref, cv_ref, gc_ref, gr_ref, ng_ref, *rest,
                     n_tok, has_s0, hps):
    if has_s0:
        s0_ref, o_ref, st_ref = rest[:3]
        scr = rest[3:]
    else:
        o_ref, st_ref = rest[:2]
        scr = rest[2:]
    pad_scr, kn_scr, u_scr, w_scr, qe_scr, qk_scr, oo_scr, s_scr = scr
    n_super = n_tok // GDN_SUPER
    n_chunk = n_tok // GDN_CHUNK
    halo = V7X_SUBLANES
    sb = GDN_SUPER
    ck = GDN_CHUNK
    dk = GDN_DK

    zeros_halo = jnp.zeros((halo, dk), F32)
    for hd in range(hps):
        for a, src in enumerate((q_ref, k_ref, v_ref)):
            pad_scr[3 * hd + a, 0:halo, :] = zeros_halo
            pad_scr[3 * hd + a, halo:halo + n_tok, :] = src[:, hd * dk:(hd + 1) * dk]
            pad_scr[3 * hd + a, halo + n_tok:2 * halo + n_tok, :] = zeros_halo
        for d in range(2):
            s_scr[2 * hd + d] = s0_ref[d, hd] if has_s0 else jnp.zeros((dk, GDN_DV), F32)

    ri = lax.broadcasted_iota(jnp.int32, (sb, sb), 0)
    ci = lax.broadcasted_iota(jnp.int32, (sb, sb), 1)
    same_chunk = _chunk_of(ri) == _chunk_of(ci)
    eye = jnp.where(ri == ci, 1.0, 0.0).astype(F32)
    n_levels = int(math.log2(ck))

    def level_mask(l):
        return ((ri >> (l + 1)) == (ci >> (l + 1))) & ((ri >> l) != (ci >> l))

    def conv_silu(a, w_ref, hd, r0):
        xp = pad_scr[a, pl.ds(r0, sb + 2 * halo), :]
        acc = None
        for j in range(GDN_CONV):
            shift = (GDN_CONV // 2 - j) % (sb + 2 * halo)
            tap = xp if shift == 0 else pltpu.roll(xp, shift, 0)
            term = w_ref[j:j + 1, hd * dk:(hd + 1) * dk] * tap[halo:halo + sb, :]
            acc = term if acc is None else acc + term
        return _silu(acc)

    chains = [(hd, d) for hd in range(hps) for d in range(2)]

    def phase_a(s):
        r0 = pl.multiple_of(s * sb, sb)
        qs, ks, vs, ps, qks = [], [], [], [], []
        for hd in range(hps):
            q = conv_silu(3 * hd, cq_ref, hd, r0)
            k = conv_silu(3 * hd + 1, ck_ref, hd, r0)
            v = conv_silu(3 * hd + 2, cv_ref, hd, r0)
            q = q * lax.rsqrt(jnp.sum(q * q, axis=-1, keepdims=True) + NORM_EPS) * (dk ** -0.5)
            k = k * lax.rsqrt(jnp.sum(k * k, axis=-1, keepdims=True) + NORM_EPS)
            kn_scr[hd, pl.ds(r0, sb), :] = k
            kb = k.astype(BF16)
            qs.append(q)
            ks.append(k)
            vs.append(v)
            ps.append(_mm_nt(kb, kb))
            qks.append(_mm_nt(q, kb))
        a_mats, ts, rhs = [], [], []
        for hd, d in chains:
            hdd = 2 * hd + d
            gcol = gc_ref[hd, pl.ds(r0, sb), d:d + 1]
            bcol = gc_ref[hd, pl.ds(r0, sb), 2 + d:3 + d]
            grow = gr_ref[hd, s, d:d + 1, :]
            if d == 0:
                mask, strict = same_chunk & (ri >= ci), same_chunk & (ri > ci)
            else:
                mask, strict = same_chunk & (ri <= ci), same_chunk & (ri < ci)
            dec = jnp.exp(jnp.where(mask, gcol - grow, -jnp.inf))
            a = jnp.where(strict, ps[hd] * bcol * dec, 0.0)
            a_mats.append(a)
            ts.append(eye - jnp.where(level_mask(0), a, 0.0))
            eg = jnp.exp(gcol)
            rhs.append(jnp.concatenate([vs[hd] * bcol, ks[hd] * (bcol * eg)], axis=1).astype(BF16))
            qe_scr[hdd, pl.ds(r0, sb), :] = qs[hd] * eg
            qkd = qks[hd] * dec
            for c in range(CHUNKS_PER_SUPER):
                qk_scr[hdd, pl.ds(r0 + c * ck, ck), :] = qkd[c * ck:(c + 1) * ck, c * ck:(c + 1) * ck]
        for l in range(1, n_levels):
            ys = [_mm(jnp.where(level_mask(l), a_mats[i], 0.0), ts[i]) for i in range(len(chains))]
            ts = [ts[i] - _mm(ts[i], ys[i]) for i in range(len(chains))]
        for i, (hd, d) in enumerate(chains):
            uw = _mm(ts[i], rhs[i])
            u_scr[2 * hd + d, pl.ds(r0, sb), :] = uw[:, :GDN_DV]
            w_scr[2 * hd + d, pl.ds(r0, sb), :] = uw[:, GDN_DV:]

    _loop(n_super, phase_a)

    def phase_b(i):
        rows, ws_qs, v_new, kws, gls = [], [], [], [], []
        for hd, d in chains:
            hdd = 2 * hd + d
            c = i if d == 0 else n_chunk - 1 - i
            r = pl.multiple_of(c * ck, ck)
            r_last = r + (ck - 1) if d == 0 else r
            rows.append(r)
            gcol = gc_ref[hd, pl.ds(r, ck), d:d + 1]
            gl = gc_ref[hd, pl.ds(r_last, 1), d:d + 1]
            gls.append(gl)
            kws.append((kn_scr[hd, pl.ds(r, ck), :] * jnp.exp(gl - gcol)).T)
            wq = jnp.concatenate([w_scr[hdd, pl.ds(r, ck), :], qe_scr[hdd, pl.ds(r, ck), :]], axis=0)
            ws_qs.append(_mm(wq, s_scr[hdd]))
        for j, (hd, d) in enumerate(chains):
            hdd = 2 * hd + d
            vn = u_scr[hdd, pl.ds(rows[j], ck), :] - ws_qs[j][:ck]
            v_new.append(vn)
            oo_scr[hdd, pl.ds(rows[j], ck), :] = ws_qs[j][ck:] + _mm(qk_scr[hdd, pl.ds(rows[j], ck), :], vn)
        for j, (hd, d) in enumerate(chains):
            hdd = 2 * hd + d
            s_scr[hdd] = s_scr[hdd] * jnp.exp(gls[j]) + _mm(kws[j], v_new[j])

    _loop(n_chunk, phase_b)

    def finish(s):
        r0 = pl.multiple_of(s * sb, sb)
        for hd in range(hps):
            o = oo_scr[2 * hd, pl.ds(r0, sb), :] + oo_scr[2 * hd + 1, pl.ds(r0, sb), :]
            z = z_ref[pl.ds(r0, sb), hd * GDN_DV:(hd + 1) * GDN_DV]
            o_ref[pl.ds(r0, sb), hd * GDN_DV:(hd + 1) * GDN_DV] = _rms(o, ng_ref[...]) * _silu(z)

    _loop(n_super, finish)
    for hd in range(hps):
        for d in range(2):
            st_ref[d, hd] = s_scr[2 * hd + d]


def _gdn_core_call(qkvz, conv_w, gc, gr, norm_g, s0, *, batch, n_tok, row_block_off, hps, name):
    has_s0 = s0 is not None
    h = GDN_HEADS
    dk = GDN_DK
    n_super = n_tok // GDN_SUPER
    hb = h // hps

    mode = dict(pipeline_mode=pl.Buffered(1)) if n_tok > GDN_SUPER else {}

    def col(sec):
        return pl.BlockSpec((n_tok, dk * hps), lambda b, hh: (b + row_block_off, sec * hb + hh), **mode)

    def cw(sec):
        return pl.BlockSpec((GDN_CONV, dk * hps), lambda b, hh: (0, sec * hb + hh))

    in_specs = [col(0), col(1), col(2), col(3), cw(0), cw(1), cw(2),
                pl.BlockSpec((None, hps, n_tok, 4), lambda b, hh: (b, hh, 0, 0)),
                pl.BlockSpec((None, hps, n_super, 4, GDN_SUPER), lambda b, hh: (b, hh, 0, 0, 0)),
                pl.BlockSpec((1, GDN_DV), lambda b, hh: (0, 0))]
    args = [qkvz, qkvz, qkvz, qkvz, conv_w, conv_w, conv_w, gc, gr, norm_g.reshape(1, GDN_DV)]
    st_spec = pl.BlockSpec((None, 2, hps, dk, GDN_DV), lambda b, hh: (b, 0, hh, 0, 0))
    if has_s0:
        in_specs.append(st_spec)
        args.append(s0)
    return pl.pallas_call(
        functools.partial(_gdn_core_kernel, n_tok=n_tok, has_s0=has_s0, hps=hps),
        grid=(batch, hb),
        in_specs=in_specs,
        out_specs=[pl.BlockSpec((n_tok, GDN_DV * hps), lambda b, hh: (b, hh)), st_spec],
        out_shape=[jax.ShapeDtypeStruct((batch * n_tok, h * GDN_DV), F32),
                   jax.ShapeDtypeStruct((batch, 2, h, dk, GDN_DV), F32)],
        scratch_shapes=[
            pltpu.VMEM((3 * hps, n_tok + 2 * V7X_SUBLANES, dk), F32),
            pltpu.VMEM((hps, n_tok, dk), F32),
            pltpu.VMEM((2 * hps, n_tok, GDN_DV), F32),
            pltpu.VMEM((2 * hps, n_tok, dk), F32),
            pltpu.VMEM((2 * hps, n_tok, dk), F32),
            pltpu.VMEM((2 * hps, n_tok, GDN_CHUNK), F32),
            pltpu.VMEM((2 * hps, n_tok, GDN_DV), F32),
            pltpu.VMEM((2 * hps, dk, GDN_DV), F32),
        ],
        compiler_params=_cparams("arbitrary", "arbitrary"),
        name=name,
    )(*args)


def _gdn_layer(x, mods_l, li, state_gdn, gdn_w_in, gdn_conv, gdn_a_log, gdn_dt_bias, gdn_norm_g):
    w_in = gdn_w_in[li]
    w_ab = jnp.pad(w_in[:, GDN_QKVZ:], ((0, 0), (0, V7X_LANES - 4 * GDN_HEADS)))
    qkvz, ab = _modproj_call(x, mods_l, 0, 1, w_in, GDN_QKVZ, w_ab, name="gdn_in")
    gates = _gdn_gates_call(ab, gdn_a_log[li], gdn_dt_bias[li])[:, :4 * GDN_HEADS]

    def head_layouts(g, batch, n_tok):
        g = g.reshape(batch, n_tok, 4, GDN_HEADS)
        gc = jnp.transpose(g, (0, 3, 1, 2))
        gr = jnp.transpose(g, (0, 3, 2, 1)).reshape(batch, GDN_HEADS, 4, n_tok // GDN_SUPER, GDN_SUPER)
        return gc, jnp.transpose(gr, (0, 1, 3, 2, 4))

    gc_p, gr_p = head_layouts(gates[:T_CTX], BATCH, SEQ)
    gc_s, gr_s = head_layouts(gates[T_CTX:], DEC_BATCH, DEC_SEQ)
    o_p, st_p = _gdn_core_call(qkvz, gdn_conv[li], gc_p, gr_p, gdn_norm_g[li], None,
                               batch=BATCH, n_tok=SEQ, row_block_off=0, hps=4, name="gdn_core_ctx")
    o_s, _ = _gdn_core_call(qkvz, gdn_conv[li], gc_s, gr_s, gdn_norm_g[li], state_gdn[:, li],
                            batch=DEC_BATCH, n_tok=DEC_SEQ, row_block_off=T_CTX // DEC_SEQ, hps=2,
                            name="gdn_core_lat")
    return jnp.concatenate([o_p, o_s], axis=0), st_p


def _rope_tables(n_tokens, rot_dim, reps):
    n_rows = n_tokens // GRID_W
    row = jnp.repeat(jnp.arange(n_rows, dtype=F32), GRID_W)
    colp = jnp.tile(jnp.arange(GRID_W, dtype=F32), n_rows)
    n_freq = rot_dim // 4
    inv_freq = ROPE_THETA ** (-jnp.arange(n_freq, dtype=F32) / n_freq)
    ra, ca = row[:, None] * inv_freq, colp[:, None] * inv_freq
    cos = jnp.concatenate([jnp.cos(ra), jnp.cos(ra), jnp.cos(ca), jnp.cos(ca)], axis=-1)
    sin = jnp.concatenate([-jnp.sin(ra), jnp.sin(ra), -jnp.sin(ca), jnp.sin(ca)], axis=-1)
    return jnp.tile(cos, (1, reps)), jnp.tile(sin, (1, reps))


def _rope(x, cos, sin, quarter):
    n = x.shape[-1]
    lane = lax.broadcasted_iota(jnp.int32, x.shape, x.ndim - 1)
    first = (lane & (2 * quarter - 1)) < quarter
    partner = jnp.where(first, pltpu.roll(x, n - quarter, x.ndim - 1), pltpu.roll(x, quarter, x.ndim - 1))
    return x * cos + partner * sin


def _gqa_prep_kernel(kv_ref, g_ref, *rest, rope):
    if rope:
        cos_ref, sin_ref, k_ref, v_ref = rest
    else:
        k_ref, v_ref = rest
    kv = kv_ref[...]
    nk = GQA_KV_HEADS * GQA_HEAD_DIM
    for hh in range(GQA_KV_HEADS):
        k = _rms(kv[:, hh * GQA_HEAD_DIM:(hh + 1) * GQA_HEAD_DIM], g_ref[...])
        if rope:
            k = _rope(k, cos_ref[...], sin_ref[...], GQA_HEAD_DIM // 4)
        k_ref[:, hh * GQA_HEAD_DIM:(hh + 1) * GQA_HEAD_DIM] = k
    v_ref[...] = kv[:, nk:]


def _gqa_prep_call(proj, k_norm, row_off, n_rows, tables, *, tm=512, name):
    rope = tables is not None
    nkv = 2 * GQA_KV_HEADS * GQA_HEAD_DIM
    kv_col = GQA_HEADS * GQA_HEAD_DIM // nkv
    off = row_off // tm
    in_specs = [pl.BlockSpec((tm, nkv), lambda i: (i + off, kv_col)),
                pl.BlockSpec((1, GQA_HEAD_DIM), lambda i: (0, 0))]
    args = [proj, k_norm.reshape(1, GQA_HEAD_DIM)]
    if rope:
        per_seq = DEC_SEQ // tm
        in_specs += [pl.BlockSpec((tm, GQA_HEAD_DIM), lambda i: (i % per_seq, 0))] * 2
        args += list(tables)
    half = nkv // 2
    return pl.pallas_call(
        functools.partial(_gqa_prep_kernel, rope=rope),
        grid=(n_rows // tm,),
        in_specs=in_specs,
        out_specs=[pl.BlockSpec((tm, half), lambda i: (i, 0))] * 2,
        out_shape=[jax.ShapeDtypeStruct((n_rows, half), F32)] * 2,
        compiler_params=_cparams("arbitrary"),
        name=name,
    )(*args)


def _softmax_av(scores, values):
    m = None
    for s in scores:
        sm = jnp.max(s, axis=-1, keepdims=True)
        m = sm if m is None else jnp.maximum(m, sm)
    den = None
    acc = None
    for s, v in zip(scores, values):
        p = jnp.exp(s - m)
        ps = jnp.sum(p, axis=-1, keepdims=True)
        pv = _mm(p, v)
        den = ps if den is None else den + ps
        acc = pv if acc is None else acc + pv
    return acc / den


def _gqa_attn_kernel(q_ref, qg_ref, *rest, n_seg, rope, kv_per_step):
    if rope:
        cos_ref, sin_ref = rest[:2]
        rest = rest[2:]
    seg_refs = rest[:2 * n_seg]
    o_ref = rest[2 * n_seg]
    dh = GQA_HEAD_DIM
    scale = dh ** -0.5
    tq = q_ref.shape[0]
    for kh in range(kv_per_step):
        qs = []
        for g in range(GQA_GROUP):
            c0 = (kh * GQA_GROUP + g) * dh
            qh = _rms(q_ref[:, c0:c0 + dh], qg_ref[...])
            if rope:
                qh = _rope(qh, cos_ref[...], sin_ref[...], dh // 4)
            qs.append(qh.astype(BF16))
        qst = jnp.concatenate(qs, axis=0)
        scores, values = [], []
        for sgi in range(n_seg):
            kk = seg_refs[2 * sgi][:, kh * dh:(kh + 1) * dh]
            scores.append(_mm_nt(qst, kk) * scale)
            values.append(seg_refs[2 * sgi + 1][:, kh * dh:(kh + 1) * dh])
        o = _softmax_av(scores, values)
        for g in range(GQA_GROUP):
            c0 = (kh * GQA_GROUP + g) * dh
            o_ref[:, c0:c0 + dh] = o[g * tq:(g + 1) * tq]


def _gqa_attn_call(proj, q_norm, segs, tables, *, batch, n_q, row_off, tq, kv_per_step, name):
    rope = tables is not None
    dh = GQA_HEAD_DIM
    qw = GQA_GROUP * dh * kv_per_step
    kw = dh * kv_per_step
    nqb = n_q // tq
    off = row_off // tq
    in_specs = [pl.BlockSpec((tq, qw), lambda b, h, i: (b * nqb + i + off, h)),
                pl.BlockSpec((1, dh), lambda b, h, i: (0, 0))]
    args = [proj, q_norm.reshape(1, dh)]
    if rope:
        in_specs += [pl.BlockSpec((tq, dh), lambda b, h, i: (i, 0))] * 2
        args += list(tables)
    for k2d, v2d, rows in segs:
        in_specs += [pl.BlockSpec((rows, kw), lambda b, h, i: (b, h))] * 2
        args += [k2d, v2d]
    return pl.pallas_call(
        functools.partial(_gqa_attn_kernel, n_seg=len(segs), rope=rope, kv_per_step=kv_per_step),
        grid=(batch, GQA_KV_HEADS // kv_per_step, nqb),
        in_specs=in_specs,
        out_specs=pl.BlockSpec((tq, qw), lambda b, h, i: (b * nqb + i, h)),
        out_shape=jax.ShapeDtypeStruct((batch * n_q, GQA_HEADS * dh), F32),
        compiler_params=_cparams("arbitrary", "arbitrary", "arbitrary"),
        name=name,
    )(*args)


def _gqa_layer(x, mods_l, li, cache_k, cache_v, gqa_w_in, gqa_q_norm, gqa_k_norm):
    proj = _modproj_call(x, mods_l, 0, 1, gqa_w_in[li], GQA_IN, name="gqa_in")
    tables = _rope_tables(DEC_SEQ, GQA_HEAD_DIM, 1)
    nkv = GQA_KV_HEADS * GQA_HEAD_DIM
    k_p, v_p = _gqa_prep_call(proj, gqa_k_norm[li], 0, T_CTX, None, name="gqa_prep_ctx")
    k_s, v_s = _gqa_prep_call(proj, gqa_k_norm[li], T_CTX, T_LAT, tables, name="gqa_prep_lat")
    o_p = _gqa_attn_call(proj, gqa_q_norm[li], [(k_p, v_p, SEQ)], None, batch=BATCH, n_q=SEQ, row_off=0,
                         tq=SEQ, kv_per_step=GQA_KV_HEADS, name="gqa_attn_ctx")
    ck = cache_k[:, li].reshape(DEC_BATCH * PAST_LEN, nkv)
    cv = cache_v[:, li].reshape(DEC_BATCH * PAST_LEN, nkv)
    o_s = _gqa_attn_call(proj, gqa_q_norm[li], [(ck, cv, PAST_LEN), (k_s, v_s, DEC_SEQ)], tables,
                         batch=DEC_BATCH, n_q=DEC_SEQ, row_off=T_CTX, tq=128, kv_per_step=1, name="gqa_attn_lat")
    new_k = k_p.reshape(BATCH, SEQ, GQA_KV_HEADS, GQA_HEAD_DIM)
    new_v = v_p.reshape(BATCH, SEQ, GQA_KV_HEADS, GQA_HEAD_DIM)
    return jnp.concatenate([o_p, o_s], axis=0), new_k, new_v


def _krope_kernel(x_ref, *rest, rope):
    if rope:
        cos_ref, sin_ref, o_ref = rest
    else:
        (o_ref,) = rest
    x = x_ref[...]
    x2 = jnp.concatenate([x, x], axis=1)
    if rope:
        x2 = _rope(x2, cos_ref[...], sin_ref[...], MLA_ROPE // 4)
    o_ref[...] = x2


def _krope_call(krope, tables, *, tm=512, name):
    rope = tables is not None
    n = krope.shape[0]
    in_specs = [pl.BlockSpec((tm, MLA_ROPE), lambda i: (i, 0))]
    args = [krope]
    if rope:
        per_seq = DEC_SEQ // tm
        in_specs += [pl.BlockSpec((tm, 2 * MLA_ROPE), lambda i: (i % per_seq, 0))] * 2
        args += list(tables)
    return pl.pallas_call(
        functools.partial(_krope_kernel, rope=rope),
        grid=(n // tm,),
        in_specs=in_specs,
        out_specs=pl.BlockSpec((tm, 2 * MLA_ROPE), lambda i: (i, 0)),
        out_shape=jax.ShapeDtypeStruct((n, 2 * MLA_ROPE), F32),
        compiler_params=_cparams("arbitrary"),
        name=name,
    )(*args)


def _mla_attn_kernel(qn_ref, qr_ref, *rest, n_seg, rope, heads_per_step):
    if rope:
        cos_ref, sin_ref = rest[:2]
        rest = rest[2:]
    seg_refs = rest[:2 * n_seg]
    o_ref = rest[2 * n_seg]
    scale = (MLA_NOPE + MLA_ROPE) ** -0.5
    h0 = pl.program_id(1) * heads_per_step
    pair_w = 2 * MLA_ROPE
    for j in range(heads_per_step):
        qn = qn_ref[:, j * MLA_NOPE:(j + 1) * MLA_NOPE]
        pair = qr_ref[:, (j // 2) * pair_w:(j // 2 + 1) * pair_w]
        if rope:
            pair = _rope(pair, cos_ref[...], sin_ref[...], MLA_ROPE // 4)
        lane = lax.broadcasted_iota(jnp.int32, pair.shape, 1)
        mine = (lane >> int(math.log2(MLA_ROPE))) == ((h0 + j) & 1)
        q = jnp.concatenate([qn, jnp.where(mine, pair, 0.0)], axis=1).astype(BF16)
        scores, values = [], []
        for sgi in range(n_seg):
            kv = seg_refs[2 * sgi][:, j * (MLA_NOPE + MLA_V):(j + 1) * (MLA_NOPE + MLA_V)]
            kmat = jnp.concatenate([kv[:, :MLA_NOPE], seg_refs[2 * sgi + 1][...]], axis=1)
            scores.append(_mm_nt(q, kmat) * scale)
            values.append(kv[:, MLA_NOPE:])
        o_ref[:, j * MLA_V:(j + 1) * MLA_V] = _softmax_av(scores, values)


def _mla_attn_call(q, segs, tables, *, batch, n_q, row_off, tq, heads_per_step, name):
    rope = tables is not None
    hps = heads_per_step
    nqb = n_q // tq
    off = row_off // tq
    rope_w = max(2 * MLA_ROPE, MLA_ROPE * hps)
    rope_base = MLA_HEADS * MLA_NOPE // rope_w

    def rope_idx(b, h, i):
        return (b * nqb + i + off, rope_base + (h * hps * MLA_ROPE) // rope_w)

    in_specs = [pl.BlockSpec((tq, MLA_NOPE * hps), lambda b, h, i: (b * nqb + i + off, h)),
                pl.BlockSpec((tq, rope_w), rope_idx)]
    args = [q, q]
    if rope:
        in_specs += [pl.BlockSpec((tq, 2 * MLA_ROPE), lambda b, h, i: (i, 0))] * 2
        args += list(tables)
    for kv2d, kr2, rows, kv_row_off in segs:
        kv_off = kv_row_off // rows
        in_specs += [pl.BlockSpec((rows, (MLA_NOPE + MLA_V) * hps), lambda b, h, i, kv_off=kv_off: (b + kv_off, h)),
                     pl.BlockSpec((rows, 2 * MLA_ROPE), lambda b, h, i: (b, 0))]
        args += [kv2d, kr2]
    return pl.pallas_call(
        functools.partial(_mla_attn_kernel, n_seg=len(segs), rope=rope, heads_per_step=hps),
        grid=(batch, MLA_HEADS // hps, nqb),
        in_specs=in_specs,
        out_specs=pl.BlockSpec((tq, MLA_V * hps), lambda b, h, i: (b * nqb + i, h)),
        out_shape=jax.ShapeDtypeStruct((batch * n_q, MLA_HEADS * MLA_V), F32),
        compiler_params=_cparams("arbitrary", "arbitrary", "arbitrary"),
        name=name,
    )(*args)


def _mla_layer(x, mods_l, li, cache_ckv, cache_krope, mla_w_in, mla_q_norm, mla_kv_norm, mla_w_q_up, mla_w_kv_up):
    w_in = jnp.pad(mla_w_in[li], ((0, 0), (0, 7 * V7X_LANES - MLA_IN)))
    proj = _modproj_call(x, mods_l, 0, 1, w_in, 7 * V7X_LANES, tn=7 * V7X_LANES, name="mla_in")
    wq = mla_w_q_up[li].reshape(MLA_Q_LORA, MLA_HEADS, MLA_NOPE + MLA_ROPE)
    wq = jnp.concatenate([wq[:, :, :MLA_NOPE].reshape(MLA_Q_LORA, -1), wq[:, :, MLA_NOPE:].reshape(MLA_Q_LORA, -1)], axis=1)
    q = _rmsproj_call(proj, 0, MLA_Q_LORA, mla_q_norm[li], wq, name="mla_q_up")
    kv, ckv = _rmsproj_call(proj, MLA_Q_LORA // MLA_KV_LORA, MLA_KV_LORA, mla_kv_norm[li], mla_w_kv_up[li],
                            emit_xn=True, name="mla_kv_up")
    krope = proj[:, MLA_Q_LORA + MLA_KV_LORA:MLA_IN]
    tables = _rope_tables(DEC_SEQ, MLA_ROPE, 2)
    kr2_p = _krope_call(krope[:T_CTX], None, name="mla_krope_ctx")
    kr2_s = _krope_call(krope[T_CTX:], tables, name="mla_krope_lat")
    ctx_ckv = cache_ckv[:, li].reshape(DEC_BATCH * PAST_LEN, MLA_KV_LORA)
    kv_c = _rmsproj_call(ctx_ckv, 0, MLA_KV_LORA, jnp.ones((MLA_KV_LORA,), F32), mla_w_kv_up[li], normalize=False,
                         name="mla_kv_up_cache")
    kr2_c = _krope_call(cache_krope[:, li].reshape(DEC_BATCH * PAST_LEN, MLA_ROPE), None, name="mla_krope_cache")
    o_p = _mla_attn_call(q, [(kv, kr2_p, SEQ, 0)], None, batch=BATCH, n_q=SEQ, row_off=0, tq=SEQ,
                         heads_per_step=MLA_HEADS, name="mla_attn_ctx")
    o_s = _mla_attn_call(q, [(kv_c, kr2_c, PAST_LEN, 0), (kv, kr2_s, DEC_SEQ, T_CTX)], tables, batch=DEC_BATCH,
                         n_q=DEC_SEQ, row_off=T_CTX, tq=512, heads_per_step=1, name="mla_attn_lat")
    new_ckv = ckv[:T_CTX].reshape(BATCH, SEQ, MLA_KV_LORA)
    new_krope = krope[:T_CTX].reshape(BATCH, SEQ, MLA_ROPE)
    return jnp.concatenate([o_p, o_s], axis=0), new_ckv, new_krope


def _router_kernel(x_ref, sh_ref, sc_ref, rw_ref, rb_ref, o_ref, *, tiles_per_seg):
    i = pl.program_id(0)
    r = _seg_row(i, tiles_per_seg)
    h = x_ref[...] * (1.0 + sc_ref[pl.ds(r, 1), :]) + sh_ref[pl.ds(r, 1), :]
    logits = lax.dot_general(rw_ref[...], h, (((1,), (1,)), ((), ())), precision=lax.Precision.HIGHEST,
                             preferred_element_type=F32)[:N_EXPERTS]
    scores = _sigmoid(logits)
    sel = scores + rb_ref[0:N_EXPERTS, :]
    tm = sel.shape[1]
    neg = -jnp.inf
    gs = []
    for g in range(N_GROUPS):
        a, b, c, d = (sel[EXPERTS_PER_GROUP * g + j:EXPERTS_PER_GROUP * g + j + 1, :] for j in range(4))
        gs.append(jnp.maximum(jnp.maximum(jnp.maximum(a + b, a + c), jnp.maximum(a + d, b + c)),
                              jnp.maximum(b + d, c + d)))
    gmax = jnp.maximum(jnp.maximum(gs[0], gs[1]), jnp.maximum(gs[2], gs[3]))
    best = jnp.where(gs[0] == gmax, 0, jnp.where(gs[1] == gmax, 1, jnp.where(gs[2] == gmax, 2, 3)))
    eid = lax.broadcasted_iota(jnp.int32, (N_EXPERTS, tm), 0)
    masked = jnp.where((eid >> int(math.log2(EXPERTS_PER_GROUP))) == best, sel, neg)
    m1 = jnp.max(masked, axis=0, keepdims=True)
    i1 = jnp.min(jnp.where(masked == m1, eid, N_EXPERTS), axis=0, keepdims=True)
    masked2 = jnp.where(eid == i1, neg, masked)
    m2 = jnp.max(masked2, axis=0, keepdims=True)
    i2 = jnp.min(jnp.where(masked2 == m2, eid, N_EXPERTS), axis=0, keepdims=True)
    w1 = jnp.sum(jnp.where(eid == i1, scores, 0.0), axis=0, keepdims=True)
    w2 = jnp.sum(jnp.where(eid == i2, scores, 0.0), axis=0, keepdims=True)
    tot = w1 + w2
    comb = jnp.where(eid == i1, w1 / tot, 0.0) + jnp.where(eid == i2, w2 / tot, 0.0)
    comb = jnp.concatenate([comb, jnp.zeros((V7X_LANES - N_EXPERTS, tm), F32)], axis=0)
    o_ref[...] = comb.T


def _router_call(x, mods_l, router_w, router_b, *, tm=512):
    t = x.shape[0]
    rw = jnp.pad(router_w.T, ((0, V7X_LANES - N_EXPERTS), (0, 0)))
    rb = jnp.pad(router_b.reshape(N_EXPERTS, 1), ((0, V7X_LANES - N_EXPERTS), (0, 0)))
    return pl.pallas_call(
        functools.partial(_router_kernel, tiles_per_seg=SEG_ROWS // tm),
        grid=(t // tm,),
        in_specs=[
            pl.BlockSpec((tm, D_MODEL), lambda i: (i, 0)),
            pl.BlockSpec((MOD_ROWS, D_MODEL), lambda i: (0, 3)),
            pl.BlockSpec((MOD_ROWS, D_MODEL), lambda i: (0, 4)),
            pl.BlockSpec((V7X_LANES, D_MODEL), lambda i: (0, 0)),
            pl.BlockSpec((V7X_LANES, 1), lambda i: (0, 0)),
        ],
        out_specs=pl.BlockSpec((tm, V7X_LANES), lambda i: (i, 0)),
        out_shape=jax.ShapeDtypeStruct((t, V7X_LANES), F32),
        compiler_params=_cparams("arbitrary"),
        name="moe_router",
    )(x, mods_l, mods_l, rw, rb)


def _moe_kernel(x_ref, sh_ref, sc_ref, gate_ref, comb_ref, wg_ref, wu_ref, wd_ref, g_ref, b_ref, o_ref,
                h_scr, acc_scr, *, tiles_per_seg):
    i, e = pl.program_id(0), pl.program_id(1)
    r = _seg_row(i, tiles_per_seg)

    @pl.when(e == 0)
    def _():
        h_scr[...] = (x_ref[...] * (1.0 + sc_ref[pl.ds(r, 1), :]) + sh_ref[pl.ds(r, 1), :]).astype(BF16)
        acc_scr[...] = jnp.zeros_like(acc_scr)

    comb = comb_ref[...]
    lane = lax.broadcasted_iota(jnp.int32, comb.shape, 1)
    c_e = jnp.sum(jnp.where(lane == e, comb, 0.0), axis=1, keepdims=True)
    h = h_scr[...]
    hid = _silu(_mm(h, wg_ref[...])) * _mm(h, wu_ref[...]) * c_e
    acc_scr[...] += _mm(hid, wd_ref[...])

    @pl.when(e == N_EXPERTS - 1)
    def _():
        gate = gate_ref[pl.ds(r, 1), :]
        o_ref[...] = _layer_norm(ALPHA * x_ref[...] + (1.0 + gate) * acc_scr[...], g_ref[...], b_ref[...])


def _moe_call(x, mods_l, comb, w_gate, w_up, w_down, g, b, *, tm=1024):
    t = x.shape[0]
    return pl.pallas_call(
        functools.partial(_moe_kernel, tiles_per_seg=SEG_ROWS // tm),
        grid=(t // tm, N_EXPERTS),
        in_specs=[
            pl.BlockSpec((tm, D_MODEL), lambda i, e: (i, 0)),
            pl.BlockSpec((MOD_ROWS, D_MODEL), lambda i, e: (0, 3)),
            pl.BlockSpec((MOD_ROWS, D_MODEL), lambda i, e: (0, 4)),
            pl.BlockSpec((MOD_ROWS, D_MODEL), lambda i, e: (0, 5)),
            pl.BlockSpec((tm, V7X_LANES), lambda i, e: (i, 0)),
            pl.BlockSpec((None, D_MODEL, D_EXPERT), lambda i, e: (e, 0, 0)),
            pl.BlockSpec((None, D_MODEL, D_EXPERT), lambda i, e: (e, 0, 0)),
            pl.BlockSpec((None, D_EXPERT, D_MODEL), lambda i, e: (e, 0, 0)),
            pl.BlockSpec((1, D_MODEL), lambda i, e: (0, 0)),
            pl.BlockSpec((1, D_MODEL), lambda i, e: (0, 0)),
        ],
        out_specs=pl.BlockSpec((tm, D_MODEL), lambda i, e: (i, 0)),
        out_shape=jax.ShapeDtypeStruct((t, D_MODEL), F32),
        scratch_shapes=[pltpu.VMEM((tm, D_MODEL), BF16), pltpu.VMEM((tm, D_MODEL), F32)],
        compiler_params=_cparams("arbitrary", "arbitrary"),
        name="moe_experts",
    )(x, mods_l, mods_l, mods_l, comb, w_gate, w_up, w_down, g.reshape(1, D_MODEL), b.reshape(1, D_MODEL))


def kernel(x_prompt, x_sample, state_gdn, cache_gqa_k, cache_gqa_v, cache_mla_ckv, cache_mla_krope, c, c_ctx, w_ada, b_ada, ln_mix_g, ln_mix_b, ln_ffn_g, ln_ffn_b, gdn_w_in, gdn_conv, gdn_a_log, gdn_dt_bias, gdn_norm_g, gdn_w_out, gqa_w_in, gqa_q_norm, gqa_k_norm, gqa_w_out, mla_w_in, mla_q_norm, mla_kv_norm, mla_w_q_up, mla_w_kv_up, mla_w_out, router_w, router_b, moe_w_gate, moe_w_up, moe_w_down):
    x = jnp.concatenate([x_prompt.reshape(T_CTX, D_MODEL), x_sample.reshape(T_LAT, D_MODEL)], axis=0)
    cond = jnp.concatenate([c_ctx[None], c, jnp.zeros((MOD_ROWS - 1 - DEC_BATCH, D_MODEL), F32)], axis=0)
    mods = _ada_call(cond, w_ada, b_ada)

    st_gdn, st_k, st_v, st_ckv, st_krope = [], [], [], [], []
    for l in range(DEPTH):
        kind, li = l % N_MIXERS, l // N_MIXERS
        mods_l = mods[l]
        if kind == 0:
            o, st = _gdn_layer(x, mods_l, li, state_gdn, gdn_w_in, gdn_conv, gdn_a_log, gdn_dt_bias, gdn_norm_g)
            st_gdn.append(st)
            w_out = gdn_w_out[li]
        elif kind == 1:
            o, nk, nv = _gqa_layer(x, mods_l, li, cache_gqa_k, cache_gqa_v, gqa_w_in, gqa_q_norm, gqa_k_norm)
            st_k.append(nk)
            st_v.append(nv)
            w_out = gqa_w_out[li]
        else:
            o, nckv, nkr = _mla_layer(x, mods_l, li, cache_mla_ckv, cache_mla_krope, mla_w_in, mla_q_norm,
                                      mla_kv_norm, mla_w_q_up, mla_w_kv_up)
            st_ckv.append(nckv)
            st_krope.append(nkr)
            w_out = mla_w_out[li]
        x = _outln_call(o, w_out, x, mods_l, 2, ln_mix_g[l], ln_mix_b[l], name="mix_out_ln")
        comb = _router_call(x, mods_l, router_w, router_b)
        x = _moe_call(x, mods_l, comb, moe_w_gate[l], moe_w_up[l], moe_w_down[l], ln_ffn_g[l], ln_ffn_b[l])

    y_prompt = x[:T_CTX].reshape(BATCH, SEQ, D_MODEL)
    y_sample = x[T_CTX:].reshape(DEC_BATCH, DEC_SEQ, D_MODEL)
    return (y_prompt, y_sample, jnp.stack(st_gdn, axis=1), jnp.stack(st_k, axis=1), jnp.stack(st_v, axis=1),
            jnp.stack(st_ckv, axis=1), jnp.stack(st_krope, axis=1))
```

```python
import functools
import math

import jax
import jax.numpy as jnp
from jax import lax
from jax.experimental import pallas as pl
from jax.experimental.pallas import tpu as pltpu

F32 = jnp.float32
BF16 = jnp.bfloat16

D_MODEL = 1024
BATCH = 32
SEQ = 256
DEPTH = 4
DEC_BATCH = 2
DEC_SEQ = 2048
PAST_LEN = 512
GRID_W = 64
N_MIXERS = 3

GDN_HEADS = 8
GDN_DK = 128
GDN_DV = 128
GDN_CONV = 5
GDN_CHUNK = 64
GDN_QKV = GDN_HEADS * (2 * GDN_DK + GDN_DV)
GDN_QKVZ = GDN_QKV + GDN_HEADS * GDN_DV

GQA_HEADS = 8
GQA_KV_HEADS = 2
GQA_GROUP = GQA_HEADS // GQA_KV_HEADS
GQA_HEAD_DIM = 128
GQA_IN = (GQA_HEADS + 2 * GQA_KV_HEADS) * GQA_HEAD_DIM

MLA_HEADS = 8
MLA_Q_LORA = 512
MLA_KV_LORA = 256
MLA_NOPE = 128
MLA_ROPE = 64
MLA_V = 128
MLA_IN = MLA_Q_LORA + MLA_KV_LORA + MLA_ROPE

N_EXPERTS = 16
N_GROUPS = 4
EXPERTS_PER_GROUP = N_EXPERTS // N_GROUPS
D_EXPERT = 256

ROPE_THETA = 10000.0
ALPHA = (2 * DEPTH) ** 0.25
NORM_EPS = 1e-6

T_CTX = BATCH * SEQ
T_LAT = DEC_BATCH * DEC_SEQ
T_ALL = T_CTX + T_LAT
SEG_ROWS = DEC_SEQ
N_CTX_SEGS = T_CTX // SEG_ROWS
MOD_ROWS = 8

V7X_LANES = 128
V7X_SUBLANES = 8
V7X_VMEM_BYTES = 64 * 1024 * 1024
VMEM_LIMIT = V7X_VMEM_BYTES * 7 // 8

GDN_SUPER = 256
CHUNKS_PER_SUPER = GDN_SUPER // GDN_CHUNK


def _cparams(*sem):
    return pltpu.CompilerParams(dimension_semantics=sem, vmem_limit_bytes=VMEM_LIMIT)


def _mm(a, b):
    return jnp.dot(a.astype(BF16), b.astype(BF16), preferred_element_type=F32)


def _mm_nt(a, b):
    return lax.dot_general(a.astype(BF16), b.astype(BF16), (((1,), (1,)), ((), ())),
                           preferred_element_type=F32)


def _silu(x):
    return x * (1.0 / (1.0 + jnp.exp(-x)))


def _sigmoid(x):
    return 1.0 / (1.0 + jnp.exp(-x))


def _chunk_of(idx):
    return idx >> int(math.log2(GDN_CHUNK))


def _seg_row(i, tiles_per_seg):
    return jnp.maximum(lax.div(i, tiles_per_seg) - (N_CTX_SEGS - 1), 0)


def _layer_norm(y, g, b):
    mu = jnp.mean(y, axis=-1, keepdims=True)
    yc = y - mu
    var = jnp.mean(yc * yc, axis=-1, keepdims=True)
    return yc * lax.rsqrt(var + NORM_EPS) * g + b


def _rms(x, g):
    return x * lax.rsqrt(jnp.mean(x * x, axis=-1, keepdims=True) + NORM_EPS) * g


def _ada_kernel(c_ref, w_ref, b_ref, o_ref):
    o_ref[...] = _mm(_silu(c_ref[...]), w_ref[...]) + b_ref[...]


def _ada_call(cond, w_ada, b_ada):
    tn = 1536
    n = w_ada.shape[-1]
    return pl.pallas_call(
        _ada_kernel,
        grid=(DEPTH, n // tn),
        in_specs=[
            pl.BlockSpec((MOD_ROWS, D_MODEL), lambda l, j: (0, 0)),
            pl.BlockSpec((None, D_MODEL, tn), lambda l, j: (l, 0, j)),
            pl.BlockSpec((None, 1, tn), lambda l, j: (l, 0, j)),
        ],
        out_specs=pl.BlockSpec((None, MOD_ROWS, tn), lambda l, j: (l, 0, j)),
        out_shape=jax.ShapeDtypeStruct((DEPTH, MOD_ROWS, n), F32),
        compiler_params=_cparams("arbitrary", "arbitrary"),
        name="ada_mod",
    )(cond, w_ada, b_ada.reshape(DEPTH, 1, n))


def _modproj_kernel(x_ref, sh_ref, sc_ref, w_ref, *rest, tiles_per_seg, has_extra):
    if has_extra:
        wx_ref, o_ref, ox_ref, h_scr, wb_scr = rest
    else:
        o_ref, h_scr, wb_scr = rest
    i, j = pl.program_id(0), pl.program_id(1)

    @pl.when(i == 0)
    def _():
        wb_scr[j] = w_ref[...].astype(BF16)

    @pl.when(j == 0)
    def _():
        r = _seg_row(i, tiles_per_seg)
        sh = sh_ref[pl.ds(r, 1), :]
        sc = sc_ref[pl.ds(r, 1), :]
        h = (x_ref[...] * (1.0 + sc) + sh).astype(BF16)
        h_scr[...] = h
        if has_extra:
            ox_ref[...] = _mm(h, wx_ref[...])

    o_ref[...] = jnp.dot(h_scr[...], wb_scr[j], preferred_element_type=F32)


def _modproj_call(x, mods_l, shift_idx, scale_idx, w, n_out, w_extra=None, *, tm=1024, tn=512, name):
    t = T_ALL
    has_extra = w_extra is not None
    nj = n_out // tn
    in_specs = [
        pl.BlockSpec((tm, D_MODEL), lambda i, j: (i, 0)),
        pl.BlockSpec((MOD_ROWS, D_MODEL), lambda i, j: (0, shift_idx)),
        pl.BlockSpec((MOD_ROWS, D_MODEL), lambda i, j: (0, scale_idx)),
        pl.BlockSpec((D_MODEL, tn), lambda i, j: (0, jnp.where(i == 0, j, nj - 1))),
    ]
    out_specs = [pl.BlockSpec((tm, tn), lambda i, j: (i, j))]
    out_shape = [jax.ShapeDtypeStruct((t, n_out), F32)]
    args = [x, mods_l, mods_l, w]
    if has_extra:
        nx = w_extra.shape[1]
        in_specs.append(pl.BlockSpec((D_MODEL, nx), lambda i, j: (0, 0)))
        out_specs.append(pl.BlockSpec((tm, nx), lambda i, j: (i, 0)))
        out_shape.append(jax.ShapeDtypeStruct((t, nx), F32))
        args.append(w_extra)
    res = pl.pallas_call(
        functools.partial(_modproj_kernel, tiles_per_seg=SEG_ROWS // tm, has_extra=has_extra),
        grid=(t // tm, nj),
        in_specs=in_specs,
        out_specs=out_specs,
        out_shape=out_shape,
        scratch_shapes=[pltpu.VMEM((tm, D_MODEL), BF16), pltpu.VMEM((nj, D_MODEL, tn), BF16)],
        compiler_params=_cparams("arbitrary", "arbitrary"),
        name=name,
    )(*args)
    return res if has_extra else res[0]


def _rmsproj_kernel(x_ref, g_ref, w_ref, *rest, normalize, emit_xn):
    if emit_xn:
        o_ref, xn_ref, h_scr = rest
    else:
        o_ref, h_scr = rest
    j = pl.program_id(1)

    @pl.when(j == 0)
    def _():
        x = x_ref[...]
        xn = _rms(x, g_ref[...]) if normalize else x
        h_scr[...] = xn.astype(BF16)
        if emit_xn:
            xn_ref[...] = xn

    o_ref[...] = _mm(h_scr[...], w_ref[...])


def _rmsproj_call(x, col_block, k, g, w, *, normalize=True, emit_xn=False, tm=1024, tn=512, name):
    t = x.shape[0]
    n_out = w.shape[1]
    out_specs = [pl.BlockSpec((tm, tn), lambda i, j: (i, j))]
    out_shape = [jax.ShapeDtypeStruct((t, n_out), F32)]
    if emit_xn:
        out_specs.append(pl.BlockSpec((tm, k), lambda i, j: (i, 0)))
        out_shape.append(jax.ShapeDtypeStruct((t, k), F32))
    res = pl.pallas_call(
        functools.partial(_rmsproj_kernel, normalize=normalize, emit_xn=emit_xn),
        grid=(t // tm, n_out // tn),
        in_specs=[
            pl.BlockSpec((tm, k), lambda i, j: (i, col_block)),
            pl.BlockSpec((1, k), lambda i, j: (0, 0)),
            pl.BlockSpec((k, tn), lambda i, j: (0, j)),
        ],
        out_specs=out_specs,
        out_shape=out_shape,
        scratch_shapes=[pltpu.VMEM((tm, k), BF16)],
        compiler_params=_cparams("arbitrary", "arbitrary"),
        name=name,
    )(x, g.reshape(1, k), w)
    return res if emit_xn else res[0]


def _outln_kernel(oc_ref, ol_ref, w_ref, x_ref, gate_ref, g_ref, b_ref, out_ref, wb_scr, *, tiles_per_seg, ctx_tiles):
    i = pl.program_id(0)

    @pl.when(i == 0)
    def _():
        wb_scr[...] = w_ref[...].astype(BF16)

    gate = gate_ref[pl.ds(_seg_row(i, tiles_per_seg), 1), :]
    o = jnp.where(i < ctx_tiles, oc_ref[...], ol_ref[...])
    f = jnp.dot(o.astype(BF16), wb_scr[...], preferred_element_type=F32)
    out_ref[...] = _layer_norm(ALPHA * x_ref[...] + (1.0 + gate) * f, g_ref[...], b_ref[...])


def _outln_call(o_ctx, o_lat, w, x, mods_l, gate_idx, g, b, *, tm=512, name):
    k = o_ctx.shape[1]
    t = T_ALL
    ctx_tiles = T_CTX // tm
    return pl.pallas_call(
        functools.partial(_outln_kernel, tiles_per_seg=SEG_ROWS // tm, ctx_tiles=ctx_tiles),
        grid=(t // tm,),
        in_specs=[
            pl.BlockSpec((tm, k), lambda i: (jnp.minimum(i, ctx_tiles - 1), 0)),
            pl.BlockSpec((tm, k), lambda i: (jnp.maximum(i - ctx_tiles, 0), 0)),
            pl.BlockSpec((k, D_MODEL), lambda i: (0, 0)),
            pl.BlockSpec((tm, D_MODEL), lambda i: (i, 0)),
            pl.BlockSpec((MOD_ROWS, D_MODEL), lambda i: (0, gate_idx)),
            pl.BlockSpec((1, D_MODEL), lambda i: (0, 0)),
            pl.BlockSpec((1, D_MODEL), lambda i: (0, 0)),
        ],
        out_specs=pl.BlockSpec((tm, D_MODEL), lambda i: (i, 0)),
        out_shape=jax.ShapeDtypeStruct((t, D_MODEL), F32),
        scratch_shapes=[pltpu.VMEM((k, D_MODEL), BF16)],
        compiler_params=_cparams("arbitrary"),
        name=name,
    )(o_ctx, o_lat, w, x, mods_l, g.reshape(1, D_MODEL), b.reshape(1, D_MODEL))


def _gdn_gates_kernel(ab_ref, alog_ref, dt_ref, o_ref):
    ab = ab_ref[...]
    tm = ab.shape[0]
    x = ab + dt_ref[...]
    softplus = jnp.maximum(x, 0.0) + jnp.log(1.0 + jnp.exp(-jnp.abs(x)))
    g = -jnp.exp(alog_ref[...]) * softplus
    ri = lax.broadcasted_iota(jnp.int32, (tm, tm), 0)
    ci = lax.broadcasted_iota(jnp.int32, (tm, tm), 1)
    same = _chunk_of(ri) == _chunk_of(ci)
    lower = jnp.where(same & (ri >= ci), 1.0, 0.0).astype(F32)
    upper = jnp.where(same & (ri <= ci), 1.0, 0.0).astype(F32)
    pre = jnp.dot(lower, g, precision=lax.Precision.HIGHEST, preferred_element_type=F32)
    suf = jnp.dot(upper, g, precision=lax.Precision.HIGHEST, preferred_element_type=F32)
    lane = lax.broadcasted_iota(jnp.int32, ab.shape, 1)
    o_ref[...] = jnp.where(lane < GDN_HEADS, pre, jnp.where(lane < 2 * GDN_HEADS, suf, _sigmoid(ab)))


def _gdn_gates_call(ab, a_log, dt_bias):
    t, n = ab.shape
    tm = GDN_SUPER
    pad = n - 2 * GDN_HEADS
    alog_row = jnp.pad(a_log.reshape(1, -1), ((0, 0), (0, pad)))
    dt_row = jnp.pad(dt_bias.reshape(1, -1), ((0, 0), (0, pad)))
    return pl.pallas_call(
        _gdn_gates_kernel,
        grid=(t // tm,),
        in_specs=[
            pl.BlockSpec((tm, n), lambda i: (i, 0)),
            pl.BlockSpec((1, n), lambda i: (0, 0)),
            pl.BlockSpec((1, n), lambda i: (0, 0)),
        ],
        out_specs=pl.BlockSpec((tm, n), lambda i: (i, 0)),
        out_shape=jax.ShapeDtypeStruct((t, n), F32),
        compiler_params=_cparams("arbitrary"),
        name="gdn_gates",
    )(ab, alog_row, dt_row)


def _loop(n, body):
    if n <= 4:
        for i in range(n):
            body(i)
    else:
        def wrapped(i, carry):
            body(i)
            return carry
        lax.fori_loop(0, n, wrapped, 0)


def _gdn_core_kernel(q_ref, k_ref, v_ref, z_ref, cq_ref, ck_ref, cv_ref, gc_ref, gr_ref, ng_ref, *rest,
                     n_tok, has_s0, hps):
    if has_s0:
        s0_ref, o_ref, st_ref = rest[:3]
        scr = rest[3:]
    else:
        o_ref, st_ref = rest[:2]
        scr = rest[2:]
    pad_scr, kn_scr, u_scr, w_scr, qe_scr, qk_scr, oo_scr, s_scr = scr
    n_super = n_tok // GDN_SUPER
    n_chunk = n_tok // GDN_CHUNK
    halo = V7X_SUBLANES
    sb = GDN_SUPER
    ck = GDN_CHUNK
    dk = GDN_DK

    zeros_halo = jnp.zeros((halo, dk), F32)
    for hd in range(hps):
        for a, src in enumerate((q_ref, k_ref, v_ref)):
            pad_scr[3 * hd + a, 0:halo, :] = zeros_halo
            pad_scr[3 * hd + a, halo:halo + n_tok, :] = src[:, hd * dk:(hd + 1) * dk]
            pad_scr[3 * hd + a, halo + n_tok:2 * halo + n_tok, :] = zeros_halo
        for d in range(2):
            s_scr[2 * hd + d] = s0_ref[d, hd] if has_s0 else jnp.zeros((dk, GDN_DV), F32)

    ri = lax.broadcasted_iota(jnp.int32, (sb, sb), 0)
    ci = lax.broadcasted_iota(jnp.int32, (sb, sb), 1)
    same_chunk = _chunk_of(ri) == _chunk_of(ci)
    eye = jnp.where(ri == ci, 1.0, 0.0).astype(F32)
    n_levels = int(math.log2(ck))

    def level_mask(l):
        return ((ri >> (l + 1)) == (ci >> (l + 1))) & ((ri >> l) != (ci >> l))

    def conv_silu(a, w_ref, hd, r0):
        xp = pad_scr[a, pl.ds(r0, sb + 2 * halo), :]
        acc = None
        for j in range(GDN_CONV):
            shift = (GDN_CONV // 2 - j) % (sb + 2 * halo)
            tap = xp if shift == 0 else pltpu.roll(xp, shift, 0)
            term = w_ref[j:j + 1, hd * dk:(hd + 1) * dk] * tap[halo:halo + sb, :]
            acc = term if acc is None else acc + term
        return _silu(acc)

    chains = [(hd, d) for hd in range(hps) for d in range(2)]

    def phase_a(s):
        r0 = pl.multiple_of(s * sb, sb)
        qs, ks, vs, ps, qks = [], [], [], [], []
        for hd in range(hps):
            q = conv_silu(3 * hd, cq_ref, hd, r0)
            k = conv_silu(3 * hd + 1, ck_ref, hd, r0)
            v = conv_silu(3 * hd + 2, cv_ref, hd, r0)
            q = q * lax.rsqrt(jnp.sum(q * q, axis=-1, keepdims=True) + NORM_EPS) * (dk ** -0.5)
            k = k * lax.rsqrt(jnp.sum(k * k, axis=-1, keepdims=True) + NORM_EPS)
            kn_scr[hd, pl.ds(r0, sb), :] = k
            kb = k.astype(BF16)
            qs.append(q)
            ks.append(k)
            vs.append(v)
            ps.append(_mm_nt(kb, kb))
            qks.append(_mm_nt(q, kb))
        a_mats, ts, rhs = [], [], []
        for hd, d in chains:
            hdd = 2 * hd + d
            gcol = gc_ref[hd, pl.ds(r0, sb), d:d + 1]
            bcol = gc_ref[hd, pl.ds(r0, sb), 2 + d:3 + d]
            grow = gr_ref[hd, s, d:d + 1, :]
            if d == 0:
                mask, strict = same_chunk & (ri >= ci), same_chunk & (ri > ci)
            else:
                mask, strict = same_chunk & (ri <= ci), same_chunk & (ri < ci)
            dec = jnp.exp(jnp.where(mask, gcol - grow, -jnp.inf))
            a = jnp.where(strict, ps[hd] * bcol * dec, 0.0)
            a_mats.append(a)
            ts.append(eye - jnp.where(level_mask(0), a, 0.0))
            eg = jnp.exp(gcol)
            rhs.append(jnp.concatenate([vs[hd] * bcol, ks[hd] * (bcol * eg)], axis=1).astype(BF16))
            qe_scr[hdd, pl.ds(r0, sb), :] = qs[hd] * eg
            qkd = qks[hd] * dec
            for c in range(CHUNKS_PER_SUPER):
                qk_scr[hdd, pl.ds(r0 + c * ck, ck), :] = qkd[c * ck:(c + 1) * ck, c * ck:(c + 1) * ck]
        for l in range(1, n_levels):
            ys = [_mm(jnp.where(level_mask(l), a_mats[i], 0.0), ts[i]) for i in range(len(chains))]
            ts = [ts[i] - _mm(ts[i], ys[i]) for i in range(len(chains))]
        for i, (hd, d) in enumerate(chains):
            uw = _mm(ts[i], rhs[i])
            u_scr[2 * hd + d, pl.ds(r0, sb), :] = uw[:, :GDN_DV]
            w_scr[2 * hd + d, pl.ds(r0, sb), :] = uw[:, GDN_DV:]

    _loop(n_super, phase_a)

    def phase_b(i):
        rows, ws_qs, v_new, kws, gls = [], [], [], [], []
        for hd, d in chains:
            hdd = 2 * hd + d
            c = i if d == 0 else n_chunk - 1 - i
            r = pl.multiple_of(c * ck, ck)
            r_last = r + (ck - 1) if d == 0 else r
            rows.append(r)
            gcol = gc_ref[hd, pl.ds(r, ck), d:d + 1]
            gl = gc_ref[hd, pl.ds(r_last, 1), d:d + 1]
            gls.append(gl)
            kws.append((kn_scr[hd, pl.ds(r, ck), :] * jnp.exp(gl - gcol)).T)
            wq = jnp.concatenate([w_scr[hdd, pl.ds(r, ck), :], qe_scr[hdd, pl.ds(r, ck), :]], axis=0)
            ws_qs.append(_mm(wq, s_scr[hdd]))
        for j, (hd, d) in enumerate(chains):
            hdd = 2 * hd + d
            vn = u_scr[hdd, pl.ds(rows[j], ck), :] - ws_qs[j][:ck]
            v_new.append(vn)
            oo_scr[hdd, pl.ds(rows[j], ck), :] = ws_qs[j][ck:] + _mm(qk_scr[hdd, pl.ds(rows[j], ck), :], vn)
        for j, (hd, d) in enumerate(chains):
            hdd = 2 * hd + d
            s_scr[hdd] = s_scr[hdd] * jnp.exp(gls[j]) + _mm(kws[j], v_new[j])

    _loop(n_chunk, phase_b)

    def finish(s):
        r0 = pl.multiple_of(s * sb, sb)
        for hd in range(hps):
            o = oo_scr[2 * hd, pl.ds(r0, sb), :] + oo_scr[2 * hd + 1, pl.ds(r0, sb), :]
            z = z_ref[pl.ds(r0, sb), hd * GDN_DV:(hd + 1) * GDN_DV]
            o_ref[pl.ds(r0, sb), hd * GDN_DV:(hd + 1) * GDN_DV] = _rms(o, ng_ref[...]) * _silu(z)

    _loop(n_super, finish)
    for hd in range(hps):
        for d in range(2):
            st_ref[d, hd] = s_scr[2 * hd + d]


def _gdn_core_call(qkvz, conv_w, gc, gr, norm_g, s0, *, batch, n_tok, row_block_off, hps, name):
    has_s0 = s0 is not None
    h = GDN_HEADS
    dk = GDN_DK
    n_super = n_tok // GDN_SUPER
    hb = h // hps

    mode = dict(pipeline_mode=pl.Buffered(1)) if n_tok > GDN_SUPER else {}

    def col(sec):
        return pl.BlockSpec((n_tok, dk * hps), lambda b, hh: (b + row_block_off, sec * hb + hh), **mode)

    def cw(sec):
        return pl.BlockSpec((GDN_CONV, dk * hps), lambda b, hh: (0, sec * hb + hh))

    in_specs = [col(0), col(1), col(2), col(3), cw(0), cw(1), cw(2),
                pl.BlockSpec((None, hps, n_tok, 4), lambda b, hh: (b, hh, 0, 0)),
                pl.BlockSpec((None, hps, n_super, 4, GDN_SUPER), lambda b, hh: (b, hh, 0, 0, 0)),
                pl.BlockSpec((1, GDN_DV), lambda b, hh: (0, 0))]
    args = [qkvz, qkvz, qkvz, qkvz, conv_w, conv_w, conv_w, gc, gr, norm_g.reshape(1, GDN_DV)]
    st_spec = pl.BlockSpec((None, 2, hps, dk, GDN_DV), lambda b, hh: (b, 0, hh, 0, 0))
    if has_s0:
        in_specs.append(st_spec)
        args.append(s0)
    return pl.pallas_call(
        functools.partial(_gdn_core_kernel, n_tok=n_tok, has_s0=has_s0, hps=hps),
        grid=(batch, hb),
        in_specs=in_specs,
        out_specs=[pl.BlockSpec((n_tok, GDN_DV * hps), lambda b, hh: (b, hh)), st_spec],
        out_shape=[jax.ShapeDtypeStruct((batch * n_tok, h * GDN_DV), F32),
                   jax.ShapeDtypeStruct((batch, 2, h, dk, GDN_DV), F32)],
        scratch_shapes=[
            pltpu.VMEM((3 * hps, n_tok + 2 * V7X_SUBLANES, dk), F32),
            pltpu.VMEM((hps, n_tok, dk), F32),
            pltpu.VMEM((2 * hps, n_tok, GDN_DV), F32),
            pltpu.VMEM((2 * hps, n_tok, dk), F32),
            pltpu.VMEM((2 * hps, n_tok, dk), F32),
            pltpu.VMEM((2 * hps, n_tok, GDN_CHUNK), F32),
            pltpu.VMEM((2 * hps, n_tok, GDN_DV), F32),
            pltpu.VMEM((2 * hps, dk, GDN_DV), F32),
        ],
        compiler_params=_cparams("arbitrary", "arbitrary"),
        name=name,
    )(*args)


def _gdn_layer(x, mods_l, li, state_gdn, gdn_w_in, gdn_conv, gdn_a_log, gdn_dt_bias, gdn_norm_g):
    w_in = gdn_w_in[li]
    w_ab = jnp.pad(w_in[:, GDN_QKVZ:], ((0, 0), (0, V7X_LANES - 4 * GDN_HEADS)))
    qkvz, ab = _modproj_call(x, mods_l, 0, 1, w_in, GDN_QKVZ, w_ab, name="gdn_in")
    gates = _gdn_gates_call(ab, gdn_a_log[li], gdn_dt_bias[li])[:, :4 * GDN_HEADS]

    def head_layouts(g, batch, n_tok):
        g = g.reshape(batch, n_tok, 4, GDN_HEADS)
        gc = jnp.transpose(g, (0, 3, 1, 2))
        gr = jnp.transpose(g, (0, 3, 2, 1)).reshape(batch, GDN_HEADS, 4, n_tok // GDN_SUPER, GDN_SUPER)
        return gc, jnp.transpose(gr, (0, 1, 3, 2, 4))

    gc_p, gr_p = head_layouts(gates[:T_CTX], BATCH, SEQ)
    gc_s, gr_s = head_layouts(gates[T_CTX:], DEC_BATCH, DEC_SEQ)
    o_p, st_p = _gdn_core_call(qkvz, gdn_conv[li], gc_p, gr_p, gdn_norm_g[li], None,
                               batch=BATCH, n_tok=SEQ, row_block_off=0, hps=4, name="gdn_core_ctx")
    o_s, _ = _gdn_core_call(qkvz, gdn_conv[li], gc_s, gr_s, gdn_norm_g[li], state_gdn[:, li],
                            batch=DEC_BATCH, n_tok=DEC_SEQ, row_block_off=T_CTX // DEC_SEQ, hps=2,
                            name="gdn_core_lat")
    return (o_p, o_s), st_p


def _rope_tables(n_tokens, rot_dim, reps):
    n_rows = n_tokens // GRID_W
    row = jnp.repeat(jnp.arange(n_rows, dtype=F32), GRID_W)
    colp = jnp.tile(jnp.arange(GRID_W, dtype=F32), n_rows)
    n_freq = rot_dim // 4
    inv_freq = ROPE_THETA ** (-jnp.arange(n_freq, dtype=F32) / n_freq)
    ra, ca = row[:, None] * inv_freq, colp[:, None] * inv_freq
    cos = jnp.concatenate([jnp.cos(ra), jnp.cos(ra), jnp.cos(ca), jnp.cos(ca)], axis=-1)
    sin = jnp.concatenate([-jnp.sin(ra), jnp.sin(ra), -jnp.sin(ca), jnp.sin(ca)], axis=-1)
    return jnp.tile(cos, (1, reps)), jnp.tile(sin, (1, reps))


def _rope(x, cos, sin, quarter):
    n = x.shape[-1]
    lane = lax.broadcasted_iota(jnp.int32, x.shape, x.ndim - 1)
    first = (lane & (2 * quarter - 1)) < quarter
    partner = jnp.where(first, pltpu.roll(x, n - quarter, x.ndim - 1), pltpu.roll(x, quarter, x.ndim - 1))
    return x * cos + partner * sin


def _gqa_prep_kernel(kv_ref, g_ref, *rest, rope):
    if rope:
        cos_ref, sin_ref, k_ref, v_ref = rest
    else:
        k_ref, v_ref = rest
    kv = kv_ref[...]
    nk = GQA_KV_HEADS * GQA_HEAD_DIM
    for hh in range(GQA_KV_HEADS):
        k = _rms(kv[:, hh * GQA_HEAD_DIM:(hh + 1) * GQA_HEAD_DIM], g_ref[...])
        if rope:
            k = _rope(k, cos_ref[...], sin_ref[...], GQA_HEAD_DIM // 4)
        k_ref[:, hh * GQA_HEAD_DIM:(hh + 1) * GQA_HEAD_DIM] = k
    v_ref[...] = kv[:, nk:]


def _gqa_prep_call(proj, k_norm, row_off, n_rows, tables, *, tm=512, name):
    rope = tables is not None
    nkv = 2 * GQA_KV_HEADS * GQA_HEAD_DIM
    kv_col = GQA_HEADS * GQA_HEAD_DIM // nkv
    off = row_off // tm
    in_specs = [pl.BlockSpec((tm, nkv), lambda i: (i + off, kv_col)),
                pl.BlockSpec((1, GQA_HEAD_DIM), lambda i: (0, 0))]
    args = [proj, k_norm.reshape(1, GQA_HEAD_DIM)]
    if rope:
        per_seq = DEC_SEQ // tm
        in_specs += [pl.BlockSpec((tm, GQA_HEAD_DIM), lambda i: (i % per_seq, 0))] * 2
        args += list(tables)
    half = nkv // 2
    return pl.pallas_call(
        functools.partial(_gqa_prep_kernel, rope=rope),
        grid=(n_rows // tm,),
        in_specs=in_specs,
        out_specs=[pl.BlockSpec((tm, half), lambda i: (i, 0))] * 2,
        out_shape=[jax.ShapeDtypeStruct((n_rows, half), F32)] * 2,
        compiler_params=_cparams("arbitrary"),
        name=name,
    )(*args)


def _softmax_av(scores, values):
    m = None
    for s in scores:
        sm = jnp.max(s, axis=-1, keepdims=True)
        m = sm if m is None else jnp.maximum(m, sm)
    den = None
    acc = None
    for s, v in zip(scores, values):
        p = jnp.exp(s - m)
        ps = jnp.sum(p, axis=-1, keepdims=True)
        pv = _mm(p, v)
        den = ps if den is None else den + ps
        acc = pv if acc is None else acc + pv
    return acc / den


def _gqa_attn_kernel(q_ref, qg_ref, *rest, n_seg, rope, kv_per_step):
    if rope:
        cos_ref, sin_ref = rest[:2]
        rest = rest[2:]
    seg_refs = rest[:2 * n_seg]
    o_ref = rest[2 * n_seg]
    dh = GQA_HEAD_DIM
    scale = dh ** -0.5
    tq = q_ref.shape[0]
    for kh in range(kv_per_step):
        qs = []
        for g in range(GQA_GROUP):
            c0 = (kh * GQA_GROUP + g) * dh
            qh = _rms(q_ref[:, c0:c0 + dh], qg_ref[...])
            if rope:
                qh = _rope(qh, cos_ref[...], sin_ref[...], dh // 4)
            qs.append(qh.astype(BF16))
        qst = jnp.concatenate(qs, axis=0)
        scores, values = [], []
        for sgi in range(n_seg):
            kk = seg_refs[2 * sgi][:, kh * dh:(kh + 1) * dh]
            scores.append(_mm_nt(qst, kk) * scale)
            values.append(seg_refs[2 * sgi + 1][:, kh * dh:(kh + 1) * dh])
        o = _softmax_av(scores, values)
        for g in range(GQA_GROUP):
            c0 = (kh * GQA_GROUP + g) * dh
            o_ref[:, c0:c0 + dh] = o[g * tq:(g + 1) * tq]


def _gqa_attn_call(proj, q_norm, segs, tables, *, batch, n_q, row_off, tq, kv_per_step, name):
    rope = tables is not None
    dh = GQA_HEAD_DIM
    qw = GQA_GROUP * dh * kv_per_step
    kw = dh * kv_per_step
    nqb = n_q // tq
    off = row_off // tq
    in_specs = [pl.BlockSpec((tq, qw), lambda b, h, i: (b * nqb + i + off, h)),
                pl.BlockSpec((1, dh), lambda b, h, i: (0, 0))]
    args = [proj, q_norm.reshape(1, dh)]
    if rope:
        in_specs += [pl.BlockSpec((tq, dh), lambda b, h, i: (i, 0))] * 2
        args += list(tables)
    for k2d, v2d, rows in segs:
        in_specs += [pl.BlockSpec((rows, kw), lambda b, h, i: (b, h))] * 2
        args += [k2d, v2d]
    return pl.pallas_call(
        functools.partial(_gqa_attn_kernel, n_seg=len(segs), rope=rope, kv_per_step=kv_per_step),
        grid=(batch, GQA_KV_HEADS // kv_per_step, nqb),
        in_specs=in_specs,
        out_specs=pl.BlockSpec((tq, qw), lambda b, h, i: (b * nqb + i, h)),
        out_shape=jax.ShapeDtypeStruct((batch * n_q, GQA_HEADS * dh), F32),
        compiler_params=_cparams("arbitrary", "arbitrary", "arbitrary"),
        name=name,
    )(*args)


def _gqa_layer(x, mods_l, li, cache_k, cache_v, gqa_w_in, gqa_q_norm, gqa_k_norm):
    proj = _modproj_call(x, mods_l, 0, 1, gqa_w_in[li], GQA_IN, name="gqa_in")
    tables = _rope_tables(DEC_SEQ, GQA_HEAD_DIM, 1)
    nkv = GQA_KV_HEADS * GQA_HEAD_DIM
    k_p, v_p = _gqa_prep_call(proj, gqa_k_norm[li], 0, T_CTX, None, name="gqa_prep_ctx")
    k_s, v_s = _gqa_prep_call(proj, gqa_k_norm[li], T_CTX, T_LAT, tables, name="gqa_prep_lat")
    o_p = _gqa_attn_call(proj, gqa_q_norm[li], [(k_p, v_p, SEQ)], None, batch=BATCH, n_q=SEQ, row_off=0,
                         tq=SEQ, kv_per_step=GQA_KV_HEADS, name="gqa_attn_ctx")
    ck = cache_k[:, li].reshape(DEC_BATCH * PAST_LEN, nkv)
    cv = cache_v[:, li].reshape(DEC_BATCH * PAST_LEN, nkv)
    o_s = _gqa_attn_call(proj, gqa_q_norm[li], [(ck, cv, PAST_LEN), (k_s, v_s, DEC_SEQ)], tables,
                         batch=DEC_BATCH, n_q=DEC_SEQ, row_off=T_CTX, tq=128, kv_per_step=1, name="gqa_attn_lat")
    new_k = k_p.reshape(BATCH, SEQ, GQA_KV_HEADS, GQA_HEAD_DIM)
    new_v = v_p.reshape(BATCH, SEQ, GQA_KV_HEADS, GQA_HEAD_DIM)
    return (o_p, o_s), new_k, new_v


def _krope_kernel(x_ref, *rest, rope):
    if rope:
        cos_ref, sin_ref, o_ref = rest
    else:
        (o_ref,) = rest
    x = x_ref[...]
    x2 = jnp.concatenate([x, x], axis=1)
    if rope:
        x2 = _rope(x2, cos_ref[...], sin_ref[...], MLA_ROPE // 4)
    o_ref[...] = x2


def _krope_call(krope, tables, *, tm=512, name):
    rope = tables is not None
    n = krope.shape[0]
    in_specs = [pl.BlockSpec((tm, MLA_ROPE), lambda i: (i, 0))]
    args = [krope]
    if rope:
        per_seq = DEC_SEQ // tm
        in_specs += [pl.BlockSpec((tm, 2 * MLA_ROPE), lambda i: (i % per_seq, 0))] * 2
        args += list(tables)
    return pl.pallas_call(
        functools.partial(_krope_kernel, rope=rope),
        grid=(n // tm,),
        in_specs=in_specs,
        out_specs=pl.BlockSpec((tm, 2 * MLA_ROPE), lambda i: (i, 0)),
        out_shape=jax.ShapeDtypeStruct((n, 2 * MLA_ROPE), F32),
        compiler_params=_cparams("arbitrary"),
        name=name,
    )(*args)


def _mla_attn_kernel(qn_ref, qr_ref, *rest, n_seg, rope, heads_per_step):
    if rope:
        cos_ref, sin_ref = rest[:2]
        rest = rest[2:]
    seg_refs = rest[:2 * n_seg]
    o_ref = rest[2 * n_seg]
    scale = (MLA_NOPE + MLA_ROPE) ** -0.5
    h0 = pl.program_id(1) * heads_per_step
    pair_w = 2 * MLA_ROPE
    for j in range(heads_per_step):
        qn = qn_ref[:, j * MLA_NOPE:(j + 1) * MLA_NOPE]
        pair = qr_ref[:, (j // 2) * pair_w:(j // 2 + 1) * pair_w]
        if rope:
            pair = _rope(pair, cos_ref[...], sin_ref[...], MLA_ROPE // 4)
        lane = lax.broadcasted_iota(jnp.int32, pair.shape, 1)
        mine = (lane >> int(math.log2(MLA_ROPE))) == ((h0 + j) & 1)
        q = jnp.concatenate([qn, jnp.where(mine, pair, 0.0)], axis=1).astype(BF16)
        scores, values = [], []
        for sgi in range(n_seg):
            kv = seg_refs[2 * sgi][:, j * (MLA_NOPE + MLA_V):(j + 1) * (MLA_NOPE + MLA_V)]
            kmat = jnp.concatenate([kv[:, :MLA_NOPE], seg_refs[2 * sgi + 1][...]], axis=1)
            scores.append(_mm_nt(q, kmat) * scale)
            values.append(kv[:, MLA_NOPE:])
        o_ref[:, j * MLA_V:(j + 1) * MLA_V] = _softmax_av(scores, values)


def _mla_attn_call(q, segs, tables, *, batch, n_q, row_off, tq, heads_per_step, name):
    rope = tables is not None
    hps = heads_per_step
    nqb = n_q // tq
    off = row_off // tq
    rope_w = max(2 * MLA_ROPE, MLA_ROPE * hps)
    rope_base = MLA_HEADS * MLA_NOPE // rope_w

    def rope_idx(b, h, i):
        return (b * nqb + i + off, rope_base + (h * hps * MLA_ROPE) // rope_w)

    in_specs = [pl.BlockSpec((tq, MLA_NOPE * hps), lambda b, h, i: (b * nqb + i + off, h)),
                pl.BlockSpec((tq, rope_w), rope_idx)]
    args = [q, q]
    if rope:
        in_specs += [pl.BlockSpec((tq, 2 * MLA_ROPE), lambda b, h, i: (i, 0))] * 2
        args += list(tables)
    for kv2d, kr2, rows, kv_row_off in segs:
        kv_off = kv_row_off // rows
        in_specs += [pl.BlockSpec((rows, (MLA_NOPE + MLA_V) * hps), lambda b, h, i, kv_off=kv_off: (b + kv_off, h)),
                     pl.BlockSpec((rows, 2 * MLA_ROPE), lambda b, h, i: (b, 0))]
        args += [kv2d, kr2]
    return pl.pallas_call(
        functools.partial(_mla_attn_kernel, n_seg=len(segs), rope=rope, heads_per_step=hps),
        grid=(batch, MLA_HEADS // hps, nqb),
        in_specs=in_specs,
        out_specs=pl.BlockSpec((tq, MLA_V * hps), lambda b, h, i: (b * nqb + i, h)),
        out_shape=jax.ShapeDtypeStruct((batch * n_q, MLA_HEADS * MLA_V), F32),
        compiler_params=_cparams("arbitrary", "arbitrary", "arbitrary"),
        name=name,
    )(*args)


def _mla_layer(x, mods_l, li, cache_ckv, cache_krope, mla_w_in, mla_q_norm, mla_kv_norm, mla_w_q_up, mla_w_kv_up):
    w_in = jnp.pad(mla_w_in[li], ((0, 0), (0, 7 * V7X_LANES - MLA_IN)))
    proj = _modproj_call(x, mods_l, 0, 1, w_in, 7 * V7X_LANES, tn=7 * V7X_LANES, name="mla_in")
    wq = mla_w_q_up[li].reshape(MLA_Q_LORA, MLA_HEADS, MLA_NOPE + MLA_ROPE)
    wq = jnp.concatenate([wq[:, :, :MLA_NOPE].reshape(MLA_Q_LORA, -1), wq[:, :, MLA_NOPE:].reshape(MLA_Q_LORA, -1)], axis=1)
    q = _rmsproj_call(proj, 0, MLA_Q_LORA, mla_q_norm[li], wq, name="mla_q_up")
    kv, ckv = _rmsproj_call(proj, MLA_Q_LORA // MLA_KV_LORA, MLA_KV_LORA, mla_kv_norm[li], mla_w_kv_up[li],
                            emit_xn=True, name="mla_kv_up")
    krope = proj[:, MLA_Q_LORA + MLA_KV_LORA:MLA_IN]
    tables = _rope_tables(DEC_SEQ, MLA_ROPE, 2)
    kr2_p = _krope_call(krope[:T_CTX], None, name="mla_krope_ctx")
    kr2_s = _krope_call(krope[T_CTX:], tables, name="mla_krope_lat")
    ctx_ckv = cache_ckv[:, li].reshape(DEC_BATCH * PAST_LEN, MLA_KV_LORA)
    kv_c = _rmsproj_call(ctx_ckv, 0, MLA_KV_LORA, jnp.ones((MLA_KV_LORA,), F32), mla_w_kv_up[li], normalize=False,
                         name="mla_kv_up_cache")
    kr2_c = _krope_call(cache_krope[:, li].reshape(DEC_BATCH * PAST_LEN, MLA_ROPE), None, name="mla_krope_cache")
    o_p = _mla_attn_call(q, [(kv, kr2_p, SEQ, 0)], None, batch=BATCH, n_q=SEQ, row_off=0, tq=SEQ,
                         heads_per_step=MLA_HEADS, name="mla_attn_ctx")
    o_s = _mla_attn_call(q, [(kv_c, kr2_c, PAST_LEN, 0), (kv, kr2_s, DEC_SEQ, T_CTX)], tables, batch=DEC_BATCH,
                         n_q=DEC_SEQ, row_off=T_CTX, tq=512, heads_per_step=1, name="mla_attn_lat")
    new_ckv = ckv[:T_CTX].reshape(BATCH, SEQ, MLA_KV_LORA)
    new_krope = krope[:T_CTX].reshape(BATCH, SEQ, MLA_ROPE)
    return (o_p, o_s), new_ckv, new_krope


def _router_kernel(x_ref, sh_ref, sc_ref, rw_ref, rb_ref, o_ref, *, tiles_per_seg):
    i = pl.program_id(0)
    r = _seg_row(i, tiles_per_seg)
    h = x_ref[...] * (1.0 + sc_ref[pl.ds(r, 1), :]) + sh_ref[pl.ds(r, 1), :]
    logits = lax.dot_general(rw_ref[...], h, (((1,), (1,)), ((), ())), precision=lax.Precision.HIGHEST,
                             preferred_element_type=F32)[:N_EXPERTS]
    scores = _sigmoid(logits)
    sel = scores + rb_ref[0:N_EXPERTS, :]
    tm = sel.shape[1]
    neg = -jnp.inf
    gs = []
    for g in range(N_GROUPS):
        a, b, c, d = (sel[EXPERTS_PER_GROUP * g + j:EXPERTS_PER_GROUP * g + j + 1, :] for j in range(4))
        gs.append(jnp.maximum(jnp.maximum(jnp.maximum(a + b, a + c), jnp.maximum(a + d, b + c)),
                              jnp.maximum(b + d, c + d)))
    gmax = jnp.maximum(jnp.maximum(gs[0], gs[1]), jnp.maximum(gs[2], gs[3]))
    best = jnp.where(gs[0] == gmax, 0, jnp.where(gs[1] == gmax, 1, jnp.where(gs[2] == gmax, 2, 3)))
    eid = lax.broadcasted_iota(jnp.int32, (N_EXPERTS, tm), 0)
    masked = jnp.where((eid >> int(math.log2(EXPERTS_PER_GROUP))) == best, sel, neg)
    m1 = jnp.max(masked, axis=0, keepdims=True)
    i1 = jnp.min(jnp.where(masked == m1, eid, N_EXPERTS), axis=0, keepdims=True)
    masked2 = jnp.where(eid == i1, neg, masked)
    m2 = jnp.max(masked2, axis=0, keepdims=True)
    i2 = jnp.min(jnp.where(masked2 == m2, eid, N_EXPERTS), axis=0, keepdims=True)
    w1 = jnp.sum(jnp.where(eid == i1, scores, 0.0), axis=0, keepdims=True)
    w2 = jnp.sum(jnp.where(eid == i2, scores, 0.0), axis=0, keepdims=True)
    tot = w1 + w2
    comb = jnp.where(eid == i1, w1 / tot, 0.0) + jnp.where(eid == i2, w2 / tot, 0.0)
    base = best * EXPERTS_PER_GROUP
    rid = lax.broadcasted_iota(jnp.int32, (V7X_SUBLANES, tm), 0)
    packed = jnp.where(rid == EXPERTS_PER_GROUP, best.astype(F32), 0.0)
    for j in range(EXPERTS_PER_GROUP):
        lw = jnp.sum(jnp.where(eid == base + j, comb, 0.0), axis=0, keepdims=True)
        packed = jnp.where(rid == j, lw, packed)
    packed = jnp.concatenate([packed, jnp.zeros((V7X_LANES - V7X_SUBLANES, tm), F32)], axis=0)
    o_ref[...] = packed.T


def _router_call(x, mods_l, router_w, router_b, *, tm=512):
    t = T_ALL
    rw = jnp.pad(router_w.T, ((0, V7X_LANES - N_EXPERTS), (0, 0)))
    rb = jnp.pad(router_b.reshape(N_EXPERTS, 1), ((0, V7X_LANES - N_EXPERTS), (0, 0)))
    return pl.pallas_call(
        functools.partial(_router_kernel, tiles_per_seg=SEG_ROWS // tm),
        grid=(t // tm,),
        in_specs=[
            pl.BlockSpec((tm, D_MODEL), lambda i: (i, 0)),
            pl.BlockSpec((MOD_ROWS, D_MODEL), lambda i: (0, 3)),
            pl.BlockSpec((MOD_ROWS, D_MODEL), lambda i: (0, 4)),
            pl.BlockSpec((V7X_LANES, D_MODEL), lambda i: (0, 0)),
            pl.BlockSpec((V7X_LANES, 1), lambda i: (0, 0)),
        ],
        out_specs=pl.BlockSpec((tm, V7X_LANES), lambda i: (i, 0)),
        out_shape=jax.ShapeDtypeStruct((t, V7X_LANES), F32),
        compiler_params=_cparams("arbitrary"),
        name="moe_router",
    )(x, mods_l, mods_l, rw, rb)


MOE_TILE = 256
N_MOD_CLASSES = 1 + DEC_BATCH
N_BUCKETS = N_GROUPS * N_MOD_CLASSES
MOE_TILES = T_ALL // MOE_TILE + N_BUCKETS
MOE_SLOTS = MOE_TILES * MOE_TILE
MOE_TRASH_ROWS = 2 * MOE_TILE


def _moe_plan(routed):
    grp = routed[:, EXPERTS_PER_GROUP].astype(jnp.int32)
    tok = jnp.arange(T_ALL, dtype=jnp.int32)
    mod_row = jnp.maximum(tok // SEG_ROWS - (N_CTX_SEGS - 1), 0)
    bucket = grp * N_MOD_CLASSES + mod_row
    order = jnp.sort(bucket * T_ALL + tok) % T_ALL
    counts = jnp.sum((bucket[:, None] == jnp.arange(N_BUCKETS)[None, :]).astype(jnp.int32), axis=0)
    tiles = (counts + MOE_TILE - 1) // MOE_TILE
    tile_end = jnp.cumsum(tiles)
    slot_start = (tile_end - tiles) * MOE_TILE
    tok_start = jnp.cumsum(counts) - counts
    tile_bucket = jnp.minimum(jnp.sum((jnp.arange(MOE_TILES)[:, None] >= tile_end[None, :]).astype(jnp.int32), axis=1),
                              N_BUCKETS - 1)
    slot = jnp.arange(MOE_SLOTS, dtype=jnp.int32)
    slot_bucket = jnp.repeat(tile_bucket, MOE_TILE)
    padded = jnp.concatenate([jnp.zeros((MOE_SLOTS,), jnp.int32), order, jnp.zeros((MOE_SLOTS - T_ALL,), jnp.int32)])
    slot_tok = jnp.full((MOE_SLOTS,), -1, jnp.int32)
    for b in range(N_BUCKETS):
        shift = slot_start[b] - tok_start[b]
        cand = lax.dynamic_slice(padded, (MOE_SLOTS - shift,), (MOE_SLOTS,))
        mine = (slot_bucket == b) & (slot - slot_start[b] < counts[b])
        slot_tok = jnp.where(mine, cand, slot_tok)
    return slot_tok, tile_bucket // N_MOD_CLASSES, tile_bucket % N_MOD_CLASSES


def _moe_kernel(tok_ref, tgrp_ref, tmod_ref, x_hbm, routed_ref, sh_ref, sc_ref, gate_ref, wg_ref, wu_ref, wd_ref,
                g_ref, b_ref, out_hbm, xbuf, obuf, cw_scr, wgb, wub, wdb, gsem, ssem):
    t = pl.program_id(0)
    nt = pl.num_programs(0)
    tm = MOE_TILE
    cur = lax.rem(t, 2)
    nxt = 1 - cur

    def start_gather(tile, buf):
        for r in range(tm):
            src = jnp.maximum(tok_ref[tile * tm + r], 0)
            pltpu.make_async_copy(x_hbm.at[pl.ds(src, 1)], xbuf.at[buf, pl.ds(r, 1)], gsem.at[buf]).start(priority=r % 2)

    def wait_gather(buf):
        pltpu.make_async_copy(x_hbm.at[pl.ds(0, tm)], xbuf.at[buf], gsem.at[buf]).wait()

    def wait_scatter(buf):
        pltpu.make_async_copy(obuf.at[buf], out_hbm.at[pl.ds(0, tm)], ssem.at[buf]).wait()

    @pl.when(t == 0)
    def _():
        start_gather(0, 0)

    wait_gather(cur)
    start_gather(jnp.minimum(t + 1, nt - 1), nxt)

    @pl.when(t >= 2)
    def _():
        wait_scatter(cur)

    @pl.when((t == 0) | (tgrp_ref[t] != tgrp_ref[jnp.maximum(t - 1, 0)]))
    def _():
        wgb[...] = wg_ref[...].astype(BF16)
        wub[...] = wu_ref[...].astype(BF16)
        wdb[...] = wd_ref[...].astype(BF16)

    for r in range(tm):
        src = jnp.maximum(tok_ref[t * tm + r], 0)
        cw_scr[r:r + 1, :] = routed_ref[pl.ds(src, 1), :]

    x = xbuf[cur]
    row = tmod_ref[t]
    h = (x * (1.0 + sc_ref[pl.ds(row, 1), :]) + sh_ref[pl.ds(row, 1), :]).astype(BF16)
    cw = cw_scr[...]
    acc = None
    for j in range(EXPERTS_PER_GROUP):
        hid = _silu(jnp.dot(h, wgb[j], preferred_element_type=F32)) * jnp.dot(h, wub[j], preferred_element_type=F32)
        f = jnp.dot((hid * cw[:, j:j + 1]).astype(BF16), wdb[j], preferred_element_type=F32)
        acc = f if acc is None else acc + f
    gate = gate_ref[pl.ds(row, 1), :]
    obuf[cur] = _layer_norm(ALPHA * x + (1.0 + gate) * acc, g_ref[...], b_ref[...])

    trash = T_ALL + cur * tm
    for r in range(tm):
        s = tok_ref[t * tm + r]
        dst = jnp.where(s >= 0, s, trash + r)
        pltpu.make_async_copy(obuf.at[cur, pl.ds(r, 1)], out_hbm.at[pl.ds(dst, 1)], ssem.at[cur]).start(priority=r % 2)

    @pl.when(t == nt - 1)
    def _():
        wait_gather(nxt)
        wait_scatter(nxt)
        wait_scatter(cur)


def _moe_call(x, mods_l, routed, w_gate, w_up, w_down, g, b):
    slot_tok, tile_grp, tile_mod = _moe_plan(routed)
    tm = MOE_TILE
    gw = EXPERTS_PER_GROUP

    def const(shape):
        return pl.BlockSpec(shape, lambda t, *_: tuple(0 for _ in shape))

    def mod(idx):
        return pl.BlockSpec((MOD_ROWS, D_MODEL), lambda t, *_: (0, idx))

    grid_spec = pltpu.PrefetchScalarGridSpec(
        num_scalar_prefetch=3,
        grid=(MOE_TILES,),
        in_specs=[
            pl.BlockSpec(memory_space=pl.ANY),
            pl.BlockSpec((T_ALL, V7X_LANES), lambda t, *_: (0, 0), pipeline_mode=pl.Buffered(1)),
            mod(3), mod(4), mod(5),
            pl.BlockSpec((gw, D_MODEL, D_EXPERT), lambda t, tok, grp, row: (grp[t], 0, 0)),
            pl.BlockSpec((gw, D_MODEL, D_EXPERT), lambda t, tok, grp, row: (grp[t], 0, 0)),
            pl.BlockSpec((gw, D_EXPERT, D_MODEL), lambda t, tok, grp, row: (grp[t], 0, 0)),
            const((1, D_MODEL)), const((1, D_MODEL)),
        ],
        out_specs=pl.BlockSpec(memory_space=pl.ANY),
        scratch_shapes=[
            pltpu.VMEM((2, tm, D_MODEL), F32),
            pltpu.VMEM((2, tm, D_MODEL), F32),
            pltpu.VMEM((tm, V7X_LANES), F32),
            pltpu.VMEM((gw, D_MODEL, D_EXPERT), BF16),
            pltpu.VMEM((gw, D_MODEL, D_EXPERT), BF16),
            pltpu.VMEM((gw, D_EXPERT, D_MODEL), BF16),
            pltpu.SemaphoreType.DMA((2,)),
            pltpu.SemaphoreType.DMA((2,)),
        ],
    )
    return pl.pallas_call(
        _moe_kernel,
        grid_spec=grid_spec,
        out_shape=jax.ShapeDtypeStruct((T_ALL + MOE_TRASH_ROWS, D_MODEL), F32),
        compiler_params=_cparams("arbitrary"),
        name="moe_experts",
    )(slot_tok, tile_grp, tile_mod, x, routed, mods_l, mods_l, mods_l, w_gate, w_up, w_down,
      g.reshape(1, D_MODEL), b.reshape(1, D_MODEL))


def kernel(x_prompt, x_sample, state_gdn, cache_gqa_k, cache_gqa_v, cache_mla_ckv, cache_mla_krope, c, c_ctx, w_ada, b_ada, ln_mix_g, ln_mix_b, ln_ffn_g, ln_ffn_b, gdn_w_in, gdn_conv, gdn_a_log, gdn_dt_bias, gdn_norm_g, gdn_w_out, gqa_w_in, gqa_q_norm, gqa_k_norm, gqa_w_out, mla_w_in, mla_q_norm, mla_kv_norm, mla_w_q_up, mla_w_kv_up, mla_w_out, router_w, router_b, moe_w_gate, moe_w_up, moe_w_down):
    x = jnp.concatenate([x_prompt.reshape(T_CTX, D_MODEL), x_sample.reshape(T_LAT, D_MODEL)], axis=0)
    cond = jnp.concatenate([c_ctx[None], c, jnp.zeros((MOD_ROWS - 1 - DEC_BATCH, D_MODEL), F32)], axis=0)
    mods = _ada_call(cond, w_ada, b_ada)

    st_gdn, st_k, st_v, st_ckv, st_krope = [], [], [], [], []
    for l in range(DEPTH):
        kind, li = l % N_MIXERS, l // N_MIXERS
        mods_l = mods[l]
        if kind == 0:
            o, st = _gdn_layer(x, mods_l, li, state_gdn, gdn_w_in, gdn_conv, gdn_a_log, gdn_dt_bias, gdn_norm_g)
            st_gdn.append(st)
            w_out = gdn_w_out[li]
        elif kind == 1:
            o, nk, nv = _gqa_layer(x, mods_l, li, cache_gqa_k, cache_gqa_v, gqa_w_in, gqa_q_norm, gqa_k_norm)
            st_k.append(nk)
            st_v.append(nv)
            w_out = gqa_w_out[li]
        else:
            o, nckv, nkr = _mla_layer(x, mods_l, li, cache_mla_ckv, cache_mla_krope, mla_w_in, mla_q_norm,
                                      mla_kv_norm, mla_w_q_up, mla_w_kv_up)
            st_ckv.append(nckv)
            st_krope.append(nkr)
            w_out = mla_w_out[li]
        x = _outln_call(o[0], o[1], w_out, x, mods_l, 2, ln_mix_g[l], ln_mix_b[l], name="mix_out_ln")
        routed = _router_call(x, mods_l, router_w, router_b)
        x = _moe_call(x, mods_l, routed, moe_w_gate[l], moe_w_up[l], moe_w_down[l], ln_ffn_g[l], ln_ffn_b[l])

    y_prompt = x[:T_CTX].reshape(BATCH, SEQ, D_MODEL)
    y_sample = x[T_CTX:T_ALL].reshape(DEC_BATCH, DEC_SEQ, D_MODEL)
    return (y_prompt, y_sample, jnp.stack(st_gdn, axis=1), jnp.stack(st_k, axis=1), jnp.stack(st_v, axis=1),
            jnp.stack(st_ckv, axis=1), jnp.stack(st_krope, axis=1))
```

```python
import functools
import math

import jax
import jax.numpy as jnp
from jax import lax
from jax.experimental import pallas as pl
from jax.experimental.pallas import tpu as pltpu

F32 = jnp.float32
BF16 = jnp.bfloat16

D_MODEL = 1024
BATCH = 32
SEQ = 256
DEPTH = 4
DEC_BATCH = 2
DEC_SEQ = 2048
PAST_LEN = 512
GRID_W = 64
N_MIXERS = 3

GDN_HEADS = 8
GDN_DK = 128
GDN_DV = 128
GDN_CONV = 5
GDN_CHUNK = 64
GDN_QKV = GDN_HEADS * (2 * GDN_DK + GDN_DV)
GDN_QKVZ = GDN_QKV + GDN_HEADS * GDN_DV

GQA_HEADS = 8
GQA_KV_HEADS = 2
GQA_GROUP = GQA_HEADS // GQA_KV_HEADS
GQA_HEAD_DIM = 128
GQA_IN = (GQA_HEADS + 2 * GQA_KV_HEADS) * GQA_HEAD_DIM

MLA_HEADS = 8
MLA_Q_LORA = 512
MLA_KV_LORA = 256
MLA_NOPE = 128
MLA_ROPE = 64
MLA_V = 128
MLA_IN = MLA_Q_LORA + MLA_KV_LORA + MLA_ROPE

N_EXPERTS = 16
N_GROUPS = 4
EXPERTS_PER_GROUP = N_EXPERTS // N_GROUPS
D_EXPERT = 256

ROPE_THETA = 10000.0
ALPHA = (2 * DEPTH) ** 0.25
NORM_EPS = 1e-6

T_CTX = BATCH * SEQ
T_LAT = DEC_BATCH * DEC_SEQ
T_ALL = T_CTX + T_LAT
SEG_ROWS = DEC_SEQ
N_CTX_SEGS = T_CTX // SEG_ROWS
MOD_ROWS = 8

V7X_LANES = 128
V7X_SUBLANES = 8
V7X_VMEM_BYTES = 64 * 1024 * 1024
VMEM_LIMIT = V7X_VMEM_BYTES * 7 // 8

GDN_SUPER = 256
CHUNKS_PER_SUPER = GDN_SUPER // GDN_CHUNK


def _cparams(*sem):
    return pltpu.CompilerParams(dimension_semantics=sem, vmem_limit_bytes=VMEM_LIMIT)


def _mm(a, b):
    return jnp.dot(a.astype(BF16), b.astype(BF16), preferred_element_type=F32)


def _mm_nt(a, b):
    return lax.dot_general(a.astype(BF16), b.astype(BF16), (((1,), (1,)), ((), ())),
                           preferred_element_type=F32)


def _silu(x):
    return x * (1.0 / (1.0 + jnp.exp(-x)))


def _sigmoid(x):
    return 1.0 / (1.0 + jnp.exp(-x))


def _chunk_of(idx):
    return idx >> int(math.log2(GDN_CHUNK))


def _seg_row(i, tiles_per_seg):
    return jnp.maximum(lax.div(i, tiles_per_seg) - (N_CTX_SEGS - 1), 0)


def _layer_norm(y, g, b):
    mu = jnp.mean(y, axis=-1, keepdims=True)
    yc = y - mu
    var = jnp.mean(yc * yc, axis=-1, keepdims=True)
    return yc * lax.rsqrt(var + NORM_EPS) * g + b


def _rms(x, g):
    return x * lax.rsqrt(jnp.mean(x * x, axis=-1, keepdims=True) + NORM_EPS) * g


ROW_TILES = D_MODEL // V7X_LANES


def _load_rows(ref):
    n = ref.shape[0] // ROW_TILES
    return jnp.concatenate([ref[pl.ds(s, n, stride=ROW_TILES), :] for s in range(ROW_TILES)], axis=1)


def _store_rows(ref, val):
    n = val.shape[0]
    for s in range(ROW_TILES):
        ref[pl.ds(s, n, stride=ROW_TILES), :] = val[:, s * V7X_LANES:(s + 1) * V7X_LANES]


def _slab_spec(tm, index_map):
    return pl.BlockSpec((tm * ROW_TILES, V7X_LANES), index_map)


def _ada_kernel(c_ref, w_ref, b_ref, o_ref):
    o_ref[...] = _mm(_silu(c_ref[...]), w_ref[...]) + b_ref[...]


def _ada_call(cond, w_ada, b_ada):
    tn = 1536
    n = w_ada.shape[-1]
    return pl.pallas_call(
        _ada_kernel,
        grid=(DEPTH, n // tn),
        in_specs=[
            pl.BlockSpec((MOD_ROWS, D_MODEL), lambda l, j: (0, 0)),
            pl.BlockSpec((None, D_MODEL, tn), lambda l, j: (l, 0, j)),
            pl.BlockSpec((None, 1, tn), lambda l, j: (l, 0, j)),
        ],
        out_specs=pl.BlockSpec((None, MOD_ROWS, tn), lambda l, j: (l, 0, j)),
        out_shape=jax.ShapeDtypeStruct((DEPTH, MOD_ROWS, n), F32),
        compiler_params=_cparams("arbitrary", "arbitrary"),
        name="ada_mod",
    )(cond, w_ada, b_ada.reshape(DEPTH, 1, n))


def _modproj_kernel(x_ref, sh_ref, sc_ref, w_ref, *rest, tiles_per_seg, has_extra):
    if has_extra:
        wx_ref, o_ref, ox_ref, h_scr, wb_scr = rest
    else:
        o_ref, h_scr, wb_scr = rest
    i, j = pl.program_id(0), pl.program_id(1)

    @pl.when(i == 0)
    def _():
        wb_scr[j] = w_ref[...].astype(BF16)

    @pl.when(j == 0)
    def _():
        r = _seg_row(i, tiles_per_seg)
        sh = sh_ref[pl.ds(r, 1), :]
        sc = sc_ref[pl.ds(r, 1), :]
        h = (_load_rows(x_ref) * (1.0 + sc) + sh).astype(BF16)
        h_scr[...] = h
        if has_extra:
            ox_ref[...] = _mm(h, wx_ref[...])

    o_ref[...] = jnp.dot(h_scr[...], wb_scr[j], preferred_element_type=F32)


def _modproj_call(x, mods_l, shift_idx, scale_idx, w, n_out, w_extra=None, *, tm=1024, tn=512, name):
    t = T_ALL
    has_extra = w_extra is not None
    nj = n_out // tn
    in_specs = [
        _slab_spec(tm, lambda i, j: (i, 0)),
        pl.BlockSpec((MOD_ROWS, D_MODEL), lambda i, j: (0, shift_idx)),
        pl.BlockSpec((MOD_ROWS, D_MODEL), lambda i, j: (0, scale_idx)),
        pl.BlockSpec((D_MODEL, tn), lambda i, j: (0, jnp.where(i == 0, j, nj - 1))),
    ]
    out_specs = [pl.BlockSpec((tm, tn), lambda i, j: (i, j))]
    out_shape = [jax.ShapeDtypeStruct((t, n_out), F32)]
    args = [x, mods_l, mods_l, w]
    if has_extra:
        nx = w_extra.shape[1]
        in_specs.append(pl.BlockSpec((D_MODEL, nx), lambda i, j: (0, 0)))
        out_specs.append(pl.BlockSpec((tm, nx), lambda i, j: (i, 0)))
        out_shape.append(jax.ShapeDtypeStruct((t, nx), F32))
        args.append(w_extra)
    res = pl.pallas_call(
        functools.partial(_modproj_kernel, tiles_per_seg=SEG_ROWS // tm, has_extra=has_extra),
        grid=(t // tm, nj),
        in_specs=in_specs,
        out_specs=out_specs,
        out_shape=out_shape,
        scratch_shapes=[pltpu.VMEM((tm, D_MODEL), BF16), pltpu.VMEM((nj, D_MODEL, tn), BF16)],
        compiler_params=_cparams("arbitrary", "arbitrary"),
        name=name,
    )(*args)
    return res if has_extra else res[0]


def _rmsproj_kernel(x_ref, g_ref, w_ref, *rest, normalize, emit_xn):
    if emit_xn:
        o_ref, xn_ref, h_scr = rest
    else:
        o_ref, h_scr = rest
    j = pl.program_id(1)

    @pl.when(j == 0)
    def _():
        x = x_ref[...]
        xn = _rms(x, g_ref[...]) if normalize else x
        h_scr[...] = xn.astype(BF16)
        if emit_xn:
            xn_ref[...] = xn

    o_ref[...] = _mm(h_scr[...], w_ref[...])


def _rmsproj_call(x, col_block, k, g, w, *, normalize=True, emit_xn=False, tm=1024, tn=512, name):
    t = x.shape[0]
    n_out = w.shape[1]
    out_specs = [pl.BlockSpec((tm, tn), lambda i, j: (i, j))]
    out_shape = [jax.ShapeDtypeStruct((t, n_out), F32)]
    if emit_xn:
        out_specs.append(pl.BlockSpec((tm, k), lambda i, j: (i, 0)))
        out_shape.append(jax.ShapeDtypeStruct((t, k), F32))
    res = pl.pallas_call(
        functools.partial(_rmsproj_kernel, normalize=normalize, emit_xn=emit_xn),
        grid=(t // tm, n_out // tn),
        in_specs=[
            pl.BlockSpec((tm, k), lambda i, j: (i, col_block)),
            pl.BlockSpec((1, k), lambda i, j: (0, 0)),
            pl.BlockSpec((k, tn), lambda i, j: (0, j)),
        ],
        out_specs=out_specs,
        out_shape=out_shape,
        scratch_shapes=[pltpu.VMEM((tm, k), BF16)],
        compiler_params=_cparams("arbitrary", "arbitrary"),
        name=name,
    )(x, g.reshape(1, k), w)
    return res if emit_xn else res[0]


def _outln_kernel(oc_ref, ol_ref, w_ref, x_ref, gate_ref, g_ref, b_ref, out_ref, wb_scr, *, tiles_per_seg, ctx_tiles):
    i = pl.program_id(0)

    @pl.when(i == 0)
    def _():
        wb_scr[...] = w_ref[...].astype(BF16)

    gate = gate_ref[pl.ds(_seg_row(i, tiles_per_seg), 1), :]
    o = jnp.where(i < ctx_tiles, oc_ref[...], ol_ref[...])
    f = jnp.dot(o.astype(BF16), wb_scr[...], preferred_element_type=F32)
    _store_rows(out_ref, _layer_norm(ALPHA * _load_rows(x_ref) + (1.0 + gate) * f, g_ref[...], b_ref[...]))


def _outln_call(o_ctx, o_lat, w, x, mods_l, gate_idx, g, b, *, tm=512, name):
    k = o_ctx.shape[1]
    t = T_ALL
    ctx_tiles = T_CTX // tm
    return pl.pallas_call(
        functools.partial(_outln_kernel, tiles_per_seg=SEG_ROWS // tm, ctx_tiles=ctx_tiles),
        grid=(t // tm,),
        in_specs=[
            pl.BlockSpec((tm, k), lambda i: (jnp.minimum(i, ctx_tiles - 1), 0)),
            pl.BlockSpec((tm, k), lambda i: (jnp.maximum(i - ctx_tiles, 0), 0)),
            pl.BlockSpec((k, D_MODEL), lambda i: (0, 0)),
            _slab_spec(tm, lambda i: (i, 0)),
            pl.BlockSpec((MOD_ROWS, D_MODEL), lambda i: (0, gate_idx)),
            pl.BlockSpec((1, D_MODEL), lambda i: (0, 0)),
            pl.BlockSpec((1, D_MODEL), lambda i: (0, 0)),
        ],
        out_specs=_slab_spec(tm, lambda i: (i, 0)),
        out_shape=jax.ShapeDtypeStruct((t * ROW_TILES, V7X_LANES), F32),
        scratch_shapes=[pltpu.VMEM((k, D_MODEL), BF16)],
        compiler_params=_cparams("arbitrary"),
        name=name,
    )(o_ctx, o_lat, w, x, mods_l, g.reshape(1, D_MODEL), b.reshape(1, D_MODEL))


def _gdn_gates_kernel(ab_ref, alog_ref, dt_ref, o_ref):
    ab = ab_ref[...]
    tm = ab.shape[0]
    x = ab + dt_ref[...]
    softplus = jnp.maximum(x, 0.0) + jnp.log(1.0 + jnp.exp(-jnp.abs(x)))
    g = -jnp.exp(alog_ref[...]) * softplus
    ri = lax.broadcasted_iota(jnp.int32, (tm, tm), 0)
    ci = lax.broadcasted_iota(jnp.int32, (tm, tm), 1)
    same = _chunk_of(ri) == _chunk_of(ci)
    lower = jnp.where(same & (ri >= ci), 1.0, 0.0).astype(F32)
    upper = jnp.where(same & (ri <= ci), 1.0, 0.0).astype(F32)
    pre = jnp.dot(lower, g, precision=lax.Precision.HIGHEST, preferred_element_type=F32)
    suf = jnp.dot(upper, g, precision=lax.Precision.HIGHEST, preferred_element_type=F32)
    lane = lax.broadcasted_iota(jnp.int32, ab.shape, 1)
    o_ref[...] = jnp.where(lane < GDN_HEADS, pre, jnp.where(lane < 2 * GDN_HEADS, suf, _sigmoid(ab)))


def _gdn_gates_call(ab, a_log, dt_bias):
    t, n = ab.shape
    tm = GDN_SUPER
    pad = n - 2 * GDN_HEADS
    alog_row = jnp.pad(a_log.reshape(1, -1), ((0, 0), (0, pad)))
    dt_row = jnp.pad(dt_bias.reshape(1, -1), ((0, 0), (0, pad)))
    return pl.pallas_call(
        _gdn_gates_kernel,
        grid=(t // tm,),
        in_specs=[
            pl.BlockSpec((tm, n), lambda i: (i, 0)),
            pl.BlockSpec((1, n), lambda i: (0, 0)),
            pl.BlockSpec((1, n), lambda i: (0, 0)),
        ],
        out_specs=pl.BlockSpec((tm, n), lambda i: (i, 0)),
        out_shape=jax.ShapeDtypeStruct((t, n), F32),
        compiler_params=_cparams("arbitrary"),
        name="gdn_gates",
    )(ab, alog_row, dt_row)


def _loop(n, body, max_static=4):
    if n <= max_static:
        for i in range(n):
            body(i)
    else:
        def wrapped(i, carry):
            body(i)
            return carry
        lax.fori_loop(0, n, wrapped, 0)


def _gdn_core_kernel(q_ref, k_ref, v_ref, z_ref, cq_ref, ck_ref, cv_ref, gc_ref, gr_ref, ng_ref, *rest,
                     n_tok, has_s0, hps):
    if has_s0:
        s0_ref, o_ref, st_ref = rest[:3]
        scr = rest[3:]
    else:
        o_ref, st_ref = rest[:2]
        scr = rest[2:]
    pad_scr, kn_scr, u_scr, w_scr, qe_scr, qk_scr, oo_scr, s_scr = scr
    n_super = n_tok // GDN_SUPER
    n_chunk = n_tok // GDN_CHUNK
    halo = V7X_SUBLANES
    sb = GDN_SUPER
    ck = GDN_CHUNK
    dk = GDN_DK

    zeros_halo = jnp.zeros((halo, dk), F32)
    for hd in range(hps):
        for a, src in enumerate((q_ref, k_ref, v_ref)):
            pad_scr[3 * hd + a, 0:halo, :] = zeros_halo
            pad_scr[3 * hd + a, halo:halo + n_tok, :] = src[:, hd * dk:(hd + 1) * dk]
            pad_scr[3 * hd + a, halo + n_tok:2 * halo + n_tok, :] = zeros_halo
        for d in range(2):
            s_scr[2 * hd + d] = s0_ref[d, hd] if has_s0 else jnp.zeros((dk, GDN_DV), F32)

    ri = lax.broadcasted_iota(jnp.int32, (sb, sb), 0)
    ci = lax.broadcasted_iota(jnp.int32, (sb, sb), 1)
    same_chunk = _chunk_of(ri) == _chunk_of(ci)
    eye = jnp.where(ri == ci, 1.0, 0.0).astype(F32)
    n_levels = int(math.log2(ck))

    def level_mask(l):
        return ((ri >> (l + 1)) == (ci >> (l + 1))) & ((ri >> l) != (ci >> l))

    def conv_silu(a, w_ref, hd, r0):
        xp = pad_scr[a, pl.ds(r0, sb + 2 * halo), :]
        acc = None
        for j in range(GDN_CONV):
            shift = (GDN_CONV // 2 - j) % (sb + 2 * halo)
            tap = xp if shift == 0 else pltpu.roll(xp, shift, 0)
            term = w_ref[j:j + 1, hd * dk:(hd + 1) * dk] * tap[halo:halo + sb, :]
            acc = term if acc is None else acc + term
        return _silu(acc)

    chains = [(hd, d) for hd in range(hps) for d in range(2)]

    a_unroll = max(1, min(n_super, 8 // (2 * hps)))

    def phase_a(it):
        units = [(it * a_unroll + u, hd) for u in range(a_unroll) for hd in range(hps)]
        r0s = [pl.multiple_of(s * sb, sb) for s, _ in units]
        qs, ks, vs, ps, qks = [], [], [], [], []
        for (s, hd), r0 in zip(units, r0s):
            q = conv_silu(3 * hd, cq_ref, hd, r0)
            k = conv_silu(3 * hd + 1, ck_ref, hd, r0)
            v = conv_silu(3 * hd + 2, cv_ref, hd, r0)
            q = q * lax.rsqrt(jnp.sum(q * q, axis=-1, keepdims=True) + NORM_EPS) * (dk ** -0.5)
            k = k * lax.rsqrt(jnp.sum(k * k, axis=-1, keepdims=True) + NORM_EPS)
            kn_scr[hd, pl.ds(r0, sb), :] = k
            kb = k.astype(BF16)
            qs.append(q)
            ks.append(k)
            vs.append(v)
            ps.append(_mm_nt(kb, kb))
            qks.append(_mm_nt(q, kb))
        a_chains = [(n, d) for n in range(len(units)) for d in range(2)]
        a_mats, ts, rhs = [], [], []
        for n, d in a_chains:
            (s, hd), r0 = units[n], r0s[n]
            hdd = 2 * hd + d
            gcol = gc_ref[hd, pl.ds(r0, sb), d:d + 1]
            bcol = gc_ref[hd, pl.ds(r0, sb), 2 + d:3 + d]
            grow = gr_ref[hd, s, d:d + 1, :]
            if d == 0:
                mask, strict = same_chunk & (ri >= ci), same_chunk & (ri > ci)
            else:
                mask, strict = same_chunk & (ri <= ci), same_chunk & (ri < ci)
            dec = jnp.exp(jnp.where(mask, gcol - grow, -jnp.inf))
            a = jnp.where(strict, ps[n] * bcol * dec, 0.0)
            a_mats.append(a)
            ts.append(eye - jnp.where(level_mask(0), a, 0.0))
            eg = jnp.exp(gcol)
            rhs.append(jnp.concatenate([vs[n] * bcol, ks[n] * (bcol * eg)], axis=1).astype(BF16))
            qe_scr[hdd, pl.ds(r0, sb), :] = qs[n] * eg
            qkd = qks[n] * dec
            for c in range(CHUNKS_PER_SUPER):
                qk_scr[hdd, pl.ds(r0 + c * ck, ck), :] = qkd[c * ck:(c + 1) * ck, c * ck:(c + 1) * ck]
        for l in range(1, n_levels):
            ys = [_mm(jnp.where(level_mask(l), a_mats[i], 0.0), ts[i]) for i in range(len(a_chains))]
            ts = [ts[i] - _mm(ts[i], ys[i]) for i in range(len(a_chains))]
        for i, (n, d) in enumerate(a_chains):
            (s, hd), r0 = units[n], r0s[n]
            uw = _mm(ts[i], rhs[i])
            u_scr[2 * hd + d, pl.ds(r0, sb), :] = uw[:, :GDN_DV]
            w_scr[2 * hd + d, pl.ds(r0, sb), :] = uw[:, GDN_DV:]

    _loop(n_super // a_unroll, phase_a, max_static=1)

    def phase_b(i):
        rows, ws_qs, v_new, kws, gls = [], [], [], [], []
        for hd, d in chains:
            hdd = 2 * hd + d
            c = i if d == 0 else n_chunk - 1 - i
            r = pl.multiple_of(c * ck, ck)
            r_last = r + (ck - 1) if d == 0 else r
            rows.append(r)
            gcol = gc_ref[hd, pl.ds(r, ck), d:d + 1]
            gl = gc_ref[hd, pl.ds(r_last, 1), d:d + 1]
            gls.append(gl)
            kws.append((kn_scr[hd, pl.ds(r, ck), :] * jnp.exp(gl - gcol)).T)
            wq = jnp.concatenate([w_scr[hdd, pl.ds(r, ck), :], qe_scr[hdd, pl.ds(r, ck), :]], axis=0)
            ws_qs.append(_mm(wq, s_scr[hdd]))
        for j, (hd, d) in enumerate(chains):
            hdd = 2 * hd + d
            vn = u_scr[hdd, pl.ds(rows[j], ck), :] - ws_qs[j][:ck]
            v_new.append(vn)
            oo_scr[hdd, pl.ds(rows[j], ck), :] = ws_qs[j][ck:] + _mm(qk_scr[hdd, pl.ds(rows[j], ck), :], vn)
        for j, (hd, d) in enumerate(chains):
            hdd = 2 * hd + d
            s_scr[hdd] = s_scr[hdd] * jnp.exp(gls[j]) + _mm(kws[j], v_new[j])

    _loop(n_chunk, phase_b)

    def finish(s):
        r0 = pl.multiple_of(s * sb, sb)
        for hd in range(hps):
            o = oo_scr[2 * hd, pl.ds(r0, sb), :] + oo_scr[2 * hd + 1, pl.ds(r0, sb), :]
            z = z_ref[pl.ds(r0, sb), hd * GDN_DV:(hd + 1) * GDN_DV]
            o_ref[pl.ds(r0, sb), hd * GDN_DV:(hd + 1) * GDN_DV] = _rms(o, ng_ref[...]) * _silu(z)

    _loop(n_super, finish)
    for hd in range(hps):
        for d in range(2):
            st_ref[d, hd] = s_scr[2 * hd + d]


def _gdn_core_call(qkvz, conv_w, gc, gr, norm_g, s0, *, batch, n_tok, row_block_off, hps, name):
    has_s0 = s0 is not None
    h = GDN_HEADS
    dk = GDN_DK
    n_super = n_tok // GDN_SUPER
    hb = h // hps

    mode = dict(pipeline_mode=pl.Buffered(1)) if n_tok > GDN_SUPER else {}

    def col(sec):
        return pl.BlockSpec((n_tok, dk * hps), lambda b, hh: (b + row_block_off, sec * hb + hh), **mode)

    def cw(sec):
        return pl.BlockSpec((GDN_CONV, dk * hps), lambda b, hh: (0, sec * hb + hh))

    in_specs = [col(0), col(1), col(2), col(3), cw(0), cw(1), cw(2),
                pl.BlockSpec((None, hps, n_tok, 4), lambda b, hh: (b, hh, 0, 0)),
                pl.BlockSpec((None, hps, n_super, 4, GDN_SUPER), lambda b, hh: (b, hh, 0, 0, 0)),
                pl.BlockSpec((1, GDN_DV), lambda b, hh: (0, 0))]
    args = [qkvz, qkvz, qkvz, qkvz, conv_w, conv_w, conv_w, gc, gr, norm_g.reshape(1, GDN_DV)]
    st_spec = pl.BlockSpec((None, 2, hps, dk, GDN_DV), lambda b, hh: (b, 0, hh, 0, 0))
    if has_s0:
        in_specs.append(st_spec)
        args.append(s0)
    return pl.pallas_call(
        functools.partial(_gdn_core_kernel, n_tok=n_tok, has_s0=has_s0, hps=hps),
        grid=(batch, hb),
        in_specs=in_specs,
        out_specs=[pl.BlockSpec((n_tok, GDN_DV * hps), lambda b, hh: (b, hh)), st_spec],
        out_shape=[jax.ShapeDtypeStruct((batch * n_tok, h * GDN_DV), F32),
                   jax.ShapeDtypeStruct((batch, 2, h, dk, GDN_DV), F32)],
        scratch_shapes=[
            pltpu.VMEM((3 * hps, n_tok + 2 * V7X_SUBLANES, dk), F32),
            pltpu.VMEM((hps, n_tok, dk), F32),
            pltpu.VMEM((2 * hps, n_tok, GDN_DV), F32),
            pltpu.VMEM((2 * hps, n_tok, dk), F32),
            pltpu.VMEM((2 * hps, n_tok, dk), F32),
            pltpu.VMEM((2 * hps, n_tok, GDN_CHUNK), F32),
            pltpu.VMEM((2 * hps, n_tok, GDN_DV), F32),
            pltpu.VMEM((2 * hps, dk, GDN_DV), F32),
        ],
        compiler_params=_cparams("arbitrary", "arbitrary"),
        name=name,
    )(*args)


def _gdn_layer(x, mods_l, li, state_gdn, gdn_w_in, gdn_conv, gdn_a_log, gdn_dt_bias, gdn_norm_g):
    w_in = gdn_w_in[li]
    w_ab = jnp.pad(w_in[:, GDN_QKVZ:], ((0, 0), (0, V7X_LANES - 4 * GDN_HEADS)))
    qkvz, ab = _modproj_call(x, mods_l, 0, 1, w_in, GDN_QKVZ, w_ab, name="gdn_in")
    gates = _gdn_gates_call(ab, gdn_a_log[li], gdn_dt_bias[li])[:, :4 * GDN_HEADS]

    def head_layouts(g, batch, n_tok):
        g = g.reshape(batch, n_tok, 4, GDN_HEADS)
        gc = jnp.transpose(g, (0, 3, 1, 2))
        gr = jnp.transpose(g, (0, 3, 2, 1)).reshape(batch, GDN_HEADS, 4, n_tok // GDN_SUPER, GDN_SUPER)
        return gc, jnp.transpose(gr, (0, 1, 3, 2, 4))

    gc_p, gr_p = head_layouts(gates[:T_CTX], BATCH, SEQ)
    gc_s, gr_s = head_layouts(gates[T_CTX:], DEC_BATCH, DEC_SEQ)
    o_p, st_p = _gdn_core_call(qkvz, gdn_conv[li], gc_p, gr_p, gdn_norm_g[li], None,
                               batch=BATCH, n_tok=SEQ, row_block_off=0, hps=4, name="gdn_core_ctx")
    o_s, _ = _gdn_core_call(qkvz, gdn_conv[li], gc_s, gr_s, gdn_norm_g[li], state_gdn[:, li],
                            batch=DEC_BATCH, n_tok=DEC_SEQ, row_block_off=T_CTX // DEC_SEQ, hps=2,
                            name="gdn_core_lat")
    return (o_p, o_s), st_p


def _rope_tables(n_tokens, rot_dim, reps):
    n_rows = n_tokens // GRID_W
    row = jnp.repeat(jnp.arange(n_rows, dtype=F32), GRID_W)
    colp = jnp.tile(jnp.arange(GRID_W, dtype=F32), n_rows)
    n_freq = rot_dim // 4
    inv_freq = ROPE_THETA ** (-jnp.arange(n_freq, dtype=F32) / n_freq)
    ra, ca = row[:, None] * inv_freq, colp[:, None] * inv_freq
    cos = jnp.concatenate([jnp.cos(ra), jnp.cos(ra), jnp.cos(ca), jnp.cos(ca)], axis=-1)
    sin = jnp.concatenate([-jnp.sin(ra), jnp.sin(ra), -jnp.sin(ca), jnp.sin(ca)], axis=-1)
    return jnp.tile(cos, (1, reps)), jnp.tile(sin, (1, reps))


def _rope(x, cos, sin, quarter):
    n = x.shape[-1]
    lane = lax.broadcasted_iota(jnp.int32, x.shape, x.ndim - 1)
    first = (lane & (2 * quarter - 1)) < quarter
    partner = jnp.where(first, pltpu.roll(x, n - quarter, x.ndim - 1), pltpu.roll(x, quarter, x.ndim - 1))
    return x * cos + partner * sin


def _gqa_prep_kernel(kv_ref, g_ref, *rest, rope):
    if rope:
        cos_ref, sin_ref, k_ref, v_ref = rest
    else:
        k_ref, v_ref = rest
    kv = kv_ref[...]
    nk = GQA_KV_HEADS * GQA_HEAD_DIM
    for hh in range(GQA_KV_HEADS):
        k = _rms(kv[:, hh * GQA_HEAD_DIM:(hh + 1) * GQA_HEAD_DIM], g_ref[...])
        if rope:
            k = _rope(k, cos_ref[...], sin_ref[...], GQA_HEAD_DIM // 4)
        k_ref[:, hh * GQA_HEAD_DIM:(hh + 1) * GQA_HEAD_DIM] = k
    v_ref[...] = kv[:, nk:]


def _gqa_prep_call(proj, k_norm, row_off, n_rows, tables, *, tm=512, name):
    rope = tables is not None
    nkv = 2 * GQA_KV_HEADS * GQA_HEAD_DIM
    kv_col = GQA_HEADS * GQA_HEAD_DIM // nkv
    off = row_off // tm
    in_specs = [pl.BlockSpec((tm, nkv), lambda i: (i + off, kv_col)),
                pl.BlockSpec((1, GQA_HEAD_DIM), lambda i: (0, 0))]
    args = [proj, k_norm.reshape(1, GQA_HEAD_DIM)]
    if rope:
        per_seq = DEC_SEQ // tm
        in_specs += [pl.BlockSpec((tm, GQA_HEAD_DIM), lambda i: (i % per_seq, 0))] * 2
        args += list(tables)
    half = nkv // 2
    return pl.pallas_call(
        functools.partial(_gqa_prep_kernel, rope=rope),
        grid=(n_rows // tm,),
        in_specs=in_specs,
        out_specs=[pl.BlockSpec((tm, half), lambda i: (i, 0))] * 2,
        out_shape=[jax.ShapeDtypeStruct((n_rows, half), F32)] * 2,
        compiler_params=_cparams("arbitrary"),
        name=name,
    )(*args)


def _attend(chains, scale):
    scores = [[_mm_nt(q, k) * scale for k in ks] for q, ks, _ in chains]
    probs, dens = [], []
    for sc in scores:
        m = None
        for s in sc:
            sm = jnp.max(s, axis=-1, keepdims=True)
            m = sm if m is None else jnp.maximum(m, sm)
        ps = [jnp.exp(s - m) for s in sc]
        den = None
        for p in ps:
            r = jnp.sum(p, axis=-1, keepdims=True)
            den = r if den is None else den + r
        probs.append([p.astype(BF16) for p in ps])
        dens.append(den)
    outs = []
    for (_, _, vs), ps, den in zip(chains, probs, dens):
        acc = None
        for p, v in zip(ps, vs):
            pv = _mm(p, v)
            acc = pv if acc is None else acc + pv
        outs.append(acc / den)
    return outs


def _gqa_attn_kernel(q_ref, qg_ref, *rest, n_seg, rope, kv_per_step):
    if rope:
        cos_ref, sin_ref = rest[:2]
        rest = rest[2:]
    seg_refs = rest[:2 * n_seg]
    o_ref = rest[2 * n_seg]
    dh = GQA_HEAD_DIM
    tq = q_ref.shape[0]
    heads_per_chain = GQA_GROUP // 2
    chains, heads = [], []
    for kh in range(kv_per_step):
        ks = [seg_refs[2 * sgi][:, kh * dh:(kh + 1) * dh].astype(BF16) for sgi in range(n_seg)]
        vs = [seg_refs[2 * sgi + 1][:, kh * dh:(kh + 1) * dh].astype(BF16) for sgi in range(n_seg)]
        for g0 in range(0, GQA_GROUP, heads_per_chain):
            qs = []
            for g in range(g0, g0 + heads_per_chain):
                c0 = (kh * GQA_GROUP + g) * dh
                qh = _rms(q_ref[:, c0:c0 + dh], qg_ref[...])
                if rope:
                    qh = _rope(qh, cos_ref[...], sin_ref[...], dh // 4)
                qs.append(qh.astype(BF16))
            chains.append((jnp.concatenate(qs, axis=0), ks, vs))
            heads.append(kh * GQA_GROUP + g0)
    for o, h0 in zip(_attend(chains, dh ** -0.5), heads):
        for g in range(heads_per_chain):
            o_ref[:, (h0 + g) * dh:(h0 + g + 1) * dh] = o[g * tq:(g + 1) * tq]


def _gqa_attn_call(proj, q_norm, segs, tables, *, batch, n_q, row_off, tq, kv_per_step, name):
    rope = tables is not None
    dh = GQA_HEAD_DIM
    qw = GQA_GROUP * dh * kv_per_step
    kw = dh * kv_per_step
    nqb = n_q // tq
    off = row_off // tq
    in_specs = [pl.BlockSpec((tq, qw), lambda b, h, i: (b * nqb + i + off, h)),
                pl.BlockSpec((1, dh), lambda b, h, i: (0, 0))]
    args = [proj, q_norm.reshape(1, dh)]
    if rope:
        in_specs += [pl.BlockSpec((tq, dh), lambda b, h, i: (i, 0))] * 2
        args += list(tables)
    for k2d, v2d, rows in segs:
        in_specs += [pl.BlockSpec((rows, kw), lambda b, h, i: (b, h))] * 2
        args += [k2d, v2d]
    return pl.pallas_call(
        functools.partial(_gqa_attn_kernel, n_seg=len(segs), rope=rope, kv_per_step=kv_per_step),
        grid=(batch, GQA_KV_HEADS // kv_per_step, nqb),
        in_specs=in_specs,
        out_specs=pl.BlockSpec((tq, qw), lambda b, h, i: (b * nqb + i, h)),
        out_shape=jax.ShapeDtypeStruct((batch * n_q, GQA_HEADS * dh), F32),
        compiler_params=_cparams("arbitrary", "arbitrary", "arbitrary"),
        name=name,
    )(*args)


def _gqa_layer(x, mods_l, li, cache_k, cache_v, gqa_w_in, gqa_q_norm, gqa_k_norm):
    proj = _modproj_call(x, mods_l, 0, 1, gqa_w_in[li], GQA_IN, name="gqa_in")
    tables = _rope_tables(DEC_SEQ, GQA_HEAD_DIM, 1)
    nkv = GQA_KV_HEADS * GQA_HEAD_DIM
    k_p, v_p = _gqa_prep_call(proj, gqa_k_norm[li], 0, T_CTX, None, name="gqa_prep_ctx")
    k_s, v_s = _gqa_prep_call(proj, gqa_k_norm[li], T_CTX, T_LAT, tables, name="gqa_prep_lat")
    o_p = _gqa_attn_call(proj, gqa_q_norm[li], [(k_p, v_p, SEQ)], None, batch=BATCH, n_q=SEQ, row_off=0,
                         tq=SEQ, kv_per_step=GQA_KV_HEADS, name="gqa_attn_ctx")
    ck = cache_k[:, li].reshape(DEC_BATCH * PAST_LEN, nkv)
    cv = cache_v[:, li].reshape(DEC_BATCH * PAST_LEN, nkv)
    o_s = _gqa_attn_call(proj, gqa_q_norm[li], [(ck, cv, PAST_LEN), (k_s, v_s, DEC_SEQ)], tables,
                         batch=DEC_BATCH, n_q=DEC_SEQ, row_off=T_CTX, tq=256, kv_per_step=1, name="gqa_attn_lat")
    new_k = k_p.reshape(BATCH, SEQ, GQA_KV_HEADS, GQA_HEAD_DIM)
    new_v = v_p.reshape(BATCH, SEQ, GQA_KV_HEADS, GQA_HEAD_DIM)
    return (o_p, o_s), new_k, new_v


def _krope_kernel(x_ref, *rest, rope):
    if rope:
        cos_ref, sin_ref, o_ref = rest
    else:
        (o_ref,) = rest
    x = x_ref[...]
    x2 = jnp.concatenate([x, x], axis=1)
    if rope:
        x2 = _rope(x2, cos_ref[...], sin_ref[...], MLA_ROPE // 4)
    o_ref[...] = x2


def _krope_call(krope, tables, *, tm=512, name):
    rope = tables is not None
    n = krope.shape[0]
    in_specs = [pl.BlockSpec((tm, MLA_ROPE), lambda i: (i, 0))]
    args = [krope]
    if rope:
        per_seq = DEC_SEQ // tm
        in_specs += [pl.BlockSpec((tm, 2 * MLA_ROPE), lambda i: (i % per_seq, 0))] * 2
        args += list(tables)
    return pl.pallas_call(
        functools.partial(_krope_kernel, rope=rope),
        grid=(n // tm,),
        in_specs=in_specs,
        out_specs=pl.BlockSpec((tm, 2 * MLA_ROPE), lambda i: (i, 0)),
        out_shape=jax.ShapeDtypeStruct((n, 2 * MLA_ROPE), F32),
        compiler_params=_cparams("arbitrary"),
        name=name,
    )(*args)


def _mla_attn_kernel(qn_ref, qr_ref, *rest, n_seg, rope, heads_per_step):
    if rope:
        cos_ref, sin_ref = rest[:2]
        rest = rest[2:]
    seg_refs = rest[:2 * n_seg]
    o_ref = rest[2 * n_seg]
    scale = (MLA_NOPE + MLA_ROPE) ** -0.5
    h0 = pl.program_id(1) * heads_per_step
    pair_w = 2 * MLA_ROPE
    tq = qn_ref.shape[0]
    row_splits = 2 if heads_per_step == 1 else 1
    rows = tq // row_splits
    chains = []
    for j in range(heads_per_step):
        qn = qn_ref[:, j * MLA_NOPE:(j + 1) * MLA_NOPE]
        pair = qr_ref[:, (j // 2) * pair_w:(j // 2 + 1) * pair_w]
        if rope:
            pair = _rope(pair, cos_ref[...], sin_ref[...], MLA_ROPE // 4)
        lane = lax.broadcasted_iota(jnp.int32, pair.shape, 1)
        mine = (lane >> int(math.log2(MLA_ROPE))) == ((h0 + j) & 1)
        q = jnp.concatenate([qn, jnp.where(mine, pair, 0.0)], axis=1).astype(BF16)
        ks, vs = [], []
        for sgi in range(n_seg):
            kv = seg_refs[2 * sgi][:, j * (MLA_NOPE + MLA_V):(j + 1) * (MLA_NOPE + MLA_V)]
            ks.append(jnp.concatenate([kv[:, :MLA_NOPE], seg_refs[2 * sgi + 1][...]], axis=1).astype(BF16))
            vs.append(kv[:, MLA_NOPE:].astype(BF16))
        for c in range(row_splits):
            chains.append((q[c * rows:(c + 1) * rows], ks, vs))
    outs = _attend(chains, scale)
    for j in range(heads_per_step):
        for c in range(row_splits):
            o_ref[c * rows:(c + 1) * rows, j * MLA_V:(j + 1) * MLA_V] = outs[j * row_splits + c]


def _mla_attn_call(q, segs, tables, *, batch, n_q, row_off, tq, heads_per_step, name):
    rope = tables is not None
    hps = heads_per_step
    nqb = n_q // tq
    off = row_off // tq
    rope_w = max(2 * MLA_ROPE, MLA_ROPE * hps)
    rope_base = MLA_HEADS * MLA_NOPE // rope_w

    def rope_idx(b, h, i):
        return (b * nqb + i + off, rope_base + (h * hps * MLA_ROPE) // rope_w)

    in_specs = [pl.BlockSpec((tq, MLA_NOPE * hps), lambda b, h, i: (b * nqb + i + off, h)),
                pl.BlockSpec((tq, rope_w), rope_idx)]
    args = [q, q]
    if rope:
        in_specs += [pl.BlockSpec((tq, 2 * MLA_ROPE), lambda b, h, i: (i, 0))] * 2
        args += list(tables)
    for kv2d, kr2, rows, kv_row_off in segs:
        kv_off = kv_row_off // rows
        in_specs += [pl.BlockSpec((rows, (MLA_NOPE + MLA_V) * hps), lambda b, h, i, kv_off=kv_off: (b + kv_off, h)),
                     pl.BlockSpec((rows, 2 * MLA_ROPE), lambda b, h, i: (b, 0))]
        args += [kv2d, kr2]
    return pl.pallas_call(
        functools.partial(_mla_attn_kernel, n_seg=len(segs), rope=rope, heads_per_step=hps),
        grid=(batch, MLA_HEADS // hps, nqb),
        in_specs=in_specs,
        out_specs=pl.BlockSpec((tq, MLA_V * hps), lambda b, h, i: (b * nqb + i, h)),
        out_shape=jax.ShapeDtypeStruct((batch * n_q, MLA_HEADS * MLA_V), F32),
        compiler_params=_cparams("arbitrary", "arbitrary", "arbitrary"),
        name=name,
    )(*args)


def _mla_layer(x, mods_l, li, cache_ckv, cache_krope, mla_w_in, mla_q_norm, mla_kv_norm, mla_w_q_up, mla_w_kv_up):
    w_in = jnp.pad(mla_w_in[li], ((0, 0), (0, 7 * V7X_LANES - MLA_IN)))
    proj = _modproj_call(x, mods_l, 0, 1, w_in, 7 * V7X_LANES, tn=7 * V7X_LANES, name="mla_in")
    wq = mla_w_q_up[li].reshape(MLA_Q_LORA, MLA_HEADS, MLA_NOPE + MLA_ROPE)
    wq = jnp.concatenate([wq[:, :, :MLA_NOPE].reshape(MLA_Q_LORA, -1), wq[:, :, MLA_NOPE:].reshape(MLA_Q_LORA, -1)], axis=1)
    q = _rmsproj_call(proj, 0, MLA_Q_LORA, mla_q_norm[li], wq, name="mla_q_up")
    kv, ckv = _rmsproj_call(proj, MLA_Q_LORA // MLA_KV_LORA, MLA_KV_LORA, mla_kv_norm[li], mla_w_kv_up[li],
                            emit_xn=True, name="mla_kv_up")
    krope = proj[:, MLA_Q_LORA + MLA_KV_LORA:MLA_IN]
    tables = _rope_tables(DEC_SEQ, MLA_ROPE, 2)
    kr2_p = _krope_call(krope[:T_CTX], None, name="mla_krope_ctx")
    kr2_s = _krope_call(krope[T_CTX:], tables, name="mla_krope_lat")
    ctx_ckv = cache_ckv[:, li].reshape(DEC_BATCH * PAST_LEN, MLA_KV_LORA)
    kv_c = _rmsproj_call(ctx_ckv, 0, MLA_KV_LORA, jnp.ones((MLA_KV_LORA,), F32), mla_w_kv_up[li], normalize=False,
                         name="mla_kv_up_cache")
    kr2_c = _krope_call(cache_krope[:, li].reshape(DEC_BATCH * PAST_LEN, MLA_ROPE), None, name="mla_krope_cache")
    o_p = _mla_attn_call(q, [(kv, kr2_p, SEQ, 0)], None, batch=BATCH, n_q=SEQ, row_off=0, tq=SEQ,
                         heads_per_step=MLA_HEADS, name="mla_attn_ctx")
    o_s = _mla_attn_call(q, [(kv_c, kr2_c, PAST_LEN, 0), (kv, kr2_s, DEC_SEQ, T_CTX)], tables, batch=DEC_BATCH,
                         n_q=DEC_SEQ, row_off=T_CTX, tq=512, heads_per_step=1, name="mla_attn_lat")
    new_ckv = ckv[:T_CTX].reshape(BATCH, SEQ, MLA_KV_LORA)
    new_krope = krope[:T_CTX].reshape(BATCH, SEQ, MLA_ROPE)
    return (o_p, o_s), new_ckv, new_krope


def _router_kernel(x_ref, sh_ref, sc_ref, rw_ref, rb_ref, o_ref, *, tiles_per_seg):
    i = pl.program_id(0)
    r = _seg_row(i, tiles_per_seg)
    h = _load_rows(x_ref) * (1.0 + sc_ref[pl.ds(r, 1), :]) + sh_ref[pl.ds(r, 1), :]
    logits = lax.dot_general(rw_ref[...], h, (((1,), (1,)), ((), ())), precision=lax.Precision.HIGHEST,
                             preferred_element_type=F32)[:N_EXPERTS]
    scores = _sigmoid(logits)
    sel = scores + rb_ref[0:N_EXPERTS, :]
    tm = sel.shape[1]
    neg = -jnp.inf
    gs = []
    for g in range(N_GROUPS):
        a, b, c, d = (sel[EXPERTS_PER_GROUP * g + j:EXPERTS_PER_GROUP * g + j + 1, :] for j in range(4))
        gs.append(jnp.maximum(jnp.maximum(jnp.maximum(a + b, a + c), jnp.maximum(a + d, b + c)),
                              jnp.maximum(b + d, c + d)))
    gmax = jnp.maximum(jnp.maximum(gs[0], gs[1]), jnp.maximum(gs[2], gs[3]))
    best = jnp.where(gs[0] == gmax, 0, jnp.where(gs[1] == gmax, 1, jnp.where(gs[2] == gmax, 2, 3)))
    eid = lax.broadcasted_iota(jnp.int32, (N_EXPERTS, tm), 0)
    masked = jnp.where((eid >> int(math.log2(EXPERTS_PER_GROUP))) == best, sel, neg)
    m1 = jnp.max(masked, axis=0, keepdims=True)
    i1 = jnp.min(jnp.where(masked == m1, eid, N_EXPERTS), axis=0, keepdims=True)
    masked2 = jnp.where(eid == i1, neg, masked)
    m2 = jnp.max(masked2, axis=0, keepdims=True)
    i2 = jnp.min(jnp.where(masked2 == m2, eid, N_EXPERTS), axis=0, keepdims=True)
    w1 = jnp.sum(jnp.where(eid == i1, scores, 0.0), axis=0, keepdims=True)
    w2 = jnp.sum(jnp.where(eid == i2, scores, 0.0), axis=0, keepdims=True)
    tot = w1 + w2
    comb = jnp.where(eid == i1, w1 / tot, 0.0) + jnp.where(eid == i2, w2 / tot, 0.0)
    base = best * EXPERTS_PER_GROUP
    rid = lax.broadcasted_iota(jnp.int32, (V7X_SUBLANES, tm), 0)
    packed = jnp.where(rid == EXPERTS_PER_GROUP, best.astype(F32), 0.0)
    for j in range(EXPERTS_PER_GROUP):
        lw = jnp.sum(jnp.where(eid == base + j, comb, 0.0), axis=0, keepdims=True)
        packed = jnp.where(rid == j, lw, packed)
    packed = jnp.concatenate([packed, jnp.zeros((V7X_LANES - V7X_SUBLANES, tm), F32)], axis=0)
    o_ref[...] = packed.T


def _router_call(x, mods_l, router_w, router_b, *, tm=512):
    t = T_ALL
    rw = jnp.pad(router_w.T, ((0, V7X_LANES - N_EXPERTS), (0, 0)))
    rb = jnp.pad(router_b.reshape(N_EXPERTS, 1), ((0, V7X_LANES - N_EXPERTS), (0, 0)))
    return pl.pallas_call(
        functools.partial(_router_kernel, tiles_per_seg=SEG_ROWS // tm),
        grid=(t // tm,),
        in_specs=[
            _slab_spec(tm, lambda i: (i, 0)),
            pl.BlockSpec((MOD_ROWS, D_MODEL), lambda i: (0, 3)),
            pl.BlockSpec((MOD_ROWS, D_MODEL), lambda i: (0, 4)),
            pl.BlockSpec((V7X_LANES, D_MODEL), lambda i: (0, 0)),
            pl.BlockSpec((V7X_LANES, 1), lambda i: (0, 0)),
        ],
        out_specs=pl.BlockSpec((tm, V7X_LANES), lambda i: (i, 0)),
        out_shape=jax.ShapeDtypeStruct((t, V7X_LANES), F32),
        compiler_params=_cparams("arbitrary"),
        name="moe_router",
    )(x, mods_l, mods_l, rw, rb)


MOE_TILE = 256
N_MOD_CLASSES = 1 + DEC_BATCH
N_BUCKETS = N_GROUPS * N_MOD_CLASSES
MOE_TILES = T_ALL // MOE_TILE + N_BUCKETS
MOE_SLOTS = MOE_TILES * MOE_TILE
MOE_TRASH_ROWS = 2 * MOE_TILE


def _moe_plan(routed):
    grp = routed[:, EXPERTS_PER_GROUP].astype(jnp.int32)
    tok = jnp.arange(T_ALL, dtype=jnp.int32)
    mod_row = jnp.maximum(tok // SEG_ROWS - (N_CTX_SEGS - 1), 0)
    bucket = grp * N_MOD_CLASSES + mod_row
    order = jnp.sort(bucket * T_ALL + tok) % T_ALL
    counts = jnp.sum((bucket[:, None] == jnp.arange(N_BUCKETS)[None, :]).astype(jnp.int32), axis=0)
    tiles = (counts + MOE_TILE - 1) // MOE_TILE
    tile_end = jnp.cumsum(tiles)
    slot_start = (tile_end - tiles) * MOE_TILE
    tok_start = jnp.cumsum(counts) - counts
    tile_bucket = jnp.minimum(jnp.sum((jnp.arange(MOE_TILES)[:, None] >= tile_end[None, :]).astype(jnp.int32), axis=1),
                              N_BUCKETS - 1)
    slot = jnp.arange(MOE_SLOTS, dtype=jnp.int32)
    slot_bucket = jnp.repeat(tile_bucket, MOE_TILE)
    padded = jnp.concatenate([jnp.zeros((MOE_SLOTS,), jnp.int32), order, jnp.zeros((MOE_SLOTS - T_ALL,), jnp.int32)])
    slot_tok = jnp.full((MOE_SLOTS,), -1, jnp.int32)
    for b in range(N_BUCKETS):
        shift = slot_start[b] - tok_start[b]
        cand = lax.dynamic_slice(padded, (MOE_SLOTS - shift,), (MOE_SLOTS,))
        mine = (slot_bucket == b) & (slot - slot_start[b] < counts[b])
        slot_tok = jnp.where(mine, cand, slot_tok)
    return slot_tok, tile_bucket // N_MOD_CLASSES, tile_bucket % N_MOD_CLASSES


def _moe_kernel(tok_ref, tgrp_ref, tmod_ref, x_hbm, routed_ref, sh_ref, sc_ref, gate_ref, wg_ref, wu_ref, wd_ref,
                g_ref, b_ref, out_hbm, xbuf, obuf, cw_scr, wgb, wub, wdb, gsem, ssem):
    t = pl.program_id(0)
    nt = pl.num_programs(0)
    tm = MOE_TILE
    rt = ROW_TILES
    cur = lax.rem(t, 2)
    nxt = 1 - cur

    def start_gather(tile, buf, rows=range(MOE_TILE)):
        for r in rows:
            src = pl.multiple_of(jnp.maximum(tok_ref[tile * tm + r], 0) * rt, rt)
            pltpu.make_async_copy(x_hbm.at[pl.ds(src, rt)], xbuf.at[buf, pl.ds(r * rt, rt)],
                                  gsem.at[buf]).start(priority=r % 2)

    def start_scatter(tile, buf, live, rows=range(MOE_TILE)):
        trash = T_ALL + buf * tm
        for r in rows:
            s = tok_ref[tile * tm + r]
            dst = pl.multiple_of(jnp.where((s >= 0) & live, s, trash + r) * rt, rt)
            pltpu.make_async_copy(obuf.at[buf, pl.ds(r * rt, rt)], out_hbm.at[pl.ds(dst, rt)],
                                  ssem.at[buf]).start(priority=r % 2)

    def wait_gather(buf):
        pltpu.make_async_copy(x_hbm.at[pl.ds(0, tm * rt)], xbuf.at[buf], gsem.at[buf]).wait()

    def wait_scatter(buf):
        pltpu.make_async_copy(obuf.at[buf], out_hbm.at[pl.ds(0, tm * rt)], ssem.at[buf]).wait()

    @pl.when(t == 0)
    def _():
        start_gather(0, 0)
        obuf[1] = jnp.zeros(obuf.shape[1:], F32)

    wait_gather(cur)

    @pl.when(t >= 1)
    def _():
        wait_scatter(cur)

    @pl.when((t == 0) | (tgrp_ref[t] != tgrp_ref[jnp.maximum(t - 1, 0)]))
    def _():
        wgb[...] = wg_ref[...].astype(BF16)
        wub[...] = wu_ref[...].astype(BF16)
        wdb[...] = wd_ref[...].astype(BF16)

    n_portions = EXPERTS_PER_GROUP + 2
    per = -(-tm // n_portions)

    def issue_portion(p):
        rows = range(p * per, min((p + 1) * per, tm))
        start_gather(jnp.minimum(t + 1, nt - 1), nxt, rows)
        start_scatter(jnp.maximum(t - 1, 0), nxt, t >= 1, rows)

    issue_portion(0)
    for r in range(tm):
        src = jnp.maximum(tok_ref[t * tm + r], 0)
        cw_scr[r:r + 1, :] = routed_ref[pl.ds(src, 1), :]

    issue_portion(1)
    x = _load_rows(xbuf.at[cur])
    row = tmod_ref[t]
    h = (x * (1.0 + sc_ref[pl.ds(row, 1), :]) + sh_ref[pl.ds(row, 1), :]).astype(BF16)
    cw = cw_scr[...]
    acc = None
    for j in range(EXPERTS_PER_GROUP):
        hid = _silu(jnp.dot(h, wgb[j], preferred_element_type=F32)) * jnp.dot(h, wub[j], preferred_element_type=F32)
        f = jnp.dot((hid * cw[:, j:j + 1]).astype(BF16), wdb[j], preferred_element_type=F32)
        acc = f if acc is None else acc + f
        issue_portion(2 + j)
    gate = gate_ref[pl.ds(row, 1), :]
    _store_rows(obuf.at[cur], _layer_norm(ALPHA * x + (1.0 + gate) * acc, g_ref[...], b_ref[...]))

    @pl.when(t == nt - 1)
    def _():
        start_scatter(t, cur, True)
        wait_gather(nxt)
        wait_scatter(nxt)
        wait_scatter(cur)


def _moe_call(x, mods_l, routed, w_gate, w_up, w_down, g, b):
    slot_tok, tile_grp, tile_mod = _moe_plan(routed)
    tm = MOE_TILE
    gw = EXPERTS_PER_GROUP

    def const(shape):
        return pl.BlockSpec(shape, lambda t, *_: tuple(0 for _ in shape))

    def mod(idx):
        return pl.BlockSpec((MOD_ROWS, D_MODEL), lambda t, *_: (0, idx))

    grid_spec = pltpu.PrefetchScalarGridSpec(
        num_scalar_prefetch=3,
        grid=(MOE_TILES,),
        in_specs=[
            pl.BlockSpec(memory_space=pl.ANY),
            pl.BlockSpec((T_ALL, V7X_LANES), lambda t, *_: (0, 0), pipeline_mode=pl.Buffered(1)),
            mod(3), mod(4), mod(5),
            pl.BlockSpec((gw, D_MODEL, D_EXPERT), lambda t, tok, grp, row: (grp[t], 0, 0)),
            pl.BlockSpec((gw, D_MODEL, D_EXPERT), lambda t, tok, grp, row: (grp[t], 0, 0)),
            pl.BlockSpec((gw, D_EXPERT, D_MODEL), lambda t, tok, grp, row: (grp[t], 0, 0)),
            const((1, D_MODEL)), const((1, D_MODEL)),
        ],
        out_specs=pl.BlockSpec(memory_space=pl.ANY),
        scratch_shapes=[
            pltpu.VMEM((2, tm * ROW_TILES, V7X_LANES), F32),
            pltpu.VMEM((2, tm * ROW_TILES, V7X_LANES), F32),
            pltpu.VMEM((tm, V7X_LANES), F32),
            pltpu.VMEM((gw, D_MODEL, D_EXPERT), BF16),
            pltpu.VMEM((gw, D_MODEL, D_EXPERT), BF16),
            pltpu.VMEM((gw, D_EXPERT, D_MODEL), BF16),
            pltpu.SemaphoreType.DMA((2,)),
            pltpu.SemaphoreType.DMA((2,)),
        ],
    )
    return pl.pallas_call(
        _moe_kernel,
        grid_spec=grid_spec,
        out_shape=jax.ShapeDtypeStruct(((T_ALL + MOE_TRASH_ROWS) * ROW_TILES, V7X_LANES), F32),
        compiler_params=_cparams("arbitrary"),
        name="moe_experts",
    )(slot_tok, tile_grp, tile_mod, x, routed, mods_l, mods_l, mods_l, w_gate, w_up, w_down,
      g.reshape(1, D_MODEL), b.reshape(1, D_MODEL))


def kernel(x_prompt, x_sample, state_gdn, cache_gqa_k, cache_gqa_v, cache_mla_ckv, cache_mla_krope, c, c_ctx, w_ada, b_ada, ln_mix_g, ln_mix_b, ln_ffn_g, ln_ffn_b, gdn_w_in, gdn_conv, gdn_a_log, gdn_dt_bias, gdn_norm_g, gdn_w_out, gqa_w_in, gqa_q_norm, gqa_k_norm, gqa_w_out, mla_w_in, mla_q_norm, mla_kv_norm, mla_w_q_up, mla_w_kv_up, mla_w_out, router_w, router_b, moe_w_gate, moe_w_up, moe_w_down):
    x = jnp.concatenate([x_prompt.reshape(T_CTX * ROW_TILES, V7X_LANES),
                         x_sample.reshape(T_LAT * ROW_TILES, V7X_LANES)], axis=0)
    cond = jnp.concatenate([c_ctx[None], c, jnp.zeros((MOD_ROWS - 1 - DEC_BATCH, D_MODEL), F32)], axis=0)
    mods = _ada_call(cond, w_ada, b_ada)

    st_gdn, st_k, st_v, st_ckv, st_krope = [], [], [], [], []
    for l in range(DEPTH):
        kind, li = l % N_MIXERS, l // N_MIXERS
        mods_l = mods[l]
        if kind == 0:
            o, st = _gdn_layer(x, mods_l, li, state_gdn, gdn_w_in, gdn_conv, gdn_a_log, gdn_dt_bias, gdn_norm_g)
            st_gdn.append(st)
            w_out = gdn_w_out[li]
        elif kind == 1:
            o, nk, nv = _gqa_layer(x, mods_l, li, cache_gqa_k, cache_gqa_v, gqa_w_in, gqa_q_norm, gqa_k_norm)
            st_k.append(nk)
            st_v.append(nv)
            w_out = gqa_w_out[li]
        else:
            o, nckv, nkr = _mla_layer(x, mods_l, li, cache_mla_ckv, cache_mla_krope, mla_w_in, mla_q_norm,
                                      mla_kv_norm, mla_w_q_up, mla_w_kv_up)
            st_ckv.append(nckv)
            st_krope.append(nkr)
            w_out = mla_w_out[li]
        x = _outln_call(o[0], o[1], w_out, x, mods_l, 2, ln_mix_g[l], ln_mix_b[l], name="mix_out_ln")
        routed = _router_call(x, mods_l, router_w, router_b)
        x = _moe_call(x, mods_l, routed, moe_w_gate[l], moe_w_up[l], moe_w_down[l], ln_ffn_g[l], ln_ffn_b[l])

    y_prompt = x[:T_CTX * ROW_TILES].reshape(BATCH, SEQ, D_MODEL)
    y_sample = x[T_CTX * ROW_TILES:T_ALL * ROW_TILES].reshape(DEC_BATCH, DEC_SEQ, D_MODEL)
    return (y_prompt, y_sample, jnp.stack(st_gdn, axis=1), jnp.stack(st_k, axis=1), jnp.stack(st_v, axis=1),
            jnp.stack(st_ckv, axis=1), jnp.stack(st_krope, axis=1))
```

```python
import functools
import math

import jax
import jax.numpy as jnp
from jax import lax
from jax.experimental import pallas as pl
from jax.experimental.pallas import tpu as pltpu

F32 = jnp.float32
BF16 = jnp.bfloat16

D_MODEL = 1024
BATCH = 32
SEQ = 256
DEPTH = 4
DEC_BATCH = 2
DEC_SEQ = 2048
PAST_LEN = 512
GRID_W = 64
N_MIXERS = 3

GDN_HEADS = 8
GDN_DK = 128
GDN_DV = 128
GDN_CONV = 5
GDN_CHUNK = 64
GDN_QKV = GDN_HEADS * (2 * GDN_DK + GDN_DV)
GDN_QKVZ = GDN_QKV + GDN_HEADS * GDN_DV

GQA_HEADS = 8
GQA_KV_HEADS = 2
GQA_GROUP = GQA_HEADS // GQA_KV_HEADS
GQA_HEAD_DIM = 128
GQA_IN = (GQA_HEADS + 2 * GQA_KV_HEADS) * GQA_HEAD_DIM

MLA_HEADS = 8
MLA_Q_LORA = 512
MLA_KV_LORA = 256
MLA_NOPE = 128
MLA_ROPE = 64
MLA_V = 128
MLA_IN = MLA_Q_LORA + MLA_KV_LORA + MLA_ROPE

N_EXPERTS = 16
N_GROUPS = 4
EXPERTS_PER_GROUP = N_EXPERTS // N_GROUPS
D_EXPERT = 256

ROPE_THETA = 10000.0
ALPHA = (2 * DEPTH) ** 0.25
NORM_EPS = 1e-6

T_CTX = BATCH * SEQ
T_LAT = DEC_BATCH * DEC_SEQ
T_ALL = T_CTX + T_LAT
SEG_ROWS = DEC_SEQ
N_CTX_SEGS = T_CTX // SEG_ROWS
MOD_ROWS = 8

V7X_LANES = 128
V7X_SUBLANES = 8
V7X_VMEM_BYTES = 64 * 1024 * 1024
VMEM_LIMIT = V7X_VMEM_BYTES * 7 // 8

GDN_SUPER = 256
CHUNKS_PER_SUPER = GDN_SUPER // GDN_CHUNK


def _cparams(*sem):
    return pltpu.CompilerParams(dimension_semantics=sem, vmem_limit_bytes=VMEM_LIMIT)


def _mm(a, b):
    return jnp.dot(a.astype(BF16), b.astype(BF16), preferred_element_type=F32)


def _mm_nt(a, b):
    return lax.dot_general(a.astype(BF16), b.astype(BF16), (((1,), (1,)), ((), ())),
                           preferred_element_type=F32)


def _silu(x):
    return x * (1.0 / (1.0 + jnp.exp(-x)))


def _sigmoid(x):
    return 1.0 / (1.0 + jnp.exp(-x))


def _chunk_of(idx):
    return idx >> int(math.log2(GDN_CHUNK))


def _seg_row(i, tiles_per_seg):
    return jnp.maximum(lax.div(i, tiles_per_seg) - (N_CTX_SEGS - 1), 0)


def _layer_norm(y, g, b):
    mu = jnp.mean(y, axis=-1, keepdims=True)
    yc = y - mu
    var = jnp.mean(yc * yc, axis=-1, keepdims=True)
    return yc * lax.rsqrt(var + NORM_EPS) * g + b


def _rms(x, g):
    return x * lax.rsqrt(jnp.mean(x * x, axis=-1, keepdims=True) + NORM_EPS) * g


def _ada_kernel(c_ref, w_ref, b_ref, o_ref):
    o_ref[...] = _mm(_silu(c_ref[...]), w_ref[...]) + b_ref[...]


def _ada_call(cond, w_ada, b_ada):
    tn = 1536
    n = w_ada.shape[-1]
    return pl.pallas_call(
        _ada_kernel,
        grid=(DEPTH, n // tn),
        in_specs=[
            pl.BlockSpec((MOD_ROWS, D_MODEL), lambda l, j: (0, 0)),
            pl.BlockSpec((None, D_MODEL, tn), lambda l, j: (l, 0, j)),
            pl.BlockSpec((None, 1, tn), lambda l, j: (l, 0, j)),
        ],
        out_specs=pl.BlockSpec((None, MOD_ROWS, tn), lambda l, j: (l, 0, j)),
        out_shape=jax.ShapeDtypeStruct((DEPTH, MOD_ROWS, n), F32),
        compiler_params=_cparams("arbitrary", "arbitrary"),
        name="ada_mod",
    )(cond, w_ada, b_ada.reshape(DEPTH, 1, n))


def _modproj_kernel(x_ref, sh_ref, sc_ref, w_ref, *rest, tiles_per_seg, has_extra):
    if has_extra:
        wx_ref, o_ref, ox_ref, h_scr, wb_scr = rest
    else:
        o_ref, h_scr, wb_scr = rest
    i, j = pl.program_id(0), pl.program_id(1)

    @pl.when(i == 0)
    def _():
        wb_scr[j] = w_ref[...].astype(BF16)

    @pl.when(j == 0)
    def _():
        r = _seg_row(i, tiles_per_seg)
        sh = sh_ref[pl.ds(r, 1), :]
        sc = sc_ref[pl.ds(r, 1), :]
        h = (x_ref[...] * (1.0 + sc) + sh).astype(BF16)
        h_scr[...] = h
        if has_extra:
            ox_ref[...] = _mm(h, wx_ref[...])

    o_ref[...] = jnp.dot(h_scr[...], wb_scr[j], preferred_element_type=F32).astype(o_ref.dtype)


def _modproj_call(x, mods_l, shift_idx, scale_idx, w, n_out, w_extra=None, *, tm=1024, tn=512, out_dtype=F32, name):
    t = T_ALL
    has_extra = w_extra is not None
    nj = n_out // tn
    in_specs = [
        pl.BlockSpec((tm, D_MODEL), lambda i, j: (i, 0)),
        pl.BlockSpec((MOD_ROWS, D_MODEL), lambda i, j: (0, shift_idx)),
        pl.BlockSpec((MOD_ROWS, D_MODEL), lambda i, j: (0, scale_idx)),
        pl.BlockSpec((D_MODEL, tn), lambda i, j: (0, jnp.where(i == 0, j, nj - 1))),
    ]
    out_specs = [pl.BlockSpec((tm, tn), lambda i, j: (i, j))]
    out_shape = [jax.ShapeDtypeStruct((t, n_out), out_dtype)]
    args = [x, mods_l, mods_l, w]
    if has_extra:
        nx = w_extra.shape[1]
        in_specs.append(pl.BlockSpec((D_MODEL, nx), lambda i, j: (0, 0)))
        out_specs.append(pl.BlockSpec((tm, nx), lambda i, j: (i, 0)))
        out_shape.append(jax.ShapeDtypeStruct((t, nx), F32))
        args.append(w_extra)
    res = pl.pallas_call(
        functools.partial(_modproj_kernel, tiles_per_seg=SEG_ROWS // tm, has_extra=has_extra),
        grid=(t // tm, nj),
        in_specs=in_specs,
        out_specs=out_specs,
        out_shape=out_shape,
        scratch_shapes=[pltpu.VMEM((tm, D_MODEL), BF16), pltpu.VMEM((nj, D_MODEL, tn), BF16)],
        compiler_params=_cparams("arbitrary", "arbitrary"),
        name=name,
    )(*args)
    return res if has_extra else res[0]


def _rmsproj_kernel(x_ref, g_ref, w_ref, *rest, normalize, emit_xn):
    if emit_xn:
        o_ref, xn_ref, h_scr = rest
    else:
        o_ref, h_scr = rest
    j = pl.program_id(1)

    @pl.when(j == 0)
    def _():
        x = x_ref[...]
        xn = _rms(x, g_ref[...]) if normalize else x
        h_scr[...] = xn.astype(BF16)
        if emit_xn:
            xn_ref[...] = xn

    o_ref[...] = _mm(h_scr[...], w_ref[...])


def _rmsproj_call(x, col_block, k, g, w, *, normalize=True, emit_xn=False, tm=1024, tn=512, name):
    t = x.shape[0]
    n_out = w.shape[1]
    out_specs = [pl.BlockSpec((tm, tn), lambda i, j: (i, j))]
    out_shape = [jax.ShapeDtypeStruct((t, n_out), F32)]
    if emit_xn:
        out_specs.append(pl.BlockSpec((tm, k), lambda i, j: (i, 0)))
        out_shape.append(jax.ShapeDtypeStruct((t, k), F32))
    res = pl.pallas_call(
        functools.partial(_rmsproj_kernel, normalize=normalize, emit_xn=emit_xn),
        grid=(t // tm, n_out // tn),
        in_specs=[
            pl.BlockSpec((tm, k), lambda i, j: (i, col_block)),
            pl.BlockSpec((1, k), lambda i, j: (0, 0)),
            pl.BlockSpec((k, tn), lambda i, j: (0, j)),
        ],
        out_specs=out_specs,
        out_shape=out_shape,
        scratch_shapes=[pltpu.VMEM((tm, k), BF16)],
        compiler_params=_cparams("arbitrary", "arbitrary"),
        name=name,
    )(x, g.reshape(1, k), w)
    return res if emit_xn else res[0]


def _outln_kernel(oc_ref, ol_ref, w_ref, x_ref, gate_ref, g_ref, b_ref, out_ref, wb_scr, *, tiles_per_seg, ctx_tiles):
    i = pl.program_id(0)

    @pl.when(i == 0)
    def _():
        wb_scr[...] = w_ref[...].astype(BF16)

    gate = gate_ref[pl.ds(_seg_row(i, tiles_per_seg), 1), :]
    o = jnp.where(i < ctx_tiles, oc_ref[...], ol_ref[...])
    f = jnp.dot(o.astype(BF16), wb_scr[...], preferred_element_type=F32)
    out_ref[...] = _layer_norm(ALPHA * x_ref[...] + (1.0 + gate) * f, g_ref[...], b_ref[...])


def _outln_call(o_ctx, o_lat, w, x, mods_l, gate_idx, g, b, *, tm=512, name):
    k = o_ctx.shape[1]
    t = T_ALL
    ctx_tiles = T_CTX // tm
    return pl.pallas_call(
        functools.partial(_outln_kernel, tiles_per_seg=SEG_ROWS // tm, ctx_tiles=ctx_tiles),
        grid=(t // tm,),
        in_specs=[
            pl.BlockSpec((tm, k), lambda i: (jnp.minimum(i, ctx_tiles - 1), 0)),
            pl.BlockSpec((tm, k), lambda i: (jnp.maximum(i - ctx_tiles, 0), 0)),
            pl.BlockSpec((k, D_MODEL), lambda i: (0, 0)),
            pl.BlockSpec((tm, D_MODEL), lambda i: (i, 0)),
            pl.BlockSpec((MOD_ROWS, D_MODEL), lambda i: (0, gate_idx)),
            pl.BlockSpec((1, D_MODEL), lambda i: (0, 0)),
            pl.BlockSpec((1, D_MODEL), lambda i: (0, 0)),
        ],
        out_specs=pl.BlockSpec((tm, D_MODEL), lambda i: (i, 0)),
        out_shape=jax.ShapeDtypeStruct((t, D_MODEL), F32),
        scratch_shapes=[pltpu.VMEM((k, D_MODEL), BF16)],
        compiler_params=_cparams("arbitrary"),
        name=name,
    )(o_ctx, o_lat, w, x, mods_l, g.reshape(1, D_MODEL), b.reshape(1, D_MODEL))


def _gdn_gates_kernel(ab_ref, alog_ref, dt_ref, o_ref):
    ab = ab_ref[...]
    tm = ab.shape[0]
    x = ab + dt_ref[...]
    softplus = jnp.maximum(x, 0.0) + jnp.log(1.0 + jnp.exp(-jnp.abs(x)))
    g = -jnp.exp(alog_ref[...]) * softplus
    ri = lax.broadcasted_iota(jnp.int32, (tm, tm), 0)
    ci = lax.broadcasted_iota(jnp.int32, (tm, tm), 1)
    same = _chunk_of(ri) == _chunk_of(ci)
    lower = jnp.where(same & (ri >= ci), 1.0, 0.0).astype(F32)
    upper = jnp.where(same & (ri <= ci), 1.0, 0.0).astype(F32)
    pre = jnp.dot(lower, g, precision=lax.Precision.HIGHEST, preferred_element_type=F32)
    suf = jnp.dot(upper, g, precision=lax.Precision.HIGHEST, preferred_element_type=F32)
    lane = lax.broadcasted_iota(jnp.int32, ab.shape, 1)
    o_ref[...] = jnp.where(lane < GDN_HEADS, pre, jnp.where(lane < 2 * GDN_HEADS, suf, _sigmoid(ab)))


def _gdn_gates_call(ab, a_log, dt_bias):
    t, n = ab.shape
    tm = GDN_SUPER
    pad = n - 2 * GDN_HEADS
    alog_row = jnp.pad(a_log.reshape(1, -1), ((0, 0), (0, pad)))
    dt_row = jnp.pad(dt_bias.reshape(1, -1), ((0, 0), (0, pad)))
    return pl.pallas_call(
        _gdn_gates_kernel,
        grid=(t // tm,),
        in_specs=[
            pl.BlockSpec((tm, n), lambda i: (i, 0)),
            pl.BlockSpec((1, n), lambda i: (0, 0)),
            pl.BlockSpec((1, n), lambda i: (0, 0)),
        ],
        out_specs=pl.BlockSpec((tm, n), lambda i: (i, 0)),
        out_shape=jax.ShapeDtypeStruct((t, n), F32),
        compiler_params=_cparams("arbitrary"),
        name="gdn_gates",
    )(ab, alog_row, dt_row)


def _loop(n, body, max_static=4):
    if n <= max_static:
        for i in range(n):
            body(i)
    else:
        def wrapped(i, carry):
            body(i)
            return carry
        lax.fori_loop(0, n, wrapped, 0)


def _gdn_core_kernel(q_ref, k_ref, v_ref, z_ref, cq_ref, ck_ref, cv_ref, gc_ref, gr_ref, ng_ref, *rest,
                     n_tok, has_s0, hps):
    if has_s0:
        s0_ref, o_ref, st_ref = rest[:3]
        scr = rest[3:]
    else:
        o_ref, st_ref = rest[:2]
        scr = rest[2:]
    pad_scr, kn_scr, u_scr, w_scr, qe_scr, qk_scr, oo_scr, s_scr = scr
    n_super = n_tok // GDN_SUPER
    n_chunk = n_tok // GDN_CHUNK
    halo = V7X_SUBLANES
    sb = GDN_SUPER
    ck = GDN_CHUNK
    dk = GDN_DK

    zeros_halo = jnp.zeros((halo, dk), F32)
    for hd in range(hps):
        for a, src in enumerate((q_ref, k_ref, v_ref)):
            pad_scr[3 * hd + a, 0:halo, :] = zeros_halo
            pad_scr[3 * hd + a, halo:halo + n_tok, :] = src[:, hd * dk:(hd + 1) * dk].astype(F32)
            pad_scr[3 * hd + a, halo + n_tok:2 * halo + n_tok, :] = zeros_halo
        for d in range(2):
            s_scr[2 * hd + d] = s0_ref[d, hd] if has_s0 else jnp.zeros((dk, GDN_DV), F32)

    ri = lax.broadcasted_iota(jnp.int32, (sb, sb), 0)
    ci = lax.broadcasted_iota(jnp.int32, (sb, sb), 1)
    same_chunk = _chunk_of(ri) == _chunk_of(ci)
    eye = jnp.where(ri == ci, 1.0, 0.0).astype(F32)
    n_levels = int(math.log2(ck))

    def level_mask(l):
        return ((ri >> (l + 1)) == (ci >> (l + 1))) & ((ri >> l) != (ci >> l))

    def conv_silu(a, w_ref, hd, r0):
        xp = pad_scr[a, pl.ds(r0, sb + 2 * halo), :]
        acc = None
        for j in range(GDN_CONV):
            shift = (GDN_CONV // 2 - j) % (sb + 2 * halo)
            tap = xp if shift == 0 else pltpu.roll(xp, shift, 0)
            term = w_ref[j:j + 1, hd * dk:(hd + 1) * dk] * tap[halo:halo + sb, :]
            acc = term if acc is None else acc + term
        return _silu(acc)

    chains = [(hd, d) for hd in range(hps) for d in range(2)]

    a_unroll = max(1, min(n_super, 8 // (2 * hps)))

    def phase_a(it):
        units = [(it * a_unroll + u, hd) for u in range(a_unroll) for hd in range(hps)]
        r0s = [pl.multiple_of(s * sb, sb) for s, _ in units]
        qs, ks, vs, ps, qks = [], [], [], [], []
        for (s, hd), r0 in zip(units, r0s):
            q = conv_silu(3 * hd, cq_ref, hd, r0)
            k = conv_silu(3 * hd + 1, ck_ref, hd, r0)
            v = conv_silu(3 * hd + 2, cv_ref, hd, r0)
            q = q * lax.rsqrt(jnp.sum(q * q, axis=-1, keepdims=True) + NORM_EPS) * (dk ** -0.5)
            k = k * lax.rsqrt(jnp.sum(k * k, axis=-1, keepdims=True) + NORM_EPS)
            kn_scr[hd, pl.ds(r0, sb), :] = k
            kb = k.astype(BF16)
            qs.append(q)
            ks.append(k)
            vs.append(v)
            ps.append(_mm_nt(kb, kb))
            qks.append(_mm_nt(q, kb))
        a_chains = [(n, d) for n in range(len(units)) for d in range(2)]
        a_mats, ts, rhs = [], [], []
        for n, d in a_chains:
            (s, hd), r0 = units[n], r0s[n]
            hdd = 2 * hd + d
            gcol = gc_ref[hd, pl.ds(r0, sb), d:d + 1]
            bcol = gc_ref[hd, pl.ds(r0, sb), 2 + d:3 + d]
            grow = gr_ref[hd, s, d:d + 1, :]
            if d == 0:
                mask, strict = same_chunk & (ri >= ci), same_chunk & (ri > ci)
            else:
                mask, strict = same_chunk & (ri <= ci), same_chunk & (ri < ci)
            dec = jnp.exp(jnp.where(mask, gcol - grow, -jnp.inf))
            a = jnp.where(strict, ps[n] * bcol * dec, 0.0)
            a_mats.append(a)
            ts.append(eye - jnp.where(level_mask(0), a, 0.0))
            eg = jnp.exp(gcol)
            rhs.append(jnp.concatenate([vs[n] * bcol, ks[n] * (bcol * eg)], axis=1).astype(BF16))
            qe_scr[hdd, pl.ds(r0, sb), :] = qs[n] * eg
            qkd = qks[n] * dec
            for c in range(CHUNKS_PER_SUPER):
                qk_scr[hdd, pl.ds(r0 + c * ck, ck), :] = qkd[c * ck:(c + 1) * ck, c * ck:(c + 1) * ck]
        for l in range(1, n_levels):
            ys = [_mm(jnp.where(level_mask(l), a_mats[i], 0.0), ts[i]) for i in range(len(a_chains))]
            ts = [ts[i] - _mm(ts[i], ys[i]) for i in range(len(a_chains))]
        for i, (n, d) in enumerate(a_chains):
            (s, hd), r0 = units[n], r0s[n]
            uw = _mm(ts[i], rhs[i])
            u_scr[2 * hd + d, pl.ds(r0, sb), :] = uw[:, :GDN_DV]
            w_scr[2 * hd + d, pl.ds(r0, sb), :] = uw[:, GDN_DV:]

    _loop(n_super // a_unroll, phase_a, max_static=1)

    def phase_b(i):
        rows, ws_qs, v_new, kws, gls = [], [], [], [], []
        for hd, d in chains:
            hdd = 2 * hd + d
            c = i if d == 0 else n_chunk - 1 - i
            r = pl.multiple_of(c * ck, ck)
            r_last = r + (ck - 1) if d == 0 else r
            rows.append(r)
            gcol = gc_ref[hd, pl.ds(r, ck), d:d + 1]
            gl = gc_ref[hd, pl.ds(r_last, 1), d:d + 1]
            gls.append(gl)
            kws.append((kn_scr[hd, pl.ds(r, ck), :] * jnp.exp(gl - gcol)).T)
            wq = jnp.concatenate([w_scr[hdd, pl.ds(r, ck), :], qe_scr[hdd, pl.ds(r, ck), :]], axis=0)
            ws_qs.append(_mm(wq, s_scr[hdd]))
        for j, (hd, d) in enumerate(chains):
            hdd = 2 * hd + d
            vn = u_scr[hdd, pl.ds(rows[j], ck), :] - ws_qs[j][:ck]
            v_new.append(vn)
            oo_scr[hdd, pl.ds(rows[j], ck), :] = ws_qs[j][ck:] + _mm(qk_scr[hdd, pl.ds(rows[j], ck), :], vn)
        for j, (hd, d) in enumerate(chains):
            hdd = 2 * hd + d
            s_scr[hdd] = s_scr[hdd] * jnp.exp(gls[j]) + _mm(kws[j], v_new[j])

    _loop(n_chunk, phase_b)

    def finish(s):
        r0 = pl.multiple_of(s * sb, sb)
        for hd in range(hps):
            o = oo_scr[2 * hd, pl.ds(r0, sb), :] + oo_scr[2 * hd + 1, pl.ds(r0, sb), :]
            z = z_ref[pl.ds(r0, sb), hd * GDN_DV:(hd + 1) * GDN_DV].astype(F32)
            o_ref[pl.ds(r0, sb), hd * GDN_DV:(hd + 1) * GDN_DV] = _rms(o, ng_ref[...]) * _silu(z)

    _loop(n_super, finish)
    for hd in range(hps):
        for d in range(2):
            st_ref[d, hd] = s_scr[2 * hd + d]


def _gdn_core_call(qkvz, conv_w, gc, gr, norm_g, s0, *, batch, n_tok, row_block_off, hps, name):
    has_s0 = s0 is not None
    h = GDN_HEADS
    dk = GDN_DK
    n_super = n_tok // GDN_SUPER
    hb = h // hps

    mode = dict(pipeline_mode=pl.Buffered(1)) if n_tok > GDN_SUPER else {}

    def col(sec):
        return pl.BlockSpec((n_tok, dk * hps), lambda b, hh: (b + row_block_off, sec * hb + hh), **mode)

    def cw(sec):
        return pl.BlockSpec((GDN_CONV, dk * hps), lambda b, hh: (0, sec * hb + hh))

    in_specs = [col(0), col(1), col(2), col(3), cw(0), cw(1), cw(2),
                pl.BlockSpec((None, hps, n_tok, 4), lambda b, hh: (b, hh, 0, 0)),
                pl.BlockSpec((None, hps, n_super, 4, GDN_SUPER), lambda b, hh: (b, hh, 0, 0, 0)),
                pl.BlockSpec((1, GDN_DV), lambda b, hh: (0, 0))]
    args = [qkvz, qkvz, qkvz, qkvz, conv_w, conv_w, conv_w, gc, gr, norm_g.reshape(1, GDN_DV)]
    st_spec = pl.BlockSpec((None, 2, hps, dk, GDN_DV), lambda b, hh: (b, 0, hh, 0, 0))
    if has_s0:
        in_specs.append(st_spec)
        args.append(s0)
    return pl.pallas_call(
        functools.partial(_gdn_core_kernel, n_tok=n_tok, has_s0=has_s0, hps=hps),
        grid=(batch, hb),
        in_specs=in_specs,
        out_specs=[pl.BlockSpec((n_tok, GDN_DV * hps), lambda b, hh: (b, hh)), st_spec],
        out_shape=[jax.ShapeDtypeStruct((batch * n_tok, h * GDN_DV), F32),
                   jax.ShapeDtypeStruct((batch, 2, h, dk, GDN_DV), F32)],
        scratch_shapes=[
            pltpu.VMEM((3 * hps, n_tok + 2 * V7X_SUBLANES, dk), F32),
            pltpu.VMEM((hps, n_tok, dk), F32),
            pltpu.VMEM((2 * hps, n_tok, GDN_DV), F32),
            pltpu.VMEM((2 * hps, n_tok, dk), F32),
            pltpu.VMEM((2 * hps, n_tok, dk), F32),
            pltpu.VMEM((2 * hps, n_tok, GDN_CHUNK), F32),
            pltpu.VMEM((2 * hps, n_tok, GDN_DV), F32),
            pltpu.VMEM((2 * hps, dk, GDN_DV), F32),
        ],
        compiler_params=_cparams("arbitrary", "arbitrary"),
        name=name,
    )(*args)


def _gdn_layer(x, mods_l, li, state_gdn, gdn_w_in, gdn_conv, gdn_a_log, gdn_dt_bias, gdn_norm_g):
    w_in = gdn_w_in[li]
    w_ab = jnp.pad(w_in[:, GDN_QKVZ:], ((0, 0), (0, V7X_LANES - 4 * GDN_HEADS)))
    qkvz, ab = _modproj_call(x, mods_l, 0, 1, w_in, GDN_QKVZ, w_ab, out_dtype=BF16, name="gdn_in")
    gates = _gdn_gates_call(ab, gdn_a_log[li], gdn_dt_bias[li])[:, :4 * GDN_HEADS]

    def head_layouts(g, batch, n_tok):
        g = g.reshape(batch, n_tok, 4, GDN_HEADS)
        gc = jnp.transpose(g, (0, 3, 1, 2))
        gr = jnp.transpose(g, (0, 3, 2, 1)).reshape(batch, GDN_HEADS, 4, n_tok // GDN_SUPER, GDN_SUPER)
        return gc, jnp.transpose(gr, (0, 1, 3, 2, 4))

    gc_p, gr_p = head_layouts(gates[:T_CTX], BATCH, SEQ)
    gc_s, gr_s = head_layouts(gates[T_CTX:], DEC_BATCH, DEC_SEQ)
    o_p, st_p = _gdn_core_call(qkvz, gdn_conv[li], gc_p, gr_p, gdn_norm_g[li], None,
                               batch=BATCH, n_tok=SEQ, row_block_off=0, hps=4, name="gdn_core_ctx")
    o_s, _ = _gdn_core_call(qkvz, gdn_conv[li], gc_s, gr_s, gdn_norm_g[li], state_gdn[:, li],
                            batch=DEC_BATCH, n_tok=DEC_SEQ, row_block_off=T_CTX // DEC_SEQ, hps=2,
                            name="gdn_core_lat")
    return (o_p, o_s), st_p


def _rope_tables(n_tokens, rot_dim, reps):
    n_rows = n_tokens // GRID_W
    row = jnp.repeat(jnp.arange(n_rows, dtype=F32), GRID_W)
    colp = jnp.tile(jnp.arange(GRID_W, dtype=F32), n_rows)
    n_freq = rot_dim // 4
    inv_freq = ROPE_THETA ** (-jnp.arange(n_freq, dtype=F32) / n_freq)
    ra, ca = row[:, None] * inv_freq, colp[:, None] * inv_freq
    cos = jnp.concatenate([jnp.cos(ra), jnp.cos(ra), jnp.cos(ca), jnp.cos(ca)], axis=-1)
    sin = jnp.concatenate([-jnp.sin(ra), jnp.sin(ra), -jnp.sin(ca), jnp.sin(ca)], axis=-1)
    return jnp.tile(cos, (1, reps)), jnp.tile(sin, (1, reps))


def _rope(x, cos, sin, quarter):
    n = x.shape[-1]
    lane = lax.broadcasted_iota(jnp.int32, x.shape, x.ndim - 1)
    first = (lane & (2 * quarter - 1)) < quarter
    partner = jnp.where(first, pltpu.roll(x, n - quarter, x.ndim - 1), pltpu.roll(x, quarter, x.ndim - 1))
    return x * cos + partner * sin


def _gqa_prep_kernel(kv_ref, g_ref, *rest, rope):
    if rope:
        cos_ref, sin_ref, k_ref, v_ref = rest
    else:
        k_ref, v_ref = rest
    kv = kv_ref[...]
    nk = GQA_KV_HEADS * GQA_HEAD_DIM
    for hh in range(GQA_KV_HEADS):
        k = _rms(kv[:, hh * GQA_HEAD_DIM:(hh + 1) * GQA_HEAD_DIM], g_ref[...])
        if rope:
            k = _rope(k, cos_ref[...], sin_ref[...], GQA_HEAD_DIM // 4)
        k_ref[:, hh * GQA_HEAD_DIM:(hh + 1) * GQA_HEAD_DIM] = k
    v_ref[...] = kv[:, nk:]


def _gqa_prep_call(proj, k_norm, row_off, n_rows, tables, *, tm=512, name):
    rope = tables is not None
    nkv = 2 * GQA_KV_HEADS * GQA_HEAD_DIM
    kv_col = GQA_HEADS * GQA_HEAD_DIM // nkv
    off = row_off // tm
    in_specs = [pl.BlockSpec((tm, nkv), lambda i: (i + off, kv_col)),
                pl.BlockSpec((1, GQA_HEAD_DIM), lambda i: (0, 0))]
    args = [proj, k_norm.reshape(1, GQA_HEAD_DIM)]
    if rope:
        per_seq = DEC_SEQ // tm
        in_specs += [pl.BlockSpec((tm, GQA_HEAD_DIM), lambda i: (i % per_seq, 0))] * 2
        args += list(tables)
    half = nkv // 2
    return pl.pallas_call(
        functools.partial(_gqa_prep_kernel, rope=rope),
        grid=(n_rows // tm,),
        in_specs=in_specs,
        out_specs=[pl.BlockSpec((tm, half), lambda i: (i, 0))] * 2,
        out_shape=[jax.ShapeDtypeStruct((n_rows, half), F32)] * 2,
        compiler_params=_cparams("arbitrary"),
        name=name,
    )(*args)


def _attend(chains, scale):
    scores = [[_mm_nt(q, k) * scale for k in ks] for q, ks, _ in chains]
    probs, dens = [], []
    for sc in scores:
        m = None
        for s in sc:
            sm = jnp.max(s, axis=-1, keepdims=True)
            m = sm if m is None else jnp.maximum(m, sm)
        ps = [jnp.exp(s - m) for s in sc]
        den = None
        for p in ps:
            r = jnp.sum(p, axis=-1, keepdims=True)
            den = r if den is None else den + r
        probs.append([p.astype(BF16) for p in ps])
        dens.append(den)
    outs = []
    for (_, _, vs), ps, den in zip(chains, probs, dens):
        acc = None
        for p, v in zip(ps, vs):
            pv = _mm(p, v)
            acc = pv if acc is None else acc + pv
        outs.append(acc / den)
    return outs


def _gqa_attn_kernel(q_ref, qg_ref, *rest, n_seg, rope, kv_per_step):
    if rope:
        cos_ref, sin_ref = rest[:2]
        rest = rest[2:]
    seg_refs = rest[:2 * n_seg]
    o_ref = rest[2 * n_seg]
    dh = GQA_HEAD_DIM
    tq = q_ref.shape[0]
    heads_per_chain = GQA_GROUP // 2
    chains, heads = [], []
    for kh in range(kv_per_step):
        ks = [seg_refs[2 * sgi][:, kh * dh:(kh + 1) * dh].astype(BF16) for sgi in range(n_seg)]
        vs = [seg_refs[2 * sgi + 1][:, kh * dh:(kh + 1) * dh].astype(BF16) for sgi in range(n_seg)]
        for g0 in range(0, GQA_GROUP, heads_per_chain):
            qs = []
            for g in range(g0, g0 + heads_per_chain):
                c0 = (kh * GQA_GROUP + g) * dh
                qh = _rms(q_ref[:, c0:c0 + dh], qg_ref[...])
                if rope:
                    qh = _rope(qh, cos_ref[...], sin_ref[...], dh // 4)
                qs.append(qh.astype(BF16))
            chains.append((jnp.concatenate(qs, axis=0), ks, vs))
            heads.append(kh * GQA_GROUP + g0)
    for o, h0 in zip(_attend(chains, dh ** -0.5), heads):
        for g in range(heads_per_chain):
            o_ref[:, (h0 + g) * dh:(h0 + g + 1) * dh] = o[g * tq:(g + 1) * tq]


def _gqa_attn_call(proj, q_norm, segs, tables, *, batch, n_q, row_off, tq, kv_per_step, name):
    rope = tables is not None
    dh = GQA_HEAD_DIM
    qw = GQA_GROUP * dh * kv_per_step
    kw = dh * kv_per_step
    nqb = n_q // tq
    off = row_off // tq
    in_specs = [pl.BlockSpec((tq, qw), lambda b, h, i: (b * nqb + i + off, h)),
                pl.BlockSpec((1, dh), lambda b, h, i: (0, 0))]
    args = [proj, q_norm.reshape(1, dh)]
    if rope:
        in_specs += [pl.BlockSpec((tq, dh), lambda b, h, i: (i, 0))] * 2
        args += list(tables)
    for k2d, v2d, rows in segs:
        in_specs += [pl.BlockSpec((rows, kw), lambda b, h, i: (b, h))] * 2
        args += [k2d, v2d]
    return pl.pallas_call(
        functools.partial(_gqa_attn_kernel, n_seg=len(segs), rope=rope, kv_per_step=kv_per_step),
        grid=(batch, GQA_KV_HEADS // kv_per_step, nqb),
        in_specs=in_specs,
        out_specs=pl.BlockSpec((tq, qw), lambda b, h, i: (b * nqb + i, h)),
        out_shape=jax.ShapeDtypeStruct((batch * n_q, GQA_HEADS * dh), F32),
        compiler_params=_cparams("arbitrary", "arbitrary", "arbitrary"),
        name=name,
    )(*args)


def _gqa_layer(x, mods_l, li, cache_k, cache_v, gqa_w_in, gqa_q_norm, gqa_k_norm):
    proj = _modproj_call(x, mods_l, 0, 1, gqa_w_in[li], GQA_IN, name="gqa_in")
    tables = _rope_tables(DEC_SEQ, GQA_HEAD_DIM, 1)
    nkv = GQA_KV_HEADS * GQA_HEAD_DIM
    k_p, v_p = _gqa_prep_call(proj, gqa_k_norm[li], 0, T_CTX, None, name="gqa_prep_ctx")
    k_s, v_s = _gqa_prep_call(proj, gqa_k_norm[li], T_CTX, T_LAT, tables, name="gqa_prep_lat")
    o_p = _gqa_attn_call(proj, gqa_q_norm[li], [(k_p, v_p, SEQ)], None, batch=BATCH, n_q=SEQ, row_off=0,
                         tq=SEQ, kv_per_step=GQA_KV_HEADS, name="gqa_attn_ctx")
    ck = cache_k[:, li].reshape(DEC_BATCH * PAST_LEN, nkv)
    cv = cache_v[:, li].reshape(DEC_BATCH * PAST_LEN, nkv)
    o_s = _gqa_attn_call(proj, gqa_q_norm[li], [(ck, cv, PAST_LEN), (k_s, v_s, DEC_SEQ)], tables,
                         batch=DEC_BATCH, n_q=DEC_SEQ, row_off=T_CTX, tq=256, kv_per_step=1, name="gqa_attn_lat")
    new_k = k_p.reshape(BATCH, SEQ, GQA_KV_HEADS, GQA_HEAD_DIM)
    new_v = v_p.reshape(BATCH, SEQ, GQA_KV_HEADS, GQA_HEAD_DIM)
    return (o_p, o_s), new_k, new_v


def _krope_kernel(x_ref, *rest, rope):
    if rope:
        cos_ref, sin_ref, o_ref = rest
    else:
        (o_ref,) = rest
    x = x_ref[...]
    x2 = jnp.concatenate([x, x], axis=1)
    if rope:
        x2 = _rope(x2, cos_ref[...], sin_ref[...], MLA_ROPE // 4)
    o_ref[...] = x2


def _krope_call(krope, tables, *, tm=512, name):
    rope = tables is not None
    n = krope.shape[0]
    in_specs = [pl.BlockSpec((tm, MLA_ROPE), lambda i: (i, 0))]
    args = [krope]
    if rope:
        per_seq = DEC_SEQ // tm
        in_specs += [pl.BlockSpec((tm, 2 * MLA_ROPE), lambda i: (i % per_seq, 0))] * 2
        args += list(tables)
    return pl.pallas_call(
        functools.partial(_krope_kernel, rope=rope),
        grid=(n // tm,),
        in_specs=in_specs,
        out_specs=pl.BlockSpec((tm, 2 * MLA_ROPE), lambda i: (i, 0)),
        out_shape=jax.ShapeDtypeStruct((n, 2 * MLA_ROPE), F32),
        compiler_params=_cparams("arbitrary"),
        name=name,
    )(*args)


def _mla_attn_kernel(qn_ref, qr_ref, *rest, n_seg, rope, heads_per_step):
    if rope:
        cos_ref, sin_ref = rest[:2]
        rest = rest[2:]
    seg_refs = rest[:2 * n_seg]
    o_ref = rest[2 * n_seg]
    scale = (MLA_NOPE + MLA_ROPE) ** -0.5
    h0 = pl.program_id(1) * heads_per_step
    pair_w = 2 * MLA_ROPE
    tq = qn_ref.shape[0]
    row_splits = 2 if heads_per_step == 1 else 1
    rows = tq // row_splits
    chains = []
    for j in range(heads_per_step):
        qn = qn_ref[:, j * MLA_NOPE:(j + 1) * MLA_NOPE]
        pair = qr_ref[:, (j // 2) * pair_w:(j // 2 + 1) * pair_w]
        if rope:
            pair = _rope(pair, cos_ref[...], sin_ref[...], MLA_ROPE // 4)
        lane = lax.broadcasted_iota(jnp.int32, pair.shape, 1)
        mine = (lane >> int(math.log2(MLA_ROPE))) == ((h0 + j) & 1)
        q = jnp.concatenate([qn, jnp.where(mine, pair, 0.0)], axis=1).astype(BF16)
        ks, vs = [], []
        for sgi in range(n_seg):
            kv = seg_refs[2 * sgi][:, j * (MLA_NOPE + MLA_V):(j + 1) * (MLA_NOPE + MLA_V)]
            ks.append(jnp.concatenate([kv[:, :MLA_NOPE], seg_refs[2 * sgi + 1][...]], axis=1).astype(BF16))
            vs.append(kv[:, MLA_NOPE:].astype(BF16))
        for c in range(row_splits):
            chains.append((q[c * rows:(c + 1) * rows], ks, vs))
    outs = _attend(chains, scale)
    for j in range(heads_per_step):
        for c in range(row_splits):
            o_ref[c * rows:(c + 1) * rows, j * MLA_V:(j + 1) * MLA_V] = outs[j * row_splits + c]


def _mla_attn_call(q, segs, tables, *, batch, n_q, row_off, tq, heads_per_step, name):
    rope = tables is not None
    hps = heads_per_step
    nqb = n_q // tq
    off = row_off // tq
    rope_w = max(2 * MLA_ROPE, MLA_ROPE * hps)
    rope_base = MLA_HEADS * MLA_NOPE // rope_w

    def rope_idx(b, h, i):
        return (b * nqb + i + off, rope_base + (h * hps * MLA_ROPE) // rope_w)

    in_specs = [pl.BlockSpec((tq, MLA_NOPE * hps), lambda b, h, i: (b * nqb + i + off, h)),
                pl.BlockSpec((tq, rope_w), rope_idx)]
    args = [q, q]
    if rope:
        in_specs += [pl.BlockSpec((tq, 2 * MLA_ROPE), lambda b, h, i: (i, 0))] * 2
        args += list(tables)
    for kv2d, kr2, rows, kv_row_off in segs:
        kv_off = kv_row_off // rows
        in_specs += [pl.BlockSpec((rows, (MLA_NOPE + MLA_V) * hps), lambda b, h, i, kv_off=kv_off: (b + kv_off, h)),
                     pl.BlockSpec((rows, 2 * MLA_ROPE), lambda b, h, i: (b, 0))]
        args += [kv2d, kr2]
    return pl.pallas_call(
        functools.partial(_mla_attn_kernel, n_seg=len(segs), rope=rope, heads_per_step=hps),
        grid=(batch, MLA_HEADS // hps, nqb),
        in_specs=in_specs,
        out_specs=pl.BlockSpec((tq, MLA_V * hps), lambda b, h, i: (b * nqb + i, h)),
        out_shape=jax.ShapeDtypeStruct((batch * n_q, MLA_HEADS * MLA_V), F32),
        compiler_params=_cparams("arbitrary", "arbitrary", "arbitrary"),
        name=name,
    )(*args)


def _mla_layer(x, mods_l, li, cache_ckv, cache_krope, mla_w_in, mla_q_norm, mla_kv_norm, mla_w_q_up, mla_w_kv_up):
    w_in = jnp.pad(mla_w_in[li], ((0, 0), (0, 7 * V7X_LANES - MLA_IN)))
    proj = _modproj_call(x, mods_l, 0, 1, w_in, 7 * V7X_LANES, tn=7 * V7X_LANES, name="mla_in")
    wq = mla_w_q_up[li].reshape(MLA_Q_LORA, MLA_HEADS, MLA_NOPE + MLA_ROPE)
    wq = jnp.concatenate([wq[:, :, :MLA_NOPE].reshape(MLA_Q_LORA, -1), wq[:, :, MLA_NOPE:].reshape(MLA_Q_LORA, -1)], axis=1)
    q = _rmsproj_call(proj, 0, MLA_Q_LORA, mla_q_norm[li], wq, name="mla_q_up")
    kv, ckv = _rmsproj_call(proj, MLA_Q_LORA // MLA_KV_LORA, MLA_KV_LORA, mla_kv_norm[li], mla_w_kv_up[li],
                            emit_xn=True, name="mla_kv_up")
    krope = proj[:, MLA_Q_LORA + MLA_KV_LORA:MLA_IN]
    tables = _rope_tables(DEC_SEQ, MLA_ROPE, 2)
    kr2_p = _krope_call(krope[:T_CTX], None, name="mla_krope_ctx")
    kr2_s = _krope_call(krope[T_CTX:], tables, name="mla_krope_lat")
    ctx_ckv = cache_ckv[:, li].reshape(DEC_BATCH * PAST_LEN, MLA_KV_LORA)
    kv_c = _rmsproj_call(ctx_ckv, 0, MLA_KV_LORA, jnp.ones((MLA_KV_LORA,), F32), mla_w_kv_up[li], normalize=False,
                         name="mla_kv_up_cache")
    kr2_c = _krope_call(cache_krope[:, li].reshape(DEC_BATCH * PAST_LEN, MLA_ROPE), None, name="mla_krope_cache")
    o_p = _mla_attn_call(q, [(kv, kr2_p, SEQ, 0)], None, batch=BATCH, n_q=SEQ, row_off=0, tq=SEQ,
                         heads_per_step=MLA_HEADS, name="mla_attn_ctx")
    o_s = _mla_attn_call(q, [(kv_c, kr2_c, PAST_LEN, 0), (kv, kr2_s, DEC_SEQ, T_CTX)], tables, batch=DEC_BATCH,
                         n_q=DEC_SEQ, row_off=T_CTX, tq=512, heads_per_step=1, name="mla_attn_lat")
    new_ckv = ckv[:T_CTX].reshape(BATCH, SEQ, MLA_KV_LORA)
    new_krope = krope[:T_CTX].reshape(BATCH, SEQ, MLA_ROPE)
    return (o_p, o_s), new_ckv, new_krope


def _router_kernel(x_ref, sh_ref, sc_ref, rw_ref, rb_ref, o_ref, *, tiles_per_seg):
    i = pl.program_id(0)
    r = _seg_row(i, tiles_per_seg)
    h = x_ref[...] * (1.0 + sc_ref[pl.ds(r, 1), :]) + sh_ref[pl.ds(r, 1), :]
    logits = lax.dot_general(rw_ref[...], h, (((1,), (1,)), ((), ())), precision=lax.Precision.HIGHEST,
                             preferred_element_type=F32)[:N_EXPERTS]
    scores = _sigmoid(logits)
    sel = scores + rb_ref[0:N_EXPERTS, :]
    tm = sel.shape[1]
    neg = -jnp.inf
    gs = []
    for g in range(N_GROUPS):
        a, b, c, d = (sel[EXPERTS_PER_GROUP * g + j:EXPERTS_PER_GROUP * g + j + 1, :] for j in range(4))
        gs.append(jnp.maximum(jnp.maximum(jnp.maximum(a + b, a + c), jnp.maximum(a + d, b + c)),
                              jnp.maximum(b + d, c + d)))
    gmax = jnp.maximum(jnp.maximum(gs[0], gs[1]), jnp.maximum(gs[2], gs[3]))
    best = jnp.where(gs[0] == gmax, 0, jnp.where(gs[1] == gmax, 1, jnp.where(gs[2] == gmax, 2, 3)))
    eid = lax.broadcasted_iota(jnp.int32, (N_EXPERTS, tm), 0)
    masked = jnp.where((eid >> int(math.log2(EXPERTS_PER_GROUP))) == best, sel, neg)
    m1 = jnp.max(masked, axis=0, keepdims=True)
    i1 = jnp.min(jnp.where(masked == m1, eid, N_EXPERTS), axis=0, keepdims=True)
    masked2 = jnp.where(eid == i1, neg, masked)
    m2 = jnp.max(masked2, axis=0, keepdims=True)
    i2 = jnp.min(jnp.where(masked2 == m2, eid, N_EXPERTS), axis=0, keepdims=True)
    w1 = jnp.sum(jnp.where(eid == i1, scores, 0.0), axis=0, keepdims=True)
    w2 = jnp.sum(jnp.where(eid == i2, scores, 0.0), axis=0, keepdims=True)
    tot = w1 + w2
    comb = jnp.where(eid == i1, w1 / tot, 0.0) + jnp.where(eid == i2, w2 / tot, 0.0)
    comb = jnp.concatenate([comb, jnp.zeros((V7X_LANES - N_EXPERTS, tm), F32)], axis=0)
    o_ref[...] = comb.T


def _router_call(x, mods_l, router_w, router_b, *, tm=512):
    t = T_ALL
    rw = jnp.pad(router_w.T, ((0, V7X_LANES - N_EXPERTS), (0, 0)))
    rb = jnp.pad(router_b.reshape(N_EXPERTS, 1), ((0, V7X_LANES - N_EXPERTS), (0, 0)))
    return pl.pallas_call(
        functools.partial(_router_kernel, tiles_per_seg=SEG_ROWS // tm),
        grid=(t // tm,),
        in_specs=[
            pl.BlockSpec((tm, D_MODEL), lambda i: (i, 0)),
            pl.BlockSpec((MOD_ROWS, D_MODEL), lambda i: (0, 3)),
            pl.BlockSpec((MOD_ROWS, D_MODEL), lambda i: (0, 4)),
            pl.BlockSpec((V7X_LANES, D_MODEL), lambda i: (0, 0)),
            pl.BlockSpec((V7X_LANES, 1), lambda i: (0, 0)),
        ],
        out_specs=pl.BlockSpec((tm, V7X_LANES), lambda i: (i, 0)),
        out_shape=jax.ShapeDtypeStruct((t, V7X_LANES), F32),
        compiler_params=_cparams("arbitrary"),
        name="moe_router",
    )(x, mods_l, mods_l, rw, rb)


def _moe_kernel(x_ref, sh_ref, sc_ref, gate_ref, comb_ref, wg_ref, wu_ref, wd_ref, g_ref, b_ref, *rest,
                tiles_per_seg, ctx_tiles, split_out):
    if split_out:
        oc_ref, ol_ref, h_scr, acc_scr = rest
    else:
        o_ref, h_scr, acc_scr = rest
    i, e = pl.program_id(0), pl.program_id(1)
    r = _seg_row(i, tiles_per_seg)

    @pl.when(e == 0)
    def _():
        h_scr[...] = (x_ref[...] * (1.0 + sc_ref[pl.ds(r, 1), :]) + sh_ref[pl.ds(r, 1), :]).astype(BF16)
        acc_scr[...] = jnp.zeros_like(acc_scr)

    comb = comb_ref[...]
    lane = lax.broadcasted_iota(jnp.int32, comb.shape, 1)
    c_e = jnp.sum(jnp.where(lane == e, comb, 0.0), axis=1, keepdims=True)
    h = h_scr[...]
    hid = _silu(_mm(h, wg_ref[...])) * _mm(h, wu_ref[...]) * c_e
    acc_scr[...] += _mm(hid, wd_ref[...])

    def result():
        gate = gate_ref[pl.ds(r, 1), :]
        return _layer_norm(ALPHA * x_ref[...] + (1.0 + gate) * acc_scr[...], g_ref[...], b_ref[...])

    last = e == N_EXPERTS - 1
    if split_out:
        @pl.when(last & (i < ctx_tiles))
        def _():
            oc_ref[...] = result()

        @pl.when(last & (i >= ctx_tiles))
        def _():
            ol_ref[...] = result()
    else:
        @pl.when(last)
        def _():
            o_ref[...] = result()


def _moe_call(x, mods_l, comb, w_gate, w_up, w_down, g, b, *, split_out, tm=1024):
    t = T_ALL
    ctx_tiles = T_CTX // tm
    if split_out:
        out_specs = [pl.BlockSpec((tm, D_MODEL), lambda i, e: (jnp.minimum(i, ctx_tiles - 1), 0)),
                     pl.BlockSpec((tm, D_MODEL), lambda i, e: (jnp.maximum(i - ctx_tiles, 0), 0))]
        out_shape = [jax.ShapeDtypeStruct((T_CTX, D_MODEL), F32), jax.ShapeDtypeStruct((T_LAT, D_MODEL), F32)]
    else:
        out_specs = pl.BlockSpec((tm, D_MODEL), lambda i, e: (i, 0))
        out_shape = jax.ShapeDtypeStruct((t, D_MODEL), F32)
    return pl.pallas_call(
        functools.partial(_moe_kernel, tiles_per_seg=SEG_ROWS // tm, ctx_tiles=ctx_tiles, split_out=split_out),
        grid=(t // tm, N_EXPERTS),
        in_specs=[
            pl.BlockSpec((tm, D_MODEL), lambda i, e: (i, 0)),
            pl.BlockSpec((MOD_ROWS, D_MODEL), lambda i, e: (0, 3)),
            pl.BlockSpec((MOD_ROWS, D_MODEL), lambda i, e: (0, 4)),
            pl.BlockSpec((MOD_ROWS, D_MODEL), lambda i, e: (0, 5)),
            pl.BlockSpec((tm, V7X_LANES), lambda i, e: (i, 0)),
            pl.BlockSpec((None, D_MODEL, D_EXPERT), lambda i, e: (e, 0, 0)),
            pl.BlockSpec((None, D_MODEL, D_EXPERT), lambda i, e: (e, 0, 0)),
            pl.BlockSpec((None, D_EXPERT, D_MODEL), lambda i, e: (e, 0, 0)),
            pl.BlockSpec((1, D_MODEL), lambda i, e: (0, 0)),
            pl.BlockSpec((1, D_MODEL), lambda i, e: (0, 0)),
        ],
        out_specs=out_specs,
        out_shape=out_shape,
        scratch_shapes=[pltpu.VMEM((tm, D_MODEL), BF16), pltpu.VMEM((tm, D_MODEL), F32)],
        compiler_params=_cparams("arbitrary", "arbitrary"),
        name="moe_experts",
    )(x, mods_l, mods_l, mods_l, comb, w_gate, w_up, w_down, g.reshape(1, D_MODEL), b.reshape(1, D_MODEL))


def kernel(x_prompt, x_sample, state_gdn, cache_gqa_k, cache_gqa_v, cache_mla_ckv, cache_mla_krope, c, c_ctx, w_ada, b_ada, ln_mix_g, ln_mix_b, ln_ffn_g, ln_ffn_b, gdn_w_in, gdn_conv, gdn_a_log, gdn_dt_bias, gdn_norm_g, gdn_w_out, gqa_w_in, gqa_q_norm, gqa_k_norm, gqa_w_out, mla_w_in, mla_q_norm, mla_kv_norm, mla_w_q_up, mla_w_kv_up, mla_w_out, router_w, router_b, moe_w_gate, moe_w_up, moe_w_down):
    x = jnp.concatenate([x_prompt.reshape(T_CTX, D_MODEL), x_sample.reshape(T_LAT, D_MODEL)], axis=0)
    cond =jnp.concatenate([c_ctx[None], c, jnp.zeros((MOD_ROWS - 1 - DEC_BATCH, D_MODEL), F32)], axis=0)
    mods = _ada_call(cond, w_ada, b_ada)

    st_gdn, st_k, st_v, st_ckv, st_krope = [], [], [], [], []
    for l in range(DEPTH):
        kind, li = l % N_MIXERS, l // N_MIXERS
        mods_l = mods[l]
        if kind == 0:
            o, st = _gdn_layer(x, mods_l, li, state_gdn, gdn_w_in, gdn_conv, gdn_a_log, gdn_dt_bias, gdn_norm_g)
            st_gdn.append(st)
            w_out = gdn_w_out[li]
        elif kind == 1:
            o, nk, nv = _gqa_layer(x, mods_l, li, cache_gqa_k, cache_gqa_v, gqa_w_in, gqa_q_norm, gqa_k_norm)
            st_k.append(nk)
            st_v.append(nv)
            w_out = gqa_w_out[li]
        else:
            o, nckv, nkr = _mla_layer(x, mods_l, li, cache_mla_ckv, cache_mla_krope, mla_w_in, mla_q_norm,
                                      mla_kv_norm, mla_w_q_up, mla_w_kv_up)
            st_ckv.append(nckv)
            st_krope.append(nkr)
            w_out = mla_w_out[li]
        x = _outln_call(o[0], o[1], w_out, x, mods_l, 2, ln_mix_g[l], ln_mix_b[l], name="mix_out_ln")
        comb = _router_call(x, mods_l, router_w, router_b)
        x = _moe_call(x, mods_l, comb, moe_w_gate[l], moe_w_up[l], moe_w_down[l], ln_ffn_g[l], ln_ffn_b[l],
                      split_out=(l == DEPTH - 1))

    y_prompt = x[0].reshape(BATCH, SEQ, D_MODEL)
    y_sample = x[1].reshape(DEC_BATCH, DEC_SEQ, D_MODEL)
    return (y_prompt, y_sample, jnp.stack(st_gdn, axis=1), jnp.stack(st_k, axis=1), jnp.stack(st_v, axis=1),
            jnp.stack(st_ckv, axis=1), jnp.stack(st_krope, axis=1))
```

```python
import functools
import math

import jax
import jax.numpy as jnp
from jax import lax
from jax.experimental import pallas as pl
from jax.experimental.pallas import tpu as pltpu

F32 = jnp.float32
BF16 = jnp.bfloat16

D_MODEL = 1024
BATCH = 32
SEQ = 256
DEPTH = 4
DEC_BATCH = 2
DEC_SEQ = 2048
PAST_LEN = 512
GRID_W = 64
N_MIXERS = 3

GDN_HEADS = 8
GDN_DK = 128
GDN_DV = 128
GDN_CONV = 5
GDN_CHUNK = 64
GDN_QKV = GDN_HEADS * (2 * GDN_DK + GDN_DV)
GDN_QKVZ = GDN_QKV + GDN_HEADS * GDN_DV

GQA_HEADS = 8
GQA_KV_HEADS = 2
GQA_GROUP = GQA_HEADS // GQA_KV_HEADS
GQA_HEAD_DIM = 128
GQA_IN = (GQA_HEADS + 2 * GQA_KV_HEADS) * GQA_HEAD_DIM

MLA_HEADS = 8
MLA_Q_LORA = 512
MLA_KV_LORA = 256
MLA_NOPE = 128
MLA_ROPE = 64
MLA_V = 128
MLA_IN = MLA_Q_LORA + MLA_KV_LORA + MLA_ROPE

N_EXPERTS = 16
N_GROUPS = 4
EXPERTS_PER_GROUP = N_EXPERTS // N_GROUPS
D_EXPERT = 256

ROPE_THETA = 10000.0
ALPHA = (2 * DEPTH) ** 0.25
NORM_EPS = 1e-6

T_CTX = BATCH * SEQ
T_LAT = DEC_BATCH * DEC_SEQ
T_ALL = T_CTX + T_LAT
SEG_ROWS = DEC_SEQ
N_CTX_SEGS = T_CTX // SEG_ROWS
MOD_ROWS = 8

V7X_LANES = 128
V7X_SUBLANES = 8
V7X_VMEM_BYTES = 64 * 1024 * 1024
VMEM_LIMIT = V7X_VMEM_BYTES * 7 // 8

GDN_SUPER = 256
CHUNKS_PER_SUPER = GDN_SUPER // GDN_CHUNK


def _cparams(*sem):
    return pltpu.CompilerParams(dimension_semantics=sem, vmem_limit_bytes=VMEM_LIMIT)


def _mm(a, b):
    return jnp.dot(a.astype(BF16), b.astype(BF16), preferred_element_type=F32)


def _mm_nt(a, b):
    return lax.dot_general(a.astype(BF16), b.astype(BF16), (((1,), (1,)), ((), ())),
                           preferred_element_type=F32)


def _silu(x):
    return x * (1.0 / (1.0 + jnp.exp(-x)))


def _sigmoid(x):
    return 1.0 / (1.0 + jnp.exp(-x))


def _chunk_of(idx):
    return idx >> int(math.log2(GDN_CHUNK))


def _seg_row(i, tiles_per_seg):
    return jnp.maximum(lax.div(i, tiles_per_seg) - (N_CTX_SEGS - 1), 0)


def _layer_norm(y, g, b):
    mu = jnp.mean(y, axis=-1, keepdims=True)
    yc = y - mu
    var = jnp.mean(yc * yc, axis=-1, keepdims=True)
    return yc * lax.rsqrt(var + NORM_EPS) * g + b


def _rms(x, g):
    return x * lax.rsqrt(jnp.mean(x * x, axis=-1, keepdims=True) + NORM_EPS) * g


def _ada_kernel(c_ref, w_ref, b_ref, o_ref):
    o_ref[...] = _mm(_silu(c_ref[...]), w_ref[...]) + b_ref[...]


def _ada_call(cond, w_ada, b_ada):
    tn = 1536
    n = w_ada.shape[-1]
    return pl.pallas_call(
        _ada_kernel,
        grid=(DEPTH, n // tn),
        in_specs=[
            pl.BlockSpec((MOD_ROWS, D_MODEL), lambda l, j: (0, 0)),
            pl.BlockSpec((None, D_MODEL, tn), lambda l, j: (l, 0, j)),
            pl.BlockSpec((None, 1, tn), lambda l, j: (l, 0, j)),
        ],
        out_specs=pl.BlockSpec((None, MOD_ROWS, tn), lambda l, j: (l, 0, j)),
        out_shape=jax.ShapeDtypeStruct((DEPTH, MOD_ROWS, n), F32),
        compiler_params=_cparams("arbitrary", "arbitrary"),
        name="ada_mod",
    )(cond, w_ada, b_ada.reshape(DEPTH, 1, n))


def _modproj_kernel(x_ref, sh_ref, sc_ref, w_ref, *rest, tiles_per_seg, has_extra):
    if has_extra:
        wx_ref, o_ref, ox_ref, h_scr, wb_scr = rest
    else:
        o_ref, h_scr, wb_scr = rest
    i, j = pl.program_id(0), pl.program_id(1)

    @pl.when(i == 0)
    def _():
        wb_scr[j] = w_ref[...].astype(BF16)

    @pl.when(j == 0)
    def _():
        r = _seg_row(i, tiles_per_seg)
        sh = sh_ref[pl.ds(r, 1), :]
        sc = sc_ref[pl.ds(r, 1), :]
        h = (x_ref[...] * (1.0 + sc) + sh).astype(BF16)
        h_scr[...] = h
        if has_extra:
            ox_ref[...] = _mm(h, wx_ref[...])

    o_ref[...] = jnp.dot(h_scr[...], wb_scr[j], preferred_element_type=F32).astype(o_ref.dtype)


def _modproj_call(x, mods_l, shift_idx, scale_idx, w, n_out, w_extra=None, *, w_layer=None, tm=1024, tn=512,
                  out_dtype=F32, name):
    t = T_ALL
    has_extra = w_extra is not None
    nj = n_out // tn
    if w_layer is None:
        w_spec = pl.BlockSpec((D_MODEL, tn), lambda i, j: (0, jnp.where(i == 0, j, nj - 1)))
    else:
        w_spec = pl.BlockSpec((None, D_MODEL, tn), lambda i, j: (w_layer, 0, jnp.where(i == 0, j, nj - 1)))
    in_specs = [
        pl.BlockSpec((tm, D_MODEL), lambda i, j: (i, 0)),
        pl.BlockSpec((MOD_ROWS, D_MODEL), lambda i, j: (0, shift_idx)),
        pl.BlockSpec((MOD_ROWS, D_MODEL), lambda i, j: (0, scale_idx)),
        w_spec,
    ]
    out_specs = [pl.BlockSpec((tm, tn), lambda i, j: (i, j))]
    out_shape = [jax.ShapeDtypeStruct((t, n_out), out_dtype)]
    args = [x, mods_l, mods_l, w]
    if has_extra:
        nx = w_extra.shape[1]
        in_specs.append(pl.BlockSpec((D_MODEL, nx), lambda i, j: (0, 0)))
        out_specs.append(pl.BlockSpec((tm, nx), lambda i, j: (i, 0)))
        out_shape.append(jax.ShapeDtypeStruct((t, nx), F32))
        args.append(w_extra)
    res = pl.pallas_call(
        functools.partial(_modproj_kernel, tiles_per_seg=SEG_ROWS // tm, has_extra=has_extra),
        grid=(t // tm, nj),
        in_specs=in_specs,
        out_specs=out_specs,
        out_shape=out_shape,
        scratch_shapes=[pltpu.VMEM((tm, D_MODEL), BF16), pltpu.VMEM((nj, D_MODEL, tn), BF16)],
        compiler_params=_cparams("arbitrary", "arbitrary"),
        name=name,
    )(*args)
    return res if has_extra else res[0]


def _rmsproj_kernel(x_ref, g_ref, w_ref, *rest, normalize, emit_xn):
    if emit_xn:
        o_ref, xn_ref, h_scr = rest
    else:
        o_ref, h_scr = rest
    j = pl.program_id(1)

    @pl.when(j == 0)
    def _():
        x = x_ref[...]
        xn = _rms(x, g_ref[...]) if normalize else x
        h_scr[...] = xn.astype(BF16)
        if emit_xn:
            xn_ref[...] = xn

    o_ref[...] = _mm(h_scr[...], w_ref[...])


def _rmsproj_call(x, col_block, k, g, w, *, normalize=True, emit_xn=False, tm=1024, tn=512, name):
    t = x.shape[0]
    n_out = w.shape[1]
    out_specs = [pl.BlockSpec((tm, tn), lambda i, j: (i, j))]
    out_shape = [jax.ShapeDtypeStruct((t, n_out), F32)]
    if emit_xn:
        out_specs.append(pl.BlockSpec((tm, k), lambda i, j: (i, 0)))
        out_shape.append(jax.ShapeDtypeStruct((t, k), F32))
    res = pl.pallas_call(
        functools.partial(_rmsproj_kernel, normalize=normalize, emit_xn=emit_xn),
        grid=(t // tm, n_out // tn),
        in_specs=[
            pl.BlockSpec((tm, k), lambda i, j: (i, col_block)),
            pl.BlockSpec((1, k), lambda i, j: (0, 0)),
            pl.BlockSpec((k, tn), lambda i, j: (0, j)),
        ],
        out_specs=out_specs,
        out_shape=out_shape,
        scratch_shapes=[pltpu.VMEM((tm, k), BF16)],
        compiler_params=_cparams("arbitrary", "arbitrary"),
        name=name,
    )(x, g.reshape(1, k), w)
    return res if emit_xn else res[0]


def _outln_kernel(oc_ref, ol_ref, w_ref, x_ref, gate_ref, g_ref, b_ref, out_ref, wb_scr, *, tiles_per_seg, ctx_tiles):
    i = pl.program_id(0)

    @pl.when(i == 0)
    def _():
        wb_scr[...] = w_ref[...].astype(BF16)

    gate = gate_ref[pl.ds(_seg_row(i, tiles_per_seg), 1), :]
    o = jnp.where(i < ctx_tiles, oc_ref[...], ol_ref[...])
    f = jnp.dot(o.astype(BF16), wb_scr[...], preferred_element_type=F32)
    out_ref[...] = _layer_norm(ALPHA * x_ref[...] + (1.0 + gate) * f, g_ref[...], b_ref[...])


def _outln_call(o_ctx, o_lat, w, w_layer, x, mods_l, gate_idx, g, b, *, tm=512, name):
    k = o_ctx.shape[1]
    t = T_ALL
    ctx_tiles = T_CTX // tm
    return pl.pallas_call(
        functools.partial(_outln_kernel, tiles_per_seg=SEG_ROWS // tm, ctx_tiles=ctx_tiles),
        grid=(t // tm,),
        in_specs=[
            pl.BlockSpec((tm, k), lambda i: (jnp.minimum(i, ctx_tiles - 1), 0)),
            pl.BlockSpec((tm, k), lambda i: (jnp.maximum(i - ctx_tiles, 0), 0)),
            pl.BlockSpec((None, k, D_MODEL), lambda i: (w_layer, 0, 0)),
            pl.BlockSpec((tm, D_MODEL), lambda i: (i, 0)),
            pl.BlockSpec((MOD_ROWS, D_MODEL), lambda i: (0, gate_idx)),
            pl.BlockSpec((1, D_MODEL), lambda i: (0, 0)),
            pl.BlockSpec((1, D_MODEL), lambda i: (0, 0)),
        ],
        out_specs=pl.BlockSpec((tm, D_MODEL), lambda i: (i, 0)),
        out_shape=jax.ShapeDtypeStruct((t, D_MODEL), F32),
        scratch_shapes=[pltpu.VMEM((k, D_MODEL), BF16)],
        compiler_params=_cparams("arbitrary"),
        name=name,
    )(o_ctx, o_lat, w, x, mods_l, g.reshape(1, D_MODEL), b.reshape(1, D_MODEL))


def _gdn_gates_kernel(ab_ref, alog_ref, dt_ref, o_ref):
    ab = ab_ref[...]
    tm = ab.shape[0]
    x = ab + dt_ref[...]
    softplus = jnp.maximum(x, 0.0) + jnp.log(1.0 + jnp.exp(-jnp.abs(x)))
    g = -jnp.exp(alog_ref[...]) * softplus
    ri = lax.broadcasted_iota(jnp.int32, (tm, tm), 0)
    ci = lax.broadcasted_iota(jnp.int32, (tm, tm), 1)
    same = _chunk_of(ri) == _chunk_of(ci)
    lower = jnp.where(same & (ri >= ci), 1.0, 0.0).astype(F32)
    upper = jnp.where(same & (ri <= ci), 1.0, 0.0).astype(F32)
    pre = jnp.dot(lower, g, precision=lax.Precision.HIGHEST, preferred_element_type=F32)
    suf = jnp.dot(upper, g, precision=lax.Precision.HIGHEST, preferred_element_type=F32)
    lane = lax.broadcasted_iota(jnp.int32, ab.shape, 1)
    o_ref[...] = jnp.where(lane < GDN_HEADS, pre, jnp.where(lane < 2 * GDN_HEADS, suf, _sigmoid(ab)))


def _gdn_gates_call(ab, a_log, dt_bias):
    t, n = ab.shape
    tm = GDN_SUPER
    pad = n - 2 * GDN_HEADS
    alog_row = jnp.pad(a_log.reshape(1, -1), ((0, 0), (0, pad)))
    dt_row = jnp.pad(dt_bias.reshape(1, -1), ((0, 0), (0, pad)))
    return pl.pallas_call(
        _gdn_gates_kernel,
        grid=(t // tm,),
        in_specs=[
            pl.BlockSpec((tm, n), lambda i: (i, 0)),
            pl.BlockSpec((1, n), lambda i: (0, 0)),
            pl.BlockSpec((1, n), lambda i: (0, 0)),
        ],
        out_specs=pl.BlockSpec((tm, n), lambda i: (i, 0)),
        out_shape=jax.ShapeDtypeStruct((t, n), F32),
        compiler_params=_cparams("arbitrary"),
        name="gdn_gates",
    )(ab, alog_row, dt_row)


def _loop(n, body, max_static=4):
    if n <= max_static:
        for i in range(n):
            body(i)
    else:
        def wrapped(i, carry):
            body(i)
            return carry
        lax.fori_loop(0, n, wrapped, 0)


def _gdn_core_kernel(q_ref, k_ref, v_ref, z_ref, cq_ref, ck_ref, cv_ref, gc_ref, gr_ref, ng_ref, *rest,
                     n_tok, has_s0, hps):
    if has_s0:
        s0_ref, o_ref, st_ref = rest[:3]
        scr = rest[3:]
    else:
        o_ref, st_ref = rest[:2]
        scr = rest[2:]
    pad_scr, kn_scr, u_scr, w_scr, qe_scr, qk_scr, oo_scr, s_scr = scr
    n_super = n_tok // GDN_SUPER
    n_chunk = n_tok // GDN_CHUNK
    halo = V7X_SUBLANES
    sb = GDN_SUPER
    ck = GDN_CHUNK
    dk = GDN_DK

    zeros_halo = jnp.zeros((halo, dk), F32)
    for hd in range(hps):
        for a, src in enumerate((q_ref, k_ref, v_ref)):
            pad_scr[3 * hd + a, 0:halo, :] = zeros_halo
            pad_scr[3 * hd + a, halo:halo + n_tok, :] = src[:, hd * dk:(hd + 1) * dk].astype(F32)
            pad_scr[3 * hd + a, halo + n_tok:2 * halo + n_tok, :] = zeros_halo
        for d in range(2):
            s_scr[2 * hd + d] = s0_ref[d, hd] if has_s0 else jnp.zeros((dk, GDN_DV), F32)

    ri = lax.broadcasted_iota(jnp.int32, (sb, sb), 0)
    ci = lax.broadcasted_iota(jnp.int32, (sb, sb), 1)
    same_chunk = _chunk_of(ri) == _chunk_of(ci)
    eye = jnp.where(ri == ci, 1.0, 0.0).astype(F32)
    n_levels = int(math.log2(ck))

    def level_mask(l):
        return ((ri >> (l + 1)) == (ci >> (l + 1))) & ((ri >> l) != (ci >> l))

    def conv_silu(a, w_ref, hd, r0):
        xp = pad_scr[a, pl.ds(r0, sb + 2 * halo), :]
        acc = None
        for j in range(GDN_CONV):
            shift = (GDN_CONV // 2 - j) % (sb + 2 * halo)
            tap = xp if shift == 0 else pltpu.roll(xp, shift, 0)
            term = w_ref[j:j + 1, hd * dk:(hd + 1) * dk] * tap[halo:halo + sb, :]
            acc = term if acc is None else acc + term
        return _silu(acc)

    chains = [(hd, d) for hd in range(hps) for d in range(2)]

    a_unroll = max(1, min(n_super, 8 // (2 * hps)))

    def phase_a(it):
        units = [(it * a_unroll + u, hd) for u in range(a_unroll) for hd in range(hps)]
        r0s = [pl.multiple_of(s * sb, sb) for s, _ in units]
        qs, ks, vs, ps, qks = [], [], [], [], []
        for (s, hd), r0 in zip(units, r0s):
            q = conv_silu(3 * hd, cq_ref, hd, r0)
            k = conv_silu(3 * hd + 1, ck_ref, hd, r0)
            v = conv_silu(3 * hd + 2, cv_ref, hd, r0)
            q = q * lax.rsqrt(jnp.sum(q * q, axis=-1, keepdims=True) + NORM_EPS) * (dk ** -0.5)
            k = k * lax.rsqrt(jnp.sum(k * k, axis=-1, keepdims=True) + NORM_EPS)
            kn_scr[hd, pl.ds(r0, sb), :] = k
            kb = k.astype(BF16)
            qs.append(q)
            ks.append(k)
            vs.append(v)
            ps.append(_mm_nt(kb, kb))
            qks.append(_mm_nt(q, kb))
        a_chains = [(n, d) for n in range(len(units)) for d in range(2)]
        a_mats, ts, rhs = [], [], []
        for n, d in a_chains:
            (s, hd), r0 = units[n], r0s[n]
            hdd = 2 * hd + d
            gcol = gc_ref[hd, pl.ds(r0, sb), d:d + 1]
            bcol = gc_ref[hd, pl.ds(r0, sb), 2 + d:3 + d]
            grow = gr_ref[hd, s, d:d + 1, :]
            if d == 0:
                mask, strict = same_chunk & (ri >= ci), same_chunk & (ri > ci)
            else:
                mask, strict = same_chunk & (ri <= ci), same_chunk & (ri < ci)
            dec = jnp.exp(jnp.where(mask, gcol - grow, -jnp.inf))
            a = jnp.where(strict, ps[n] * bcol * dec, 0.0)
            a_mats.append(a)
            ts.append(eye - jnp.where(level_mask(0), a, 0.0))
            eg = jnp.exp(gcol)
            rhs.append(jnp.concatenate([vs[n] * bcol, ks[n] * (bcol * eg)], axis=1).astype(BF16))
            qe_scr[hdd, pl.ds(r0, sb), :] = qs[n] * eg
            qkd = qks[n] * dec
            for c in range(CHUNKS_PER_SUPER):
                qk_scr[hdd, pl.ds(r0 + c * ck, ck), :] = qkd[c * ck:(c + 1) * ck, c * ck:(c + 1) * ck]
        tbs = [t.astype(BF16) for t in ts]
        for l in range(1, n_levels):
            lm = level_mask(l)
            ys = [_mm(jnp.where(lm, a_mats[i], 0.0), tbs[i]) for i in range(len(a_chains))]
            ts = [ts[i] - _mm(tbs[i], ys[i]) for i in range(len(a_chains))]
            tbs = [t.astype(BF16) for t in ts]
        for i, (n, d) in enumerate(a_chains):
            (s, hd), r0 = units[n], r0s[n]
            uw = _mm(tbs[i], rhs[i])
            u_scr[2 * hd + d, pl.ds(r0, sb), :] = uw[:, :GDN_DV]
            w_scr[2 * hd + d, pl.ds(r0, sb), :] = uw[:, GDN_DV:]

    _loop(n_super // a_unroll, phase_a, max_static=1)

    def phase_b(i):
        rows, ws_qs, v_new, kws, gls = [], [], [], [], []
        for hd, d in chains:
            hdd = 2 * hd + d
            c = i if d == 0 else n_chunk - 1 - i
            r = pl.multiple_of(c * ck, ck)
            r_last = r + (ck - 1) if d == 0 else r
            rows.append(r)
            gcol = gc_ref[hd, pl.ds(r, ck), d:d + 1]
            gl = gc_ref[hd, pl.ds(r_last, 1), d:d + 1]
            gls.append(gl)
            kws.append((kn_scr[hd, pl.ds(r, ck), :] * jnp.exp(gl - gcol)).T)
            wq = jnp.concatenate([w_scr[hdd, pl.ds(r, ck), :], qe_scr[hdd, pl.ds(r, ck), :]], axis=0)
            ws_qs.append(_mm(wq, s_scr[hdd]))
        for j, (hd, d) in enumerate(chains):
            hdd = 2 * hd + d
            vn = u_scr[hdd, pl.ds(rows[j], ck), :] - ws_qs[j][:ck]
            v_new.append(vn)
            oo_scr[hdd, pl.ds(rows[j], ck), :] = ws_qs[j][ck:] + _mm(qk_scr[hdd, pl.ds(rows[j], ck), :], vn)
        for j, (hd, d) in enumerate(chains):
            hdd = 2 * hd + d
            s_scr[hdd] = s_scr[hdd] * jnp.exp(gls[j]) + _mm(kws[j], v_new[j])

    _loop(n_chunk, phase_b)

    def finish(s):
        r0 = pl.multiple_of(s * sb, sb)
        for hd in range(hps):
            o = oo_scr[2 * hd, pl.ds(r0, sb), :] + oo_scr[2 * hd + 1, pl.ds(r0, sb), :]
            z = z_ref[pl.ds(r0, sb), hd * GDN_DV:(hd + 1) * GDN_DV].astype(F32)
            o_ref[pl.ds(r0, sb), hd * GDN_DV:(hd + 1) * GDN_DV] = _rms(o, ng_ref[...]) * _silu(z)

    _loop(n_super, finish)
    for hd in range(hps):
        for d in range(2):
            st_ref[d, hd] = s_scr[2 * hd + d]


def _gdn_core_call(qkvz, conv_w, gc, gr, norm_g, s0, *, batch, n_tok, row_block_off, hps, name):
    has_s0 = s0 is not None
    h = GDN_HEADS
    dk = GDN_DK
    n_super = n_tok // GDN_SUPER
    hb = h // hps

    mode = dict(pipeline_mode=pl.Buffered(1)) if n_tok > GDN_SUPER else {}

    def col(sec):
        return pl.BlockSpec((n_tok, dk * hps), lambda b, hh: (b + row_block_off, sec * hb + hh), **mode)

    def cw(sec):
        return pl.BlockSpec((GDN_CONV, dk * hps), lambda b, hh: (0, sec * hb + hh))

    in_specs = [col(0), col(1), col(2), col(3), cw(0), cw(1), cw(2),
                pl.BlockSpec((None, hps, n_tok, 4), lambda b, hh: (b, hh, 0, 0)),
                pl.BlockSpec((None, hps, n_super, 4, GDN_SUPER), lambda b, hh: (b, hh, 0, 0, 0)),
                pl.BlockSpec((1, GDN_DV), lambda b, hh: (0, 0))]
    args = [qkvz, qkvz, qkvz, qkvz, conv_w, conv_w, conv_w, gc, gr, norm_g.reshape(1, GDN_DV)]
    st_spec = pl.BlockSpec((None, 2, hps, dk, GDN_DV), lambda b, hh: (b, 0, hh, 0, 0))
    if has_s0:
        in_specs.append(st_spec)
        args.append(s0)
    return pl.pallas_call(
        functools.partial(_gdn_core_kernel, n_tok=n_tok, has_s0=has_s0, hps=hps),
        grid=(batch, hb),
        in_specs=in_specs,
        out_specs=[pl.BlockSpec((n_tok, GDN_DV * hps), lambda b, hh: (b, hh)), st_spec],
        out_shape=[jax.ShapeDtypeStruct((batch * n_tok, h * GDN_DV), F32),
                   jax.ShapeDtypeStruct((batch, 2, h, dk, GDN_DV), F32)],
        scratch_shapes=[
            pltpu.VMEM((3 * hps, n_tok + 2 * V7X_SUBLANES, dk), F32),
            pltpu.VMEM((hps, n_tok, dk), F32),
            pltpu.VMEM((2 * hps, n_tok, GDN_DV), F32),
            pltpu.VMEM((2 * hps, n_tok, dk), F32),
            pltpu.VMEM((2 * hps, n_tok, dk), F32),
            pltpu.VMEM((2 * hps, n_tok, GDN_CHUNK), F32),
            pltpu.VMEM((2 * hps, n_tok, GDN_DV), F32),
            pltpu.VMEM((2 * hps, dk, GDN_DV), F32),
        ],
        compiler_params=_cparams("arbitrary", "arbitrary"),
        name=name,
    )(*args)


def _gdn_layer(x, mods_l, li, state_gdn, gdn_w_in, gdn_conv, gdn_a_log, gdn_dt_bias, gdn_norm_g):
    w_ab = jnp.pad(gdn_w_in[li, :, GDN_QKVZ:], ((0, 0), (0, V7X_LANES - 4 * GDN_HEADS)))
    qkvz, ab = _modproj_call(x, mods_l, 0, 1, gdn_w_in, GDN_QKVZ, w_ab, w_layer=li, out_dtype=BF16, name="gdn_in")
    gates = _gdn_gates_call(ab, gdn_a_log[li], gdn_dt_bias[li])[:, :4 * GDN_HEADS]

    def head_layouts(g, batch, n_tok):
        g = g.reshape(batch, n_tok, 4, GDN_HEADS)
        gc = jnp.transpose(g, (0, 3, 1, 2))
        gr = jnp.transpose(g, (0, 3, 2, 1)).reshape(batch, GDN_HEADS, 4, n_tok // GDN_SUPER, GDN_SUPER)
        return gc, jnp.transpose(gr, (0, 1, 3, 2, 4))

    gc_p, gr_p = head_layouts(gates[:T_CTX], BATCH, SEQ)
    gc_s, gr_s = head_layouts(gates[T_CTX:], DEC_BATCH, DEC_SEQ)
    o_p, st_p = _gdn_core_call(qkvz, gdn_conv[li], gc_p, gr_p, gdn_norm_g[li], None,
                               batch=BATCH, n_tok=SEQ, row_block_off=0, hps=4, name="gdn_core_ctx")
    o_s, _ = _gdn_core_call(qkvz, gdn_conv[li], gc_s, gr_s, gdn_norm_g[li], state_gdn[:, li],
                            batch=DEC_BATCH, n_tok=DEC_SEQ, row_block_off=T_CTX // DEC_SEQ, hps=2,
                            name="gdn_core_lat")
    return (o_p, o_s), st_p


def _rope_tables(n_tokens, rot_dim, reps):
    n_rows = n_tokens // GRID_W
    row = jnp.repeat(jnp.arange(n_rows, dtype=F32), GRID_W)
    colp = jnp.tile(jnp.arange(GRID_W, dtype=F32), n_rows)
    n_freq = rot_dim // 4
    inv_freq = ROPE_THETA ** (-jnp.arange(n_freq, dtype=F32) / n_freq)
    ra, ca = row[:, None] * inv_freq, colp[:, None] * inv_freq
    cos = jnp.concatenate([jnp.cos(ra), jnp.cos(ra), jnp.cos(ca), jnp.cos(ca)], axis=-1)
    sin = jnp.concatenate([-jnp.sin(ra), jnp.sin(ra), -jnp.sin(ca), jnp.sin(ca)], axis=-1)
    return jnp.tile(cos, (1, reps)), jnp.tile(sin, (1, reps))


def _rope(x, cos, sin, quarter):
    n = x.shape[-1]
    lane = lax.broadcasted_iota(jnp.int32, x.shape, x.ndim - 1)
    first = (lane & (2 * quarter - 1)) < quarter
    partner = jnp.where(first, pltpu.roll(x, n - quarter, x.ndim - 1), pltpu.roll(x, quarter, x.ndim - 1))
    return x * cos + partner * sin


def _gqa_prep_kernel(kv_ref, g_ref, *rest, rope):
    if rope:
        cos_ref, sin_ref, k_ref, v_ref = rest
    else:
        k_ref, v_ref = rest
    kv = kv_ref[...]
    nk = GQA_KV_HEADS * GQA_HEAD_DIM
    for hh in range(GQA_KV_HEADS):
        k = _rms(kv[:, hh * GQA_HEAD_DIM:(hh + 1) * GQA_HEAD_DIM], g_ref[...])
        if rope:
            k = _rope(k, cos_ref[...], sin_ref[...], GQA_HEAD_DIM // 4)
        k_ref[:, hh * GQA_HEAD_DIM:(hh + 1) * GQA_HEAD_DIM] = k
    v_ref[...] = kv[:, nk:]


def _gqa_prep_call(proj, k_norm, row_off, n_rows, tables, *, tm=512, name):
    rope = tables is not None
    nkv = 2 * GQA_KV_HEADS * GQA_HEAD_DIM
    kv_col = GQA_HEADS * GQA_HEAD_DIM // nkv
    off = row_off // tm
    in_specs = [pl.BlockSpec((tm, nkv), lambda i: (i + off, kv_col)),
                pl.BlockSpec((1, GQA_HEAD_DIM), lambda i: (0, 0))]
    args = [proj, k_norm.reshape(1, GQA_HEAD_DIM)]
    if rope:
        per_seq = DEC_SEQ // tm
        in_specs += [pl.BlockSpec((tm, GQA_HEAD_DIM), lambda i: (i % per_seq, 0))] * 2
        args += list(tables)
    half = nkv // 2
    return pl.pallas_call(
        functools.partial(_gqa_prep_kernel, rope=rope),
        grid=(n_rows // tm,),
        in_specs=in_specs,
        out_specs=[pl.BlockSpec((tm, half), lambda i: (i, 0))] * 2,
        out_shape=[jax.ShapeDtypeStruct((n_rows, half), F32)] * 2,
        compiler_params=_cparams("arbitrary"),
        name=name,
    )(*args)


def _attend(chains, scale):
    scores = [[_mm_nt(q, k) * scale for k in ks] for q, ks, _ in chains]
    probs, dens = [], []
    for sc in scores:
        m = None
        for s in sc:
            sm = jnp.max(s, axis=-1, keepdims=True)
            m = sm if m is None else jnp.maximum(m, sm)
        ps = [jnp.exp(s - m) for s in sc]
        den = None
        for p in ps:
            r = jnp.sum(p, axis=-1, keepdims=True)
            den = r if den is None else den + r
        probs.append([p.astype(BF16) for p in ps])
        dens.append(den)
    outs = []
    for (_, _, vs), ps, den in zip(chains, probs, dens):
        acc = None
        for p, v in zip(ps, vs):
            pv = _mm(p, v)
            acc = pv if acc is None else acc + pv
        outs.append(acc / den)
    return outs


def _gqa_attn_kernel(q_ref, qg_ref, *rest, n_seg, rope, kv_per_step):
    if rope:
        cos_ref, sin_ref = rest[:2]
        rest = rest[2:]
    seg_refs = rest[:2 * n_seg]
    o_ref = rest[2 * n_seg]
    dh = GQA_HEAD_DIM
    tq = q_ref.shape[0]
    heads_per_chain = GQA_GROUP // 2
    chains, heads = [], []
    for kh in range(kv_per_step):
        ks = [seg_refs[2 * sgi][:, kh * dh:(kh + 1) * dh].astype(BF16) for sgi in range(n_seg)]
        vs = [seg_refs[2 * sgi + 1][:, kh * dh:(kh + 1) * dh].astype(BF16) for sgi in range(n_seg)]
        for g0 in range(0, GQA_GROUP, heads_per_chain):
            qs = []
            for g in range(g0, g0 + heads_per_chain):
                c0 = (kh * GQA_GROUP + g) * dh
                qh = _rms(q_ref[:, c0:c0 + dh], qg_ref[...])
                if rope:
                    qh = _rope(qh, cos_ref[...], sin_ref[...], dh // 4)
                qs.append(qh.astype(BF16))
            chains.append((jnp.concatenate(qs, axis=0), ks, vs))
            heads.append(kh * GQA_GROUP + g0)
    for o, h0 in zip(_attend(chains, dh ** -0.5), heads):
        for g in range(heads_per_chain):
            o_ref[:, (h0 + g) * dh:(h0 + g + 1) * dh] = o[g * tq:(g + 1) * tq]


def _gqa_attn_call(proj, q_norm, segs, tables, *, batch, n_q, row_off, tq, kv_per_step, name):
    rope = tables is not None
    dh = GQA_HEAD_DIM
    qw = GQA_GROUP * dh * kv_per_step
    kw = dh * kv_per_step
    nqb = n_q // tq
    off = row_off // tq
    in_specs = [pl.BlockSpec((tq, qw), lambda b, h, i: (b * nqb + i + off, h)),
                pl.BlockSpec((1, dh), lambda b, h, i: (0, 0))]
    args = [proj, q_norm.reshape(1, dh)]
    if rope:
        in_specs += [pl.BlockSpec((tq, dh), lambda b, h, i: (i, 0))] * 2
        args += list(tables)
    for k2d, v2d, rows in segs:
        in_specs += [pl.BlockSpec((rows, kw), lambda b, h, i: (b, h))] * 2
        args += [k2d, v2d]
    return pl.pallas_call(
        functools.partial(_gqa_attn_kernel, n_seg=len(segs), rope=rope, kv_per_step=kv_per_step),
        grid=(batch, GQA_KV_HEADS // kv_per_step, nqb),
        in_specs=in_specs,
        out_specs=pl.BlockSpec((tq, qw), lambda b, h, i: (b * nqb + i, h)),
        out_shape=jax.ShapeDtypeStruct((batch * n_q, GQA_HEADS * dh), F32),
        compiler_params=_cparams("arbitrary", "arbitrary", "arbitrary"),
        name=name,
    )(*args)


def _gqa_layer(x, mods_l, li, cache_k, cache_v, gqa_w_in, gqa_q_norm, gqa_k_norm):
    proj = _modproj_call(x, mods_l, 0, 1, gqa_w_in, GQA_IN, w_layer=li, name="gqa_in")
    tables = _rope_tables(DEC_SEQ, GQA_HEAD_DIM, 1)
    nkv = GQA_KV_HEADS * GQA_HEAD_DIM
    k_p, v_p = _gqa_prep_call(proj, gqa_k_norm[li], 0, T_CTX, None, name="gqa_prep_ctx")
    k_s, v_s = _gqa_prep_call(proj, gqa_k_norm[li], T_CTX, T_LAT, tables, name="gqa_prep_lat")
    o_p = _gqa_attn_call(proj, gqa_q_norm[li], [(k_p, v_p, SEQ)], None, batch=BATCH, n_q=SEQ, row_off=0,
                         tq=SEQ, kv_per_step=GQA_KV_HEADS, name="gqa_attn_ctx")
    ck = cache_k[:, li].reshape(DEC_BATCH * PAST_LEN, nkv)
    cv = cache_v[:, li].reshape(DEC_BATCH * PAST_LEN, nkv)
    o_s = _gqa_attn_call(proj, gqa_q_norm[li], [(ck, cv, PAST_LEN), (k_s, v_s, DEC_SEQ)], tables,
                         batch=DEC_BATCH, n_q=DEC_SEQ, row_off=T_CTX, tq=256, kv_per_step=1, name="gqa_attn_lat")
    new_k = k_p.reshape(BATCH, SEQ, GQA_KV_HEADS, GQA_HEAD_DIM)
    new_v = v_p.reshape(BATCH, SEQ, GQA_KV_HEADS, GQA_HEAD_DIM)
    return (o_p, o_s), new_k, new_v


def _krope_kernel(x_ref, *rest, rope):
    if rope:
        cos_ref, sin_ref, o_ref = rest
    else:
        (o_ref,) = rest
    x = x_ref[...]
    x2 = jnp.concatenate([x, x], axis=1)
    if rope:
        x2 = _rope(x2, cos_ref[...], sin_ref[...], MLA_ROPE // 4)
    o_ref[...] = x2


def _krope_call(krope, tables, *, tm=512, name):
    rope = tables is not None
    n = krope.shape[0]
    in_specs = [pl.BlockSpec((tm, MLA_ROPE), lambda i: (i, 0))]
    args = [krope]
    if rope:
        per_seq = DEC_SEQ // tm
        in_specs += [pl.BlockSpec((tm, 2 * MLA_ROPE), lambda i: (i % per_seq, 0))] * 2
        args += list(tables)
    return pl.pallas_call(
        functools.partial(_krope_kernel, rope=rope),
        grid=(n // tm,),
        in_specs=in_specs,
        out_specs=pl.BlockSpec((tm, 2 * MLA_ROPE), lambda i: (i, 0)),
        out_shape=jax.ShapeDtypeStruct((n, 2 * MLA_ROPE), F32),
        compiler_params=_cparams("arbitrary"),
        name=name,
    )(*args)


def _mla_attn_kernel(qn_ref, qr_ref, *rest, n_seg, rope, heads_per_step):
    if rope:
        cos_ref, sin_ref = rest[:2]
        rest = rest[2:]
    seg_refs = rest[:2 * n_seg]
    o_ref = rest[2 * n_seg]
    scale = (MLA_NOPE + MLA_ROPE) ** -0.5
    h0 = pl.program_id(1) * heads_per_step
    pair_w = 2 * MLA_ROPE
    tq = qn_ref.shape[0]
    row_splits = 2 if heads_per_step == 1 else 1
    rows = tq // row_splits
    chains = []
    for j in range(heads_per_step):
        qn = qn_ref[:, j * MLA_NOPE:(j + 1) * MLA_NOPE]
        pair = qr_ref[:, (j // 2) * pair_w:(j // 2 + 1) * pair_w]
        if rope:
            pair = _rope(pair, cos_ref[...], sin_ref[...], MLA_ROPE // 4)
        lane = lax.broadcasted_iota(jnp.int32, pair.shape, 1)
        mine = (lane >> int(math.log2(MLA_ROPE))) == ((h0 + j) & 1)
        q = jnp.concatenate([qn, jnp.where(mine, pair, 0.0)], axis=1).astype(BF16)
        ks, vs = [], []
        for sgi in range(n_seg):
            kv = seg_refs[2 * sgi][:, j * (MLA_NOPE + MLA_V):(j + 1) * (MLA_NOPE + MLA_V)]
            ks.append(jnp.concatenate([kv[:, :MLA_NOPE], seg_refs[2 * sgi + 1][...]], axis=1).astype(BF16))
            vs.append(kv[:, MLA_NOPE:].astype(BF16))
        for c in range(row_splits):
            chains.append((q[c * rows:(c + 1) * rows], ks, vs))
    outs = _attend(chains, scale)
    for j in range(heads_per_step):
        for c in range(row_splits):
            o_ref[c * rows:(c + 1) * rows, j * MLA_V:(j + 1) * MLA_V] = outs[j * row_splits + c]


def _mla_attn_call(q, segs, tables, *, batch, n_q, row_off, tq, heads_per_step, name):
    rope = tables is not None
    hps = heads_per_step
    nqb = n_q // tq
    off = row_off // tq
    rope_w = max(2 * MLA_ROPE, MLA_ROPE * hps)
    rope_base = MLA_HEADS * MLA_NOPE // rope_w

    def rope_idx(b, h, i):
        return (b * nqb + i + off, rope_base + (h * hps * MLA_ROPE) // rope_w)

    in_specs = [pl.BlockSpec((tq, MLA_NOPE * hps), lambda b, h, i: (b * nqb + i + off, h)),
                pl.BlockSpec((tq, rope_w), rope_idx)]
    args = [q, q]
    if rope:
        in_specs += [pl.BlockSpec((tq, 2 * MLA_ROPE), lambda b, h, i: (i, 0))] * 2
        args += list(tables)
    for kv2d, kr2, rows, kv_row_off in segs:
        kv_off = kv_row_off // rows
        in_specs += [pl.BlockSpec((rows, (MLA_NOPE + MLA_V) * hps), lambda b, h, i, kv_off=kv_off: (b + kv_off, h)),
                     pl.BlockSpec((rows, 2 * MLA_ROPE), lambda b, h, i: (b, 0))]
        args += [kv2d, kr2]
    return pl.pallas_call(
        functools.partial(_mla_attn_kernel, n_seg=len(segs), rope=rope, heads_per_step=hps),
        grid=(batch, MLA_HEADS // hps, nqb),
        in_specs=in_specs,
        out_specs=pl.BlockSpec((tq, MLA_V * hps), lambda b, h, i: (b * nqb + i, h)),
        out_shape=jax.ShapeDtypeStruct((batch * n_q, MLA_HEADS * MLA_V), F32),
        compiler_params=_cparams("arbitrary", "arbitrary", "arbitrary"),
        name=name,
    )(*args)


def _mla_layer(x, mods_l, li, cache_ckv, cache_krope, mla_w_in, mla_q_norm, mla_kv_norm, mla_w_q_up, mla_w_kv_up):
    w_in = jnp.pad(mla_w_in[li], ((0, 0), (0, 7 * V7X_LANES - MLA_IN)))
    proj = _modproj_call(x, mods_l, 0, 1, w_in, 7 * V7X_LANES, tn=7 * V7X_LANES, name="mla_in")
    wq = mla_w_q_up[li].reshape(MLA_Q_LORA, MLA_HEADS, MLA_NOPE + MLA_ROPE)
    wq = jnp.concatenate([wq[:, :, :MLA_NOPE].reshape(MLA_Q_LORA, -1), wq[:, :, MLA_NOPE:].reshape(MLA_Q_LORA, -1)], axis=1)
    q = _rmsproj_call(proj, 0, MLA_Q_LORA, mla_q_norm[li], wq, name="mla_q_up")
    kv, ckv = _rmsproj_call(proj, MLA_Q_LORA // MLA_KV_LORA, MLA_KV_LORA, mla_kv_norm[li], mla_w_kv_up[li],
                            emit_xn=True, name="mla_kv_up")
    krope = proj[:, MLA_Q_LORA + MLA_KV_LORA:MLA_IN]
    tables = _rope_tables(DEC_SEQ, MLA_ROPE, 2)
    kr2_p = _krope_call(krope[:T_CTX], None, name="mla_krope_ctx")
    kr2_s = _krope_call(krope[T_CTX:], tables, name="mla_krope_lat")
    ctx_ckv = cache_ckv[:, li].reshape(DEC_BATCH * PAST_LEN, MLA_KV_LORA)
    kv_c = _rmsproj_call(ctx_ckv, 0, MLA_KV_LORA, jnp.ones((MLA_KV_LORA,), F32), mla_w_kv_up[li], normalize=False,
                         name="mla_kv_up_cache")
    kr2_c = _krope_call(cache_krope[:, li].reshape(DEC_BATCH * PAST_LEN, MLA_ROPE), None, name="mla_krope_cache")
    o_p = _mla_attn_call(q, [(kv, kr2_p, SEQ, 0)], None, batch=BATCH, n_q=SEQ, row_off=0, tq=SEQ,
                         heads_per_step=MLA_HEADS, name="mla_attn_ctx")
    o_s = _mla_attn_call(q, [(kv_c, kr2_c, PAST_LEN, 0), (kv, kr2_s, DEC_SEQ, T_CTX)], tables, batch=DEC_BATCH,
                         n_q=DEC_SEQ, row_off=T_CTX, tq=512, heads_per_step=1, name="mla_attn_lat")
    new_ckv = ckv[:T_CTX].reshape(BATCH, SEQ, MLA_KV_LORA)
    new_krope = krope[:T_CTX].reshape(BATCH, SEQ, MLA_ROPE)
    return (o_p, o_s), new_ckv, new_krope


def _router_kernel(x_ref, sh_ref, sc_ref, rw_ref, rb_ref, o_ref, *, tiles_per_seg):
    i = pl.program_id(0)
    r = _seg_row(i, tiles_per_seg)
    h = x_ref[...] * (1.0 + sc_ref[pl.ds(r, 1), :]) + sh_ref[pl.ds(r, 1), :]
    logits = lax.dot_general(rw_ref[...], h, (((1,), (1,)), ((), ())), precision=lax.Precision.HIGHEST,
                             preferred_element_type=F32)[:N_EXPERTS]
    scores = _sigmoid(logits)
    sel = scores + rb_ref[0:N_EXPERTS, :]
    tm = sel.shape[1]
    neg = -jnp.inf
    gs = []
    for g in range(N_GROUPS):
        a, b, c, d = (sel[EXPERTS_PER_GROUP * g + j:EXPERTS_PER_GROUP * g + j + 1, :] for j in range(4))
        gs.append(jnp.maximum(jnp.maximum(jnp.maximum(a + b, a + c), jnp.maximum(a + d, b + c)),
                              jnp.maximum(b + d, c + d)))
    gmax = jnp.maximum(jnp.maximum(gs[0], gs[1]), jnp.maximum(gs[2], gs[3]))
    best = jnp.where(gs[0] == gmax, 0, jnp.where(gs[1] == gmax, 1, jnp.where(gs[2] == gmax, 2, 3)))
    eid = lax.broadcasted_iota(jnp.int32, (N_EXPERTS, tm), 0)
    masked = jnp.where((eid >> int(math.log2(EXPERTS_PER_GROUP))) == best, sel, neg)
    m1 = jnp.max(masked, axis=0, keepdims=True)
    i1 = jnp.min(jnp.where(masked == m1, eid, N_EXPERTS), axis=0, keepdims=True)
    masked2 = jnp.where(eid == i1, neg, masked)
    m2 = jnp.max(masked2, axis=0, keepdims=True)
    i2 = jnp.min(jnp.where(masked2 == m2, eid, N_EXPERTS), axis=0, keepdims=True)
    w1 = jnp.sum(jnp.where(eid == i1, scores, 0.0), axis=0, keepdims=True)
    w2 = jnp.sum(jnp.where(eid == i2, scores, 0.0), axis=0, keepdims=True)
    tot = w1 + w2
    comb = jnp.where(eid == i1, w1 / tot, 0.0) + jnp.where(eid == i2, w2 / tot, 0.0)
    comb = jnp.concatenate([comb, jnp.zeros((V7X_LANES - N_EXPERTS, tm), F32)], axis=0)
    o_ref[...] = comb.T


def _router_call(x, mods_l, router_w, router_b, *, tm=512):
    t = T_ALL
    rw = jnp.pad(router_w.T, ((0, V7X_LANES - N_EXPERTS), (0, 0)))
    rb = jnp.pad(router_b.reshape(N_EXPERTS, 1), ((0, V7X_LANES - N_EXPERTS), (0, 0)))
    return pl.pallas_call(
        functools.partial(_router_kernel, tiles_per_seg=SEG_ROWS // tm),
        grid=(t // tm,),
        in_specs=[
            pl.BlockSpec((tm, D_MODEL), lambda i: (i, 0)),
            pl.BlockSpec((MOD_ROWS, D_MODEL), lambda i: (0, 3)),
            pl.BlockSpec((MOD_ROWS, D_MODEL), lambda i: (0, 4)),
            pl.BlockSpec((V7X_LANES, D_MODEL), lambda i: (0, 0)),
            pl.BlockSpec((V7X_LANES, 1), lambda i: (0, 0)),
        ],
        out_specs=pl.BlockSpec((tm, V7X_LANES), lambda i: (i, 0)),
        out_shape=jax.ShapeDtypeStruct((t, V7X_LANES), F32),
        compiler_params=_cparams("arbitrary"),
        name="moe_router",
    )(x, mods_l, mods_l, rw, rb)


def _moe_kernel(x_ref, sh_ref, sc_ref, gate_ref, comb_ref, wg_ref, wu_ref, wd_ref, g_ref, b_ref, *rest,
                tiles_per_seg, ctx_tiles, split_out):
    if split_out:
        oc_ref, ol_ref, h_scr, acc_scr = rest
    else:
        o_ref, h_scr, acc_scr = rest
    i, e = pl.program_id(0), pl.program_id(1)
    r = _seg_row(i, tiles_per_seg)

    @pl.when(e == 0)
    def _():
        h_scr[...] = (x_ref[...] * (1.0 + sc_ref[pl.ds(r, 1), :]) + sh_ref[pl.ds(r, 1), :]).astype(BF16)
        acc_scr[...] = jnp.zeros_like(acc_scr)

    comb = comb_ref[...]
    lane = lax.broadcasted_iota(jnp.int32, comb.shape, 1)
    h = h_scr[...]
    epb = wg_ref.shape[0]
    f = None
    for j in range(epb):
        c_e = jnp.sum(jnp.where(lane == e * epb + j, comb, 0.0), axis=1, keepdims=True)
        hid = _silu(_mm(h, wg_ref[j])) * _mm(h, wu_ref[j]) * c_e
        fj = _mm(hid, wd_ref[j])
        f = fj if f is None else f + fj
    acc_scr[...] += f

    def result():
        gate = gate_ref[pl.ds(r, 1), :]
        return _layer_norm(ALPHA * x_ref[...] + (1.0 + gate) * acc_scr[...], g_ref[...], b_ref[...])

    last = e == pl.num_programs(1) - 1
    if split_out:
        @pl.when(last & (i < ctx_tiles))
        def _():
            oc_ref[...] = result()

        @pl.when(last & (i >= ctx_tiles))
        def _():
            ol_ref[...] = result()
    else:
        @pl.when(last)
        def _():
            o_ref[...] = result()


MOE_EXPERTS_PER_STEP = 2


def _moe_call(x, mods_l, comb, w_gate, w_up, w_down, layer, g, b, *, split_out, tm=1024):
    t = T_ALL
    epb = MOE_EXPERTS_PER_STEP
    ctx_tiles = T_CTX // tm
    if split_out:
        out_specs = [pl.BlockSpec((tm, D_MODEL), lambda i, e: (jnp.minimum(i, ctx_tiles - 1), 0)),
                     pl.BlockSpec((tm, D_MODEL), lambda i, e: (jnp.maximum(i - ctx_tiles, 0), 0))]
        out_shape = [jax.ShapeDtypeStruct((T_CTX, D_MODEL), F32), jax.ShapeDtypeStruct((T_LAT, D_MODEL), F32)]
    else:
        out_specs = pl.BlockSpec((tm, D_MODEL), lambda i, e: (i, 0))
        out_shape = jax.ShapeDtypeStruct((t, D_MODEL), F32)
    return pl.pallas_call(
        functools.partial(_moe_kernel, tiles_per_seg=SEG_ROWS // tm, ctx_tiles=ctx_tiles, split_out=split_out),
        grid=(t // tm, N_EXPERTS // epb),
        in_specs=[
            pl.BlockSpec((tm, D_MODEL), lambda i, e: (i, 0)),
            pl.BlockSpec((MOD_ROWS, D_MODEL), lambda i, e: (0, 3)),
            pl.BlockSpec((MOD_ROWS, D_MODEL), lambda i, e: (0, 4)),
            pl.BlockSpec((MOD_ROWS, D_MODEL), lambda i, e: (0, 5)),
            pl.BlockSpec((tm, V7X_LANES), lambda i, e: (i, 0)),
            pl.BlockSpec((None, epb, D_MODEL, D_EXPERT), lambda i, e: (layer, e, 0, 0)),
            pl.BlockSpec((None, epb, D_MODEL, D_EXPERT), lambda i, e: (layer, e, 0, 0)),
            pl.BlockSpec((None, epb, D_EXPERT, D_MODEL), lambda i, e: (layer, e, 0, 0)),
            pl.BlockSpec((1, D_MODEL), lambda i, e: (0, 0)),
            pl.BlockSpec((1, D_MODEL), lambda i, e: (0, 0)),
        ],
        out_specs=out_specs,
        out_shape=out_shape,
        scratch_shapes=[pltpu.VMEM((tm, D_MODEL), BF16), pltpu.VMEM((tm, D_MODEL), F32)],
        compiler_params=_cparams("arbitrary", "arbitrary"),
        name="moe_experts",
    )(x, mods_l, mods_l, mods_l, comb, w_gate, w_up, w_down, g.reshape(1, D_MODEL), b.reshape(1, D_MODEL))


def kernel(x_prompt, x_sample, state_gdn, cache_gqa_k, cache_gqa_v, cache_mla_ckv, cache_mla_krope, c, c_ctx, w_ada, b_ada, ln_mix_g, ln_mix_b, ln_ffn_g, ln_ffn_b, gdn_w_in, gdn_conv, gdn_a_log, gdn_dt_bias, gdn_norm_g, gdn_w_out, gqa_w_in, gqa_q_norm, gqa_k_norm, gqa_w_out, mla_w_in, mla_q_norm, mla_kv_norm, mla_w_q_up, mla_w_kv_up, mla_w_out, router_w, router_b, moe_w_gate, moe_w_up, moe_w_down):
    x = jnp.concatenate([x_prompt.reshape(T_CTX, D_MODEL), x_sample.reshape(T_LAT, D_MODEL)], axis=0)
    cond =jnp.concatenate([c_ctx[None], c, jnp.zeros((MOD_ROWS - 1 - DEC_BATCH, D_MODEL), F32)], axis=0)
    mods = _ada_call(cond, w_ada, b_ada)

    st_gdn, st_k, st_v, st_ckv, st_krope = [], [], [], [], []
    for l in range(DEPTH):
        kind, li = l % N_MIXERS, l // N_MIXERS
        mods_l = mods[l]
        if kind == 0:
            o, st = _gdn_layer(x, mods_l, li, state_gdn, gdn_w_in, gdn_conv, gdn_a_log, gdn_dt_bias, gdn_norm_g)
            st_gdn.append(st)
            w_out = gdn_w_out
        elif kind == 1:
            o, nk, nv = _gqa_layer(x, mods_l, li, cache_gqa_k, cache_gqa_v, gqa_w_in, gqa_q_norm, gqa_k_norm)
            st_k.append(nk)
            st_v.append(nv)
            w_out = gqa_w_out
        else:
            o, nckv, nkr = _mla_layer(x, mods_l, li, cache_mla_ckv, cache_mla_krope, mla_w_in, mla_q_norm,
                                      mla_kv_norm, mla_w_q_up, mla_w_kv_up)
            st_ckv.append(nckv)
            st_krope.append(nkr)
            w_out = mla_w_out
        x = _outln_call(o[0], o[1], w_out, li, x, mods_l, 2, ln_mix_g[l], ln_mix_b[l], name="mix_out_ln")
        comb = _router_call(x, mods_l, router_w, router_b)
        x = _moe_call(x, mods_l, comb, moe_w_gate, moe_w_up, moe_w_down, l, ln_ffn_g[l], ln_ffn_b[l],
                      split_out=(l == DEPTH - 1))

    y_prompt = x[0].reshape(BATCH, SEQ, D_MODEL)
    y_sample = x[1].reshape(DEC_BATCH, DEC_SEQ, D_MODEL)
    return (y_prompt, y_sample, jnp.stack(st_gdn, axis=1), jnp.stack(st_k, axis=1), jnp.stack(st_v, axis=1),
            jnp.stack(st_ckv, axis=1), jnp.stack(st_krope, axis=1))
```

```python
import functools
import math

import jax
import jax.numpy as jnp
from jax import lax
from jax.experimental import pallas as pl
from jax.experimental.pallas import tpu as pltpu

F32 = jnp.float32
BF16 = jnp.bfloat16

D_MODEL = 1024
BATCH = 32
SEQ = 256
DEPTH = 4
DEC_BATCH = 2
DEC_SEQ = 2048
PAST_LEN = 512
GRID_W = 64
N_MIXERS = 3

GDN_HEADS = 8
GDN_DK = 128
GDN_DV = 128
GDN_CONV = 5
GDN_CHUNK = 64
GDN_QKV = GDN_HEADS * (2 * GDN_DK + GDN_DV)
GDN_QKVZ = GDN_QKV + GDN_HEADS * GDN_DV

GQA_HEADS = 8
GQA_KV_HEADS = 2
GQA_GROUP = GQA_HEADS // GQA_KV_HEADS
GQA_HEAD_DIM = 128
GQA_IN = (GQA_HEADS + 2 * GQA_KV_HEADS) * GQA_HEAD_DIM

MLA_HEADS = 8
MLA_Q_LORA = 512
MLA_KV_LORA = 256
MLA_NOPE = 128
MLA_ROPE = 64
MLA_V = 128
MLA_IN = MLA_Q_LORA + MLA_KV_LORA + MLA_ROPE

N_EXPERTS = 16
N_GROUPS = 4
EXPERTS_PER_GROUP = N_EXPERTS // N_GROUPS
D_EXPERT = 256

ROPE_THETA = 10000.0
ALPHA = (2 * DEPTH) ** 0.25
NORM_EPS = 1e-6

T_CTX = BATCH * SEQ
T_LAT = DEC_BATCH * DEC_SEQ
T_ALL = T_CTX + T_LAT
SEG_ROWS = DEC_SEQ
N_CTX_SEGS = T_CTX // SEG_ROWS
MOD_ROWS = 8

V7X_LANES = 128
V7X_SUBLANES = 8
V7X_VMEM_BYTES = 64 * 1024 * 1024
VMEM_LIMIT = V7X_VMEM_BYTES * 7 // 8

GDN_SUPER = 256
CHUNKS_PER_SUPER = GDN_SUPER // GDN_CHUNK


def _cparams(*sem):
    return pltpu.CompilerParams(dimension_semantics=sem, vmem_limit_bytes=VMEM_LIMIT)


def _mm(a, b):
    return jnp.dot(a.astype(BF16), b.astype(BF16), preferred_element_type=F32)


def _mm_nt(a, b):
    return lax.dot_general(a.astype(BF16), b.astype(BF16), (((1,), (1,)), ((), ())),
                           preferred_element_type=F32)


def _silu(x):
    return x * (1.0 / (1.0 + jnp.exp(-x)))


def _sigmoid(x):
    return 1.0 / (1.0 + jnp.exp(-x))


def _chunk_of(idx):
    return idx >> int(math.log2(GDN_CHUNK))


def _seg_row(i, tiles_per_seg):
    return jnp.maximum(lax.div(i, tiles_per_seg) - (N_CTX_SEGS - 1), 0)


def _layer_norm(y, g, b):
    mu = jnp.mean(y, axis=-1, keepdims=True)
    yc = y - mu
    var = jnp.mean(yc * yc, axis=-1, keepdims=True)
    return yc * lax.rsqrt(var + NORM_EPS) * g + b


def _rms(x, g):
    return x * lax.rsqrt(jnp.mean(x * x, axis=-1, keepdims=True) + NORM_EPS) * g


def _slab_specs(tm, n_grid_axes, **mode):
    ctx_tiles = T_CTX // tm
    if n_grid_axes == 1:
        return [pl.BlockSpec((tm, D_MODEL), lambda i: (jnp.minimum(i, ctx_tiles - 1), 0), **mode),
                pl.BlockSpec((tm, D_MODEL), lambda i: (jnp.maximum(i - ctx_tiles, 0), 0), **mode)]
    return [pl.BlockSpec((tm, D_MODEL), lambda i, j: (jnp.minimum(i, ctx_tiles - 1), 0), **mode),
            pl.BlockSpec((tm, D_MODEL), lambda i, j: (jnp.maximum(i - ctx_tiles, 0), 0), **mode)]


def _slab_shapes():
    return [jax.ShapeDtypeStruct((T_CTX, D_MODEL), F32), jax.ShapeDtypeStruct((T_LAT, D_MODEL), F32)]


def _slab_read(i, xc_ref, xl_ref):
    return jnp.where(i < T_CTX // xc_ref.shape[0], xc_ref[...], xl_ref[...])


def _slab_write(i, when, yc_ref, yl_ref, value_fn):
    ctx_tiles = T_CTX // yc_ref.shape[0]

    @pl.when(when & (i < ctx_tiles))
    def _():
        yc_ref[...] = value_fn()

    @pl.when(when & (i >= ctx_tiles))
    def _():
        yl_ref[...] = value_fn()


def _ada_kernel(c_ref, w_ref, b_ref, o_ref):
    o_ref[...] = _mm(_silu(c_ref[...]), w_ref[...]) + b_ref[...]


def _ada_call(cond, w_ada, b_ada):
    tn = 1536
    n = w_ada.shape[-1]
    return pl.pallas_call(
        _ada_kernel,
        grid=(DEPTH, n // tn),
        in_specs=[
            pl.BlockSpec((MOD_ROWS, D_MODEL), lambda l, j: (0, 0)),
            pl.BlockSpec((None, D_MODEL, tn), lambda l, j: (l, 0, j)),
            pl.BlockSpec((None, 1, tn), lambda l, j: (l, 0, j)),
        ],
        out_specs=pl.BlockSpec((None, MOD_ROWS, tn), lambda l, j: (l, 0, j)),
        out_shape=jax.ShapeDtypeStruct((DEPTH, MOD_ROWS, n), F32),
        compiler_params=_cparams("arbitrary", "arbitrary"),
        name="ada_mod",
    )(cond, w_ada, b_ada.reshape(DEPTH, 1, n))


def _modproj_kernel(xc_ref, xl_ref, sh_ref, sc_ref, w_ref, *rest, tiles_per_seg, has_extra):
    if has_extra:
        wx_ref, o_ref, ox_ref, h_scr, wb_scr = rest
    else:
        o_ref, h_scr, wb_scr = rest
    i, j = pl.program_id(0), pl.program_id(1)

    @pl.when(i == 0)
    def _():
        wb_scr[j] = w_ref[...].astype(BF16)

    @pl.when(j == 0)
    def _():
        r = _seg_row(i, tiles_per_seg)
        sh = sh_ref[pl.ds(r, 1), :]
        sc = sc_ref[pl.ds(r, 1), :]
        h = (_slab_read(i, xc_ref, xl_ref) * (1.0 + sc) + sh).astype(BF16)
        h_scr[...] = h
        if has_extra:
            ox_ref[...] = _mm(h, wx_ref[...])

    o_ref[...] = jnp.dot(h_scr[...], wb_scr[j], preferred_element_type=F32).astype(o_ref.dtype)


def _modproj_call(x, mods_l, shift_idx, scale_idx, w, n_out, w_extra=None, *, w_layer=None, tm=1024, tn=512,
                  out_dtype=F32, name):
    t = T_ALL
    has_extra = w_extra is not None
    nj = n_out // tn
    if w_layer is None:
        w_spec = pl.BlockSpec((D_MODEL, tn), lambda i, j: (0, jnp.where(i == 0, j, nj - 1)))
    else:
        w_spec = pl.BlockSpec((None, D_MODEL, tn), lambda i, j: (w_layer, 0, jnp.where(i == 0, j, nj - 1)))
    in_specs = _slab_specs(tm, 2) + [
        pl.BlockSpec((MOD_ROWS, D_MODEL), lambda i, j: (0, shift_idx)),
        pl.BlockSpec((MOD_ROWS, D_MODEL), lambda i, j: (0, scale_idx)),
        w_spec,
    ]
    out_specs = [pl.BlockSpec((tm, tn), lambda i, j: (i, j))]
    out_shape = [jax.ShapeDtypeStruct((t, n_out), out_dtype)]
    args = [x[0], x[1], mods_l, mods_l, w]
    if has_extra:
        nx = w_extra.shape[1]
        in_specs.append(pl.BlockSpec((D_MODEL, nx), lambda i, j: (0, 0)))
        out_specs.append(pl.BlockSpec((tm, nx), lambda i, j: (i, 0)))
        out_shape.append(jax.ShapeDtypeStruct((t, nx), F32))
        args.append(w_extra)
    res = pl.pallas_call(
        functools.partial(_modproj_kernel, tiles_per_seg=SEG_ROWS // tm, has_extra=has_extra),
        grid=(t // tm, nj),
        in_specs=in_specs,
        out_specs=out_specs,
        out_shape=out_shape,
        scratch_shapes=[pltpu.VMEM((tm, D_MODEL), BF16), pltpu.VMEM((nj, D_MODEL, tn), BF16)],
        compiler_params=_cparams("arbitrary", "arbitrary"),
        name=name,
    )(*args)
    return res if has_extra else res[0]


def _rmsproj_kernel(x_ref, g_ref, w_ref, *rest, normalize, emit_xn):
    if emit_xn:
        o_ref, xn_ref, h_scr = rest
    else:
        o_ref, h_scr = rest
    j = pl.program_id(1)

    @pl.when(j == 0)
    def _():
        x = x_ref[...]
        xn = _rms(x, g_ref[...]) if normalize else x
        h_scr[...] = xn.astype(BF16)
        if emit_xn:
            xn_ref[...] = xn

    o_ref[...] = _mm(h_scr[...], w_ref[...])


def _rmsproj_call(x, col_block, k, g, w, *, normalize=True, emit_xn=False, tm=1024, tn=512, name):
    t = x.shape[0]
    n_out = w.shape[1]
    out_specs = [pl.BlockSpec((tm, tn), lambda i, j: (i, j))]
    out_shape = [jax.ShapeDtypeStruct((t, n_out), F32)]
    if emit_xn:
        out_specs.append(pl.BlockSpec((tm, k), lambda i, j: (i, 0)))
        out_shape.append(jax.ShapeDtypeStruct((t, k), F32))
    res = pl.pallas_call(
        functools.partial(_rmsproj_kernel, normalize=normalize, emit_xn=emit_xn),
        grid=(t // tm, n_out // tn),
        in_specs=[
            pl.BlockSpec((tm, k), lambda i, j: (i, col_block)),
            pl.BlockSpec((1, k), lambda i, j: (0, 0)),
            pl.BlockSpec((k, tn), lambda i, j: (0, j)),
        ],
        out_specs=out_specs,
        out_shape=out_shape,
        scratch_shapes=[pltpu.VMEM((tm, k), BF16)],
        compiler_params=_cparams("arbitrary", "arbitrary"),
        name=name,
    )(x, g.reshape(1, k), w)
    return res if emit_xn else res[0]


def _outln_kernel(oc_ref, ol_ref, w_ref, xc_ref, xl_ref, gate_ref, sh_ref, sc_ref, g_ref, b_ref, rw_ref, rb_ref,
                  yc_ref, yl_ref, comb_ref, wb_scr, *, tiles_per_seg):
    i = pl.program_id(0)

    @pl.when(i == 0)
    def _():
        wb_scr[...] = w_ref[...].astype(BF16)

    r = _seg_row(i, tiles_per_seg)
    gate = gate_ref[pl.ds(r, 1), :]
    f = jnp.dot(_slab_read(i, oc_ref, ol_ref).astype(BF16), wb_scr[...], preferred_element_type=F32)
    y = _layer_norm(ALPHA * _slab_read(i, xc_ref, xl_ref) + (1.0 + gate) * f, g_ref[...], b_ref[...])
    _slab_write(i, True, yc_ref, yl_ref, lambda: y)
    comb_ref[...] = _route(y * (1.0 + sc_ref[pl.ds(r, 1), :]) + sh_ref[pl.ds(r, 1), :], rw_ref, rb_ref)


def _outln_call(o, w, w_layer, x, mods_l, g, b, router_w, router_b, *, tm=512, name):
    k = o[0].shape[1]
    t = T_ALL
    rw = jnp.pad(router_w.T, ((0, V7X_LANES - N_EXPERTS), (0, 0)))
    rb = jnp.pad(router_b.reshape(N_EXPERTS, 1), ((0, V7X_LANES - N_EXPERTS), (0, 0)))
    o_specs = [pl.BlockSpec((tm, k), s.index_map) for s in _slab_specs(tm, 1)]

    def mod(idx):
        return pl.BlockSpec((MOD_ROWS, D_MODEL), lambda i: (0, idx))

    yc, yl, comb = pl.pallas_call(
        functools.partial(_outln_kernel, tiles_per_seg=SEG_ROWS // tm),
        grid=(t // tm,),
        in_specs=o_specs + [pl.BlockSpec((None, k, D_MODEL), lambda i: (w_layer, 0, 0))] + _slab_specs(tm, 1) + [
            mod(2), mod(3), mod(4),
            pl.BlockSpec((1, D_MODEL), lambda i: (0, 0)),
            pl.BlockSpec((1, D_MODEL), lambda i: (0, 0)),
            pl.BlockSpec((V7X_LANES, D_MODEL), lambda i: (0, 0)),
            pl.BlockSpec((V7X_LANES, 1), lambda i: (0, 0)),
        ],
        out_specs=_slab_specs(tm, 1) + [pl.BlockSpec((tm, V7X_LANES), lambda i: (i, 0))],
        out_shape=_slab_shapes() + [jax.ShapeDtypeStruct((t, V7X_LANES), F32)],
        scratch_shapes=[pltpu.VMEM((k, D_MODEL), BF16)],
        compiler_params=_cparams("arbitrary"),
        name=name,
    )(o[0], o[1], w, x[0], x[1], mods_l, mods_l, mods_l, g.reshape(1, D_MODEL), b.reshape(1, D_MODEL), rw, rb)
    return (yc, yl), comb


def _gdn_gates_kernel(ab_ref, alog_ref, dt_ref, o_ref):
    ab = ab_ref[...]
    tm = ab.shape[0]
    x = ab + dt_ref[...]
    softplus = jnp.maximum(x, 0.0) + jnp.log(1.0 + jnp.exp(-jnp.abs(x)))
    g = -jnp.exp(alog_ref[...]) * softplus
    ri = lax.broadcasted_iota(jnp.int32, (tm, tm), 0)
    ci = lax.broadcasted_iota(jnp.int32, (tm, tm), 1)
    same = _chunk_of(ri) == _chunk_of(ci)
    lower = jnp.where(same & (ri >= ci), 1.0, 0.0).astype(F32)
    upper = jnp.where(same & (ri <= ci), 1.0, 0.0).astype(F32)
    pre = jnp.dot(lower, g, precision=lax.Precision.HIGHEST, preferred_element_type=F32)
    suf = jnp.dot(upper, g, precision=lax.Precision.HIGHEST, preferred_element_type=F32)
    lane = lax.broadcasted_iota(jnp.int32, ab.shape, 1)
    o_ref[...] = jnp.where(lane < GDN_HEADS, pre, jnp.where(lane < 2 * GDN_HEADS, suf, _sigmoid(ab)))


def _gdn_gates_call(ab, a_log, dt_bias):
    t, n = ab.shape
    tm = GDN_SUPER
    pad = n - 2 * GDN_HEADS
    alog_row = jnp.pad(a_log.reshape(1, -1), ((0, 0), (0, pad)))
    dt_row = jnp.pad(dt_bias.reshape(1, -1), ((0, 0), (0, pad)))
    return pl.pallas_call(
        _gdn_gates_kernel,
        grid=(t // tm,),
        in_specs=[
            pl.BlockSpec((tm, n), lambda i: (i, 0)),
            pl.BlockSpec((1, n), lambda i: (0, 0)),
            pl.BlockSpec((1, n), lambda i: (0, 0)),
        ],
        out_specs=pl.BlockSpec((tm, n), lambda i: (i, 0)),
        out_shape=jax.ShapeDtypeStruct((t, n), F32),
        compiler_params=_cparams("arbitrary"),
        name="gdn_gates",
    )(ab, alog_row, dt_row)


def _loop(n, body, max_static=4):
    if n <= max_static:
        for i in range(n):
            body(i)
    else:
        def wrapped(i, carry):
            body(i)
            return carry
        lax.fori_loop(0, n, wrapped, 0)


def _gdn_core_kernel(q_ref, k_ref, v_ref, z_ref, cq_ref, ck_ref, cv_ref, gc_ref, gr_ref, ng_ref, *rest,
                     n_tok, has_s0, hps):
    if has_s0:
        s0_ref, o_ref, st_ref = rest[:3]
        scr = rest[3:]
    else:
        o_ref, st_ref = rest[:2]
        scr = rest[2:]
    pad_scr, kn_scr, u_scr, w_scr, qe_scr, qk_scr, oo_scr, s_scr = scr
    n_super = n_tok // GDN_SUPER
    n_chunk = n_tok // GDN_CHUNK
    halo = V7X_SUBLANES
    sb = GDN_SUPER
    ck = GDN_CHUNK
    dk = GDN_DK

    zeros_halo = jnp.zeros((halo, dk), F32)
    for hd in range(hps):
        for a, src in enumerate((q_ref, k_ref, v_ref)):
            pad_scr[3 * hd + a, 0:halo, :] = zeros_halo
            pad_scr[3 * hd + a, halo:halo + n_tok, :] = src[:, hd * dk:(hd + 1) * dk].astype(F32)
            pad_scr[3 * hd + a, halo + n_tok:2 * halo + n_tok, :] = zeros_halo
        for d in range(2):
            s_scr[2 * hd + d] = s0_ref[d, hd] if has_s0 else jnp.zeros((dk, GDN_DV), F32)

    ri = lax.broadcasted_iota(jnp.int32, (sb, sb), 0)
    ci = lax.broadcasted_iota(jnp.int32, (sb, sb), 1)
    same_chunk = _chunk_of(ri) == _chunk_of(ci)
    eye = jnp.where(ri == ci, 1.0, 0.0).astype(F32)
    n_levels = int(math.log2(ck))

    def level_mask(l):
        return ((ri >> (l + 1)) == (ci >> (l + 1))) & ((ri >> l) != (ci >> l))

    def conv_silu(a, w_ref, hd, r0):
        xp = pad_scr[a, pl.ds(r0, sb + 2 * halo), :]
        acc = None
        for j in range(GDN_CONV):
            shift = (GDN_CONV // 2 - j) % (sb + 2 * halo)
            tap = xp if shift == 0 else pltpu.roll(xp, shift, 0)
            term = w_ref[j:j + 1, hd * dk:(hd + 1) * dk] * tap[halo:halo + sb, :]
            acc = term if acc is None else acc + term
        return _silu(acc)

    chains = [(hd, d) for hd in range(hps) for d in range(2)]

    a_unroll = max(1, min(n_super, 8 // (2 * hps)))

    def phase_a(it):
        units = [(it * a_unroll + u, hd) for u in range(a_unroll) for hd in range(hps)]
        r0s = [pl.multiple_of(s * sb, sb) for s, _ in units]
        qs, ks, vs, ps, qks = [], [], [], [], []
        for (s, hd), r0 in zip(units, r0s):
            q = conv_silu(3 * hd, cq_ref, hd, r0)
            k = conv_silu(3 * hd + 1, ck_ref, hd, r0)
            v = conv_silu(3 * hd + 2, cv_ref, hd, r0)
            q = q * lax.rsqrt(jnp.sum(q * q, axis=-1, keepdims=True) + NORM_EPS) * (dk ** -0.5)
            k = k * lax.rsqrt(jnp.sum(k * k, axis=-1, keepdims=True) + NORM_EPS)
            kn_scr[hd, pl.ds(r0, sb), :] = k
            kb = k.astype(BF16)
            qs.append(q)
            ks.append(k)
            vs.append(v)
            ps.append(_mm_nt(kb, kb))
            qks.append(_mm_nt(q, kb))
        a_chains = [(n, d) for n in range(len(units)) for d in range(2)]
        a_mats, ts, rhs = [], [], []
        for n, d in a_chains:
            (s, hd), r0 = units[n], r0s[n]
            hdd = 2 * hd + d
            gcol = gc_ref[hd, pl.ds(r0, sb), d:d + 1]
            bcol = gc_ref[hd, pl.ds(r0, sb), 2 + d:3 + d]
            grow = gr_ref[hd, s, d:d + 1, :]
            if d == 0:
                mask, strict = same_chunk & (ri >= ci), same_chunk & (ri > ci)
            else:
                mask, strict = same_chunk & (ri <= ci), same_chunk & (ri < ci)
            dec = jnp.exp(jnp.where(mask, gcol - grow, -jnp.inf))
            a = jnp.where(strict, ps[n] * bcol * dec, 0.0)
            a_mats.append(a)
            ts.append(eye - jnp.where(level_mask(0), a, 0.0))
            eg = jnp.exp(gcol)
            rhs.append(jnp.concatenate([vs[n] * bcol, ks[n] * (bcol * eg)], axis=1).astype(BF16))
            qe_scr[hdd, pl.ds(r0, sb), :] = qs[n] * eg
            qkd = qks[n] * dec
            for c in range(CHUNKS_PER_SUPER):
                qk_scr[hdd, pl.ds(r0 + c * ck, ck), :] = qkd[c * ck:(c + 1) * ck, c * ck:(c + 1) * ck]
        tbs = [t.astype(BF16) for t in ts]
        for l in range(1, n_levels):
            lm = level_mask(l)
            ys = [_mm(jnp.where(lm, a_mats[i], 0.0), tbs[i]) for i in range(len(a_chains))]
            ts = [ts[i] - _mm(tbs[i], ys[i]) for i in range(len(a_chains))]
            tbs = [t.astype(BF16) for t in ts]
        for i, (n, d) in enumerate(a_chains):
            (s, hd), r0 = units[n], r0s[n]
            uw = _mm(tbs[i], rhs[i])
            u_scr[2 * hd + d, pl.ds(r0, sb), :] = uw[:, :GDN_DV]
            w_scr[2 * hd + d, pl.ds(r0, sb), :] = uw[:, GDN_DV:]

    _loop(n_super // a_unroll, phase_a, max_static=1)

    def phase_b(i):
        rows, ws_qs, v_new, kws, gls = [], [], [], [], []
        for hd, d in chains:
            hdd = 2 * hd + d
            c = i if d == 0 else n_chunk - 1 - i
            r = pl.multiple_of(c * ck, ck)
            r_last = r + (ck - 1) if d == 0 else r
            rows.append(r)
            gcol = gc_ref[hd, pl.ds(r, ck), d:d + 1]
            gl = gc_ref[hd, pl.ds(r_last, 1), d:d + 1]
            gls.append(gl)
            kws.append((kn_scr[hd, pl.ds(r, ck), :] * jnp.exp(gl - gcol)).T)
            wq = jnp.concatenate([w_scr[hdd, pl.ds(r, ck), :], qe_scr[hdd, pl.ds(r, ck), :]], axis=0)
            ws_qs.append(_mm(wq, s_scr[hdd]))
        for j, (hd, d) in enumerate(chains):
            hdd = 2 * hd + d
            vn = u_scr[hdd, pl.ds(rows[j], ck), :] - ws_qs[j][:ck]
            v_new.append(vn)
            oo_scr[hdd, pl.ds(rows[j], ck), :] = ws_qs[j][ck:] + _mm(qk_scr[hdd, pl.ds(rows[j], ck), :], vn)
        for j, (hd, d) in enumerate(chains):
            hdd = 2 * hd + d
            s_scr[hdd] = s_scr[hdd] * jnp.exp(gls[j]) + _mm(kws[j], v_new[j])

    _loop(n_chunk, phase_b)

    def finish(s):
        r0 = pl.multiple_of(s * sb, sb)
        for hd in range(hps):
            o = oo_scr[2 * hd, pl.ds(r0, sb), :] + oo_scr[2 * hd + 1, pl.ds(r0, sb), :]
            z = z_ref[pl.ds(r0, sb), hd * GDN_DV:(hd + 1) * GDN_DV].astype(F32)
            o_ref[pl.ds(r0, sb), hd * GDN_DV:(hd + 1) * GDN_DV] = _rms(o, ng_ref[...]) * _silu(z)

    _loop(n_super, finish)
    for hd in range(hps):
        for d in range(2):
            st_ref[d, hd] = s_scr[2 * hd + d]


def _gdn_core_call(qkvz, conv_w, gc, gr, norm_g, s0, *, batch, n_tok, row_block_off, hps, name):
    has_s0 = s0 is not None
    h = GDN_HEADS
    dk = GDN_DK
    n_super = n_tok // GDN_SUPER
    hb = h // hps

    mode = dict(pipeline_mode=pl.Buffered(1)) if n_tok > GDN_SUPER else {}

    def col(sec):
        return pl.BlockSpec((n_tok, dk * hps), lambda b, hh: (b + row_block_off, sec * hb + hh), **mode)

    def cw(sec):
        return pl.BlockSpec((GDN_CONV, dk * hps), lambda b, hh: (0, sec * hb + hh))

    in_specs = [col(0), col(1), col(2), col(3), cw(0), cw(1), cw(2),
                pl.BlockSpec((None, hps, n_tok, 4), lambda b, hh: (b, hh, 0, 0)),
                pl.BlockSpec((None, hps, n_super, 4, GDN_SUPER), lambda b, hh: (b, hh, 0, 0, 0)),
                pl.BlockSpec((1, GDN_DV), lambda b, hh: (0, 0))]
    args = [qkvz, qkvz, qkvz, qkvz, conv_w, conv_w, conv_w, gc, gr, norm_g.reshape(1, GDN_DV)]
    st_spec = pl.BlockSpec((None, 2, hps, dk, GDN_DV), lambda b, hh: (b, 0, hh, 0, 0))
    if has_s0:
        in_specs.append(st_spec)
        args.append(s0)
    return pl.pallas_call(
        functools.partial(_gdn_core_kernel, n_tok=n_tok, has_s0=has_s0, hps=hps),
        grid=(batch, hb),
        in_specs=in_specs,
        out_specs=[pl.BlockSpec((n_tok, GDN_DV * hps), lambda b, hh: (b, hh)), st_spec],
        out_shape=[jax.ShapeDtypeStruct((batch * n_tok, h * GDN_DV), F32),
                   jax.ShapeDtypeStruct((batch, 2, h, dk, GDN_DV), F32)],
        scratch_shapes=[
            pltpu.VMEM((3 * hps, n_tok + 2 * V7X_SUBLANES, dk), F32),
            pltpu.VMEM((hps, n_tok, dk), F32),
            pltpu.VMEM((2 * hps, n_tok, GDN_DV), F32),
            pltpu.VMEM((2 * hps, n_tok, dk), F32),
            pltpu.VMEM((2 * hps, n_tok, dk), F32),
            pltpu.VMEM((2 * hps, n_tok, GDN_CHUNK), F32),
            pltpu.VMEM((2 * hps, n_tok, GDN_DV), F32),
            pltpu.VMEM((2 * hps, dk, GDN_DV), F32),
        ],
        compiler_params=_cparams("arbitrary", "arbitrary"),
        name=name,
    )(*args)


def _gdn_layer(x, mods_l, li, state_gdn, gdn_w_in, gdn_conv, gdn_a_log, gdn_dt_bias, gdn_norm_g):
    w_ab = jnp.pad(gdn_w_in[li, :, GDN_QKVZ:], ((0, 0), (0, V7X_LANES - 4 * GDN_HEADS)))
    qkvz, ab = _modproj_call(x, mods_l, 0, 1, gdn_w_in, GDN_QKVZ, w_ab, w_layer=li, tn=1024, out_dtype=BF16,
                             name="gdn_in")
    gates = _gdn_gates_call(ab, gdn_a_log[li], gdn_dt_bias[li])[:, :4 * GDN_HEADS]

    def head_layouts(g, batch, n_tok):
        g = g.reshape(batch, n_tok, 4, GDN_HEADS)
        gc = jnp.transpose(g, (0, 3, 1, 2))
        gr = jnp.transpose(g, (0, 3, 2, 1)).reshape(batch, GDN_HEADS, 4, n_tok // GDN_SUPER, GDN_SUPER)
        return gc, jnp.transpose(gr, (0, 1, 3, 2, 4))

    gc_p, gr_p = head_layouts(gates[:T_CTX], BATCH, SEQ)
    gc_s, gr_s = head_layouts(gates[T_CTX:], DEC_BATCH, DEC_SEQ)
    o_p, st_p = _gdn_core_call(qkvz, gdn_conv[li], gc_p, gr_p, gdn_norm_g[li], None,
                               batch=BATCH, n_tok=SEQ, row_block_off=0, hps=4, name="gdn_core_ctx")
    o_s, _ = _gdn_core_call(qkvz, gdn_conv[li], gc_s, gr_s, gdn_norm_g[li], state_gdn[:, li],
                            batch=DEC_BATCH, n_tok=DEC_SEQ, row_block_off=T_CTX // DEC_SEQ, hps=2,
                            name="gdn_core_lat")
    return (o_p, o_s), st_p


def _rope_tables(n_tokens, rot_dim, reps):
    n_rows = n_tokens // GRID_W
    row = jnp.repeat(jnp.arange(n_rows, dtype=F32), GRID_W)
    colp = jnp.tile(jnp.arange(GRID_W, dtype=F32), n_rows)
    n_freq = rot_dim // 4
    inv_freq = ROPE_THETA ** (-jnp.arange(n_freq, dtype=F32) / n_freq)
    ra, ca = row[:, None] * inv_freq, colp[:, None] * inv_freq
    cos = jnp.concatenate([jnp.cos(ra), jnp.cos(ra), jnp.cos(ca), jnp.cos(ca)], axis=-1)
    sin = jnp.concatenate([-jnp.sin(ra), jnp.sin(ra), -jnp.sin(ca), jnp.sin(ca)], axis=-1)
    return jnp.tile(cos, (1, reps)), jnp.tile(sin, (1, reps))


def _rope(x, cos, sin, quarter):
    n = x.shape[-1]
    lane = lax.broadcasted_iota(jnp.int32, x.shape, x.ndim - 1)
    first = (lane & (2 * quarter - 1)) < quarter
    partner = jnp.where(first, pltpu.roll(x, n - quarter, x.ndim - 1), pltpu.roll(x, quarter, x.ndim - 1))
    return x * cos + partner * sin


def _gqa_prep_kernel(kv_ref, g_ref, *rest, rope):
    if rope:
        cos_ref, sin_ref, k_ref, v_ref = rest
    else:
        k_ref, v_ref = rest
    kv = kv_ref[...]
    nk = GQA_KV_HEADS * GQA_HEAD_DIM
    for hh in range(GQA_KV_HEADS):
        k = _rms(kv[:, hh * GQA_HEAD_DIM:(hh + 1) * GQA_HEAD_DIM], g_ref[...])
        if rope:
            k = _rope(k, cos_ref[...], sin_ref[...], GQA_HEAD_DIM // 4)
        k_ref[:, hh * GQA_HEAD_DIM:(hh + 1) * GQA_HEAD_DIM] = k
    v_ref[...] = kv[:, nk:]


def _gqa_prep_call(proj, k_norm, row_off, n_rows, tables, *, tm=512, name):
    rope = tables is not None
    nkv = 2 * GQA_KV_HEADS * GQA_HEAD_DIM
    kv_col = GQA_HEADS * GQA_HEAD_DIM // nkv
    off = row_off // tm
    in_specs = [pl.BlockSpec((tm, nkv), lambda i: (i + off, kv_col)),
                pl.BlockSpec((1, GQA_HEAD_DIM), lambda i: (0, 0))]
    args = [proj, k_norm.reshape(1, GQA_HEAD_DIM)]
    if rope:
        per_seq = DEC_SEQ // tm
        in_specs += [pl.BlockSpec((tm, GQA_HEAD_DIM), lambda i: (i % per_seq, 0))] * 2
        args += list(tables)
    half = nkv // 2
    return pl.pallas_call(
        functools.partial(_gqa_prep_kernel, rope=rope),
        grid=(n_rows // tm,),
        in_specs=in_specs,
        out_specs=[pl.BlockSpec((tm, half), lambda i: (i, 0))] * 2,
        out_shape=[jax.ShapeDtypeStruct((n_rows, half), F32)] * 2,
        compiler_params=_cparams("arbitrary"),
        name=name,
    )(*args)


def _attend(chains, scale):
    scores = [[_mm_nt(q, k) * scale for k in ks] for q, ks, _ in chains]
    probs, dens = [], []
    for sc in scores:
        m = None
        for s in sc:
            sm = jnp.max(s, axis=-1, keepdims=True)
            m = sm if m is None else jnp.maximum(m, sm)
        ps = [jnp.exp(s - m) for s in sc]
        den = None
        for p in ps:
            r = jnp.sum(p, axis=-1, keepdims=True)
            den = r if den is None else den + r
        probs.append([p.astype(BF16) for p in ps])
        dens.append(den)
    outs = []
    for (_, _, vs), ps, den in zip(chains, probs, dens):
        acc = None
        for p, v in zip(ps, vs):
            pv = _mm(p, v)
            acc = pv if acc is None else acc + pv
        outs.append(acc / den)
    return outs


def _gqa_attn_kernel(q_ref, qg_ref, *rest, n_seg, rope, kv_per_step):
    if rope:
        cos_ref, sin_ref = rest[:2]
        rest = rest[2:]
    seg_refs = rest[:2 * n_seg]
    o_ref = rest[2 * n_seg]
    dh = GQA_HEAD_DIM
    tq = q_ref.shape[0]
    heads_per_chain = GQA_GROUP // 2
    chains, heads = [], []
    for kh in range(kv_per_step):
        ks = [seg_refs[2 * sgi][:, kh * dh:(kh + 1) * dh].astype(BF16) for sgi in range(n_seg)]
        vs = [seg_refs[2 * sgi + 1][:, kh * dh:(kh + 1) * dh].astype(BF16) for sgi in range(n_seg)]
        for g0 in range(0, GQA_GROUP, heads_per_chain):
            qs = []
            for g in range(g0, g0 + heads_per_chain):
                c0 = (kh * GQA_GROUP + g) * dh
                qh = _rms(q_ref[:, c0:c0 + dh], qg_ref[...])
                if rope:
                    qh = _rope(qh, cos_ref[...], sin_ref[...], dh // 4)
                qs.append(qh.astype(BF16))
            chains.append((jnp.concatenate(qs, axis=0), ks, vs))
            heads.append(kh * GQA_GROUP + g0)
    for o, h0 in zip(_attend(chains, dh ** -0.5), heads):
        for g in range(heads_per_chain):
            o_ref[:, (h0 + g) * dh:(h0 + g + 1) * dh] = o[g * tq:(g + 1) * tq]


def _gqa_attn_call(proj, q_norm, segs, tables, *, batch, n_q, row_off, tq, kv_per_step, name):
    rope = tables is not None
    dh = GQA_HEAD_DIM
    qw = GQA_GROUP * dh * kv_per_step
    kw = dh * kv_per_step
    nqb = n_q // tq
    off = row_off // tq
    in_specs = [pl.BlockSpec((tq, qw), lambda b, h, i: (b * nqb + i + off, h)),
                pl.BlockSpec((1, dh), lambda b, h, i: (0, 0))]
    args = [proj, q_norm.reshape(1, dh)]
    if rope:
        in_specs += [pl.BlockSpec((tq, dh), lambda b, h, i: (i, 0))] * 2
        args += list(tables)
    for k2d, v2d, rows in segs:
        in_specs += [pl.BlockSpec((rows, kw), lambda b, h, i: (b, h))] * 2
        args += [k2d, v2d]
    return pl.pallas_call(
        functools.partial(_gqa_attn_kernel, n_seg=len(segs), rope=rope, kv_per_step=kv_per_step),
        grid=(batch, GQA_KV_HEADS // kv_per_step, nqb),
        in_specs=in_specs,
        out_specs=pl.BlockSpec((tq, qw), lambda b, h, i: (b * nqb + i, h)),
        out_shape=jax.ShapeDtypeStruct((batch * n_q, GQA_HEADS * dh), F32),
        compiler_params=_cparams("arbitrary", "arbitrary", "arbitrary"),
        name=name,
    )(*args)


def _gqa_layer(x, mods_l, li, cache_k, cache_v, gqa_w_in, gqa_q_norm, gqa_k_norm):
    proj = _modproj_call(x, mods_l, 0, 1, gqa_w_in, GQA_IN, w_layer=li, name="gqa_in")
    tables = _rope_tables(DEC_SEQ, GQA_HEAD_DIM, 1)
    nkv = GQA_KV_HEADS * GQA_HEAD_DIM
    k_p, v_p = _gqa_prep_call(proj, gqa_k_norm[li], 0, T_CTX, None, name="gqa_prep_ctx")
    k_s, v_s = _gqa_prep_call(proj, gqa_k_norm[li], T_CTX, T_LAT, tables, name="gqa_prep_lat")
    o_p = _gqa_attn_call(proj, gqa_q_norm[li], [(k_p, v_p, SEQ)], None, batch=BATCH, n_q=SEQ, row_off=0,
                         tq=SEQ, kv_per_step=GQA_KV_HEADS, name="gqa_attn_ctx")
    ck = cache_k[:, li].reshape(DEC_BATCH * PAST_LEN, nkv)
    cv = cache_v[:, li].reshape(DEC_BATCH * PAST_LEN, nkv)
    o_s = _gqa_attn_call(proj, gqa_q_norm[li], [(ck, cv, PAST_LEN), (k_s, v_s, DEC_SEQ)], tables,
                         batch=DEC_BATCH, n_q=DEC_SEQ, row_off=T_CTX, tq=256, kv_per_step=1, name="gqa_attn_lat")
    new_k = k_p.reshape(BATCH, SEQ, GQA_KV_HEADS, GQA_HEAD_DIM)
    new_v = v_p.reshape(BATCH, SEQ, GQA_KV_HEADS, GQA_HEAD_DIM)
    return (o_p, o_s), new_k, new_v


def _krope_kernel(x_ref, *rest, rope):
    if rope:
        cos_ref, sin_ref, o_ref = rest
    else:
        (o_ref,) = rest
    x = x_ref[...]
    x2 = jnp.concatenate([x, x], axis=1)
    if rope:
        x2 = _rope(x2, cos_ref[...], sin_ref[...], MLA_ROPE // 4)
    o_ref[...] = x2


def _krope_call(krope, tables, *, tm=512, name):
    rope = tables is not None
    n = krope.shape[0]
    in_specs = [pl.BlockSpec((tm, MLA_ROPE), lambda i: (i, 0))]
    args = [krope]
    if rope:
        per_seq = DEC_SEQ // tm
        in_specs += [pl.BlockSpec((tm, 2 * MLA_ROPE), lambda i: (i % per_seq, 0))] * 2
        args += list(tables)
    return pl.pallas_call(
        functools.partial(_krope_kernel, rope=rope),
        grid=(n // tm,),
        in_specs=in_specs,
        out_specs=pl.BlockSpec((tm, 2 * MLA_ROPE), lambda i: (i, 0)),
        out_shape=jax.ShapeDtypeStruct((n, 2 * MLA_ROPE), F32),
        compiler_params=_cparams("arbitrary"),
        name=name,
    )(*args)


def _mla_attn_kernel(qn_ref, qr_ref, *rest, n_seg, rope, heads_per_step):
    if rope:
        cos_ref, sin_ref = rest[:2]
        rest = rest[2:]
    seg_refs = rest[:2 * n_seg]
    o_ref = rest[2 * n_seg]
    scale = (MLA_NOPE + MLA_ROPE) ** -0.5
    h0 = pl.program_id(1) * heads_per_step
    pair_w = 2 * MLA_ROPE
    tq = qn_ref.shape[0]
    row_splits = 2 if heads_per_step == 1 else 1
    rows = tq // row_splits
    chains = []
    for j in range(heads_per_step):
        qn = qn_ref[:, j * MLA_NOPE:(j + 1) * MLA_NOPE]
        pair = qr_ref[:, (j // 2) * pair_w:(j // 2 + 1) * pair_w]
        if rope:
            pair = _rope(pair, cos_ref[...], sin_ref[...], MLA_ROPE // 4)
        lane = lax.broadcasted_iota(jnp.int32, pair.shape, 1)
        mine = (lane >> int(math.log2(MLA_ROPE))) == ((h0 + j) & 1)
        q = jnp.concatenate([qn, jnp.where(mine, pair, 0.0)], axis=1).astype(BF16)
        ks, vs = [], []
        for sgi in range(n_seg):
            kv = seg_refs[2 * sgi][:, j * (MLA_NOPE + MLA_V):(j + 1) * (MLA_NOPE + MLA_V)]
            ks.append(jnp.concatenate([kv[:, :MLA_NOPE], seg_refs[2 * sgi + 1][...]], axis=1).astype(BF16))
            vs.append(kv[:, MLA_NOPE:].astype(BF16))
        for c in range(row_splits):
            chains.append((q[c * rows:(c + 1) * rows], ks, vs))
    outs = _attend(chains, scale)
    for j in range(heads_per_step):
        for c in range(row_splits):
            o_ref[c * rows:(c + 1) * rows, j * MLA_V:(j + 1) * MLA_V] = outs[j * row_splits + c]


def _mla_attn_call(q, segs, tables, *, batch, n_q, row_off, tq, heads_per_step, name):
    rope = tables is not None
    hps = heads_per_step
    nqb = n_q // tq
    off = row_off // tq
    rope_w = max(2 * MLA_ROPE, MLA_ROPE * hps)
    rope_base = MLA_HEADS * MLA_NOPE // rope_w

    def rope_idx(b, h, i):
        return (b * nqb + i + off, rope_base + (h * hps * MLA_ROPE) // rope_w)

    in_specs = [pl.BlockSpec((tq, MLA_NOPE * hps), lambda b, h, i: (b * nqb + i + off, h)),
                pl.BlockSpec((tq, rope_w), rope_idx)]
    args = [q, q]
    if rope:
        in_specs += [pl.BlockSpec((tq, 2 * MLA_ROPE), lambda b, h, i: (i, 0))] * 2
        args += list(tables)
    for kv2d, kr2, rows, kv_row_off in segs:
        kv_off = kv_row_off // rows
        in_specs += [pl.BlockSpec((rows, (MLA_NOPE + MLA_V) * hps), lambda b, h, i, kv_off=kv_off: (b + kv_off, h)),
                     pl.BlockSpec((rows, 2 * MLA_ROPE), lambda b, h, i: (b, 0))]
        args += [kv2d, kr2]
    return pl.pallas_call(
        functools.partial(_mla_attn_kernel, n_seg=len(segs), rope=rope, heads_per_step=hps),
        grid=(batch, MLA_HEADS // hps, nqb),
        in_specs=in_specs,
        out_specs=pl.BlockSpec((tq, MLA_V * hps), lambda b, h, i: (b * nqb + i, h)),
        out_shape=jax.ShapeDtypeStruct((batch * n_q, MLA_HEADS * MLA_V), F32),
        compiler_params=_cparams("arbitrary", "arbitrary", "arbitrary"),
        name=name,
    )(*args)


def _mla_layer(x, mods_l, li, cache_ckv, cache_krope, mla_w_in, mla_q_norm, mla_kv_norm, mla_w_q_up, mla_w_kv_up):
    w_in = jnp.pad(mla_w_in[li], ((0, 0), (0, 7 * V7X_LANES - MLA_IN)))
    proj = _modproj_call(x, mods_l, 0, 1, w_in, 7 * V7X_LANES, tn=7 * V7X_LANES, name="mla_in")
    wq = mla_w_q_up[li].reshape(MLA_Q_LORA, MLA_HEADS, MLA_NOPE + MLA_ROPE)
    wq = jnp.concatenate([wq[:, :, :MLA_NOPE].reshape(MLA_Q_LORA, -1), wq[:, :, MLA_NOPE:].reshape(MLA_Q_LORA, -1)], axis=1)
    q = _rmsproj_call(proj, 0, MLA_Q_LORA, mla_q_norm[li], wq, name="mla_q_up")
    kv, ckv = _rmsproj_call(proj, MLA_Q_LORA // MLA_KV_LORA, MLA_KV_LORA, mla_kv_norm[li], mla_w_kv_up[li],
                            emit_xn=True, name="mla_kv_up")
    krope = proj[:, MLA_Q_LORA + MLA_KV_LORA:MLA_IN]
    tables = _rope_tables(DEC_SEQ, MLA_ROPE, 2)
    kr2_p = _krope_call(krope[:T_CTX], None, name="mla_krope_ctx")
    kr2_s = _krope_call(krope[T_CTX:], tables, name="mla_krope_lat")
    ctx_ckv = cache_ckv[:, li].reshape(DEC_BATCH * PAST_LEN, MLA_KV_LORA)
    kv_c = _rmsproj_call(ctx_ckv, 0, MLA_KV_LORA, jnp.ones((MLA_KV_LORA,), F32), mla_w_kv_up[li], normalize=False,
                         name="mla_kv_up_cache")
    kr2_c = _krope_call(cache_krope[:, li].reshape(DEC_BATCH * PAST_LEN, MLA_ROPE), None, name="mla_krope_cache")
    o_p = _mla_attn_call(q, [(kv, kr2_p, SEQ, 0)], None, batch=BATCH, n_q=SEQ, row_off=0, tq=SEQ,
                         heads_per_step=MLA_HEADS, name="mla_attn_ctx")
    o_s = _mla_attn_call(q, [(kv_c, kr2_c, PAST_LEN, 0), (kv, kr2_s, DEC_SEQ, T_CTX)], tables, batch=DEC_BATCH,
                         n_q=DEC_SEQ, row_off=T_CTX, tq=512, heads_per_step=1, name="mla_attn_lat")
    new_ckv = ckv[:T_CTX].reshape(BATCH, SEQ, MLA_KV_LORA)
    new_krope = krope[:T_CTX].reshape(BATCH, SEQ, MLA_ROPE)
    return (o_p, o_s), new_ckv, new_krope


def _route(h, rw_ref, rb_ref):
    logits = lax.dot_general(rw_ref[...], h, (((1,), (1,)), ((), ())), precision=lax.Precision.HIGHEST,
                             preferred_element_type=F32)[:N_EXPERTS]
    scores = _sigmoid(logits)
    sel = scores + rb_ref[0:N_EXPERTS, :]
    tm = sel.shape[1]
    neg = -jnp.inf
    gs = []
    for g in range(N_GROUPS):
        a, b, c, d = (sel[EXPERTS_PER_GROUP * g + j:EXPERTS_PER_GROUP * g + j + 1, :] for j in range(4))
        gs.append(jnp.maximum(jnp.maximum(jnp.maximum(a + b, a + c), jnp.maximum(a + d, b + c)),
                              jnp.maximum(b + d, c + d)))
    gmax = jnp.maximum(jnp.maximum(gs[0], gs[1]), jnp.maximum(gs[2], gs[3]))
    best = jnp.where(gs[0] == gmax, 0, jnp.where(gs[1] == gmax, 1, jnp.where(gs[2] == gmax, 2, 3)))
    eid = lax.broadcasted_iota(jnp.int32, (N_EXPERTS, tm), 0)
    masked = jnp.where((eid >> int(math.log2(EXPERTS_PER_GROUP))) == best, sel, neg)
    m1 = jnp.max(masked, axis=0, keepdims=True)
    i1 = jnp.min(jnp.where(masked == m1, eid, N_EXPERTS), axis=0, keepdims=True)
    masked2 = jnp.where(eid == i1, neg, masked)
    m2 = jnp.max(masked2, axis=0, keepdims=True)
    i2 = jnp.min(jnp.where(masked2 == m2, eid, N_EXPERTS), axis=0, keepdims=True)
    w1 = jnp.sum(jnp.where(eid == i1, scores, 0.0), axis=0, keepdims=True)
    w2 = jnp.sum(jnp.where(eid == i2, scores, 0.0), axis=0, keepdims=True)
    tot = w1 + w2
    comb = jnp.where(eid == i1, w1 / tot, 0.0) + jnp.where(eid == i2, w2 / tot, 0.0)
    comb = jnp.concatenate([comb, jnp.zeros((V7X_LANES - N_EXPERTS, tm), F32)], axis=0)
    return comb.T


def _moe_kernel(xc_ref, xl_ref, sh_ref, sc_ref, gate_ref, comb_ref, wg_ref, wu_ref, wd_ref, g_ref, b_ref,
                yc_ref, yl_ref, h_scr, acc_scr, *, tiles_per_seg):
    i, e = pl.program_id(0), pl.program_id(1)
    r = _seg_row(i, tiles_per_seg)

    @pl.when(e == 0)
    def _():
        x = _slab_read(i, xc_ref, xl_ref)
        h_scr[...] = (x * (1.0 + sc_ref[pl.ds(r, 1), :]) + sh_ref[pl.ds(r, 1), :]).astype(BF16)
        acc_scr[...] = jnp.zeros_like(acc_scr)

    comb = comb_ref[...]
    lane = lax.broadcasted_iota(jnp.int32, comb.shape, 1)
    h = h_scr[...]
    epb = wg_ref.shape[0]
    f = None
    for j in range(epb):
        c_e = jnp.sum(jnp.where(lane == e * epb + j, comb, 0.0), axis=1, keepdims=True)
        hid = _silu(_mm(h, wg_ref[j])) * _mm(h, wu_ref[j]) * c_e
        fj = _mm(hid, wd_ref[j])
        f = fj if f is None else f + fj
    acc_scr[...] += f

    def result():
        gate = gate_ref[pl.ds(r, 1), :]
        x = _slab_read(i, xc_ref, xl_ref)
        return _layer_norm(ALPHA * x + (1.0 + gate) * acc_scr[...], g_ref[...], b_ref[...])

    _slab_write(i, e == pl.num_programs(1) - 1, yc_ref, yl_ref, result)


MOE_EXPERTS_PER_STEP = 2


def _moe_call(x, mods_l, comb, w_gate, w_up, w_down, layer, g, b, *, tm=1024):
    t = T_ALL
    epb = MOE_EXPERTS_PER_STEP
    yc, yl = pl.pallas_call(
        functools.partial(_moe_kernel, tiles_per_seg=SEG_ROWS // tm),
        grid=(t // tm, N_EXPERTS // epb),
        in_specs=_slab_specs(tm, 2, pipeline_mode=pl.Buffered(1)) + [
            pl.BlockSpec((MOD_ROWS, D_MODEL), lambda i, e: (0, 3)),
            pl.BlockSpec((MOD_ROWS, D_MODEL), lambda i, e: (0, 4)),
            pl.BlockSpec((MOD_ROWS, D_MODEL), lambda i, e: (0, 5)),
            pl.BlockSpec((tm, V7X_LANES), lambda i, e: (i, 0)),
            pl.BlockSpec((None, epb, D_MODEL, D_EXPERT), lambda i, e: (layer, e, 0, 0)),
            pl.BlockSpec((None, epb, D_MODEL, D_EXPERT), lambda i, e: (layer, e, 0, 0)),
            pl.BlockSpec((None, epb, D_EXPERT, D_MODEL), lambda i, e: (layer, e, 0, 0)),
            pl.BlockSpec((1, D_MODEL), lambda i, e: (0, 0)),
            pl.BlockSpec((1, D_MODEL), lambda i, e: (0, 0)),
        ],
        out_specs=_slab_specs(tm, 2),
        out_shape=_slab_shapes(),
        scratch_shapes=[pltpu.VMEM((tm, D_MODEL), BF16), pltpu.VMEM((tm, D_MODEL), F32)],
        compiler_params=_cparams("arbitrary", "arbitrary"),
        name="moe_experts",
    )(x[0], x[1], mods_l, mods_l, mods_l, comb, w_gate, w_up, w_down, g.reshape(1, D_MODEL), b.reshape(1, D_MODEL))
    return yc, yl


def kernel(x_prompt, x_sample, state_gdn, cache_gqa_k, cache_gqa_v, cache_mla_ckv, cache_mla_krope, c, c_ctx, w_ada, b_ada, ln_mix_g, ln_mix_b, ln_ffn_g, ln_ffn_b, gdn_w_in, gdn_conv, gdn_a_log, gdn_dt_bias, gdn_norm_g, gdn_w_out, gqa_w_in, gqa_q_norm, gqa_k_norm, gqa_w_out, mla_w_in, mla_q_norm, mla_kv_norm, mla_w_q_up, mla_w_kv_up, mla_w_out, router_w, router_b, moe_w_gate, moe_w_up, moe_w_down):
    x = (x_prompt.reshape(T_CTX, D_MODEL), x_sample.reshape(T_LAT, D_MODEL))
    cond = jnp.concatenate([c_ctx[None], c, jnp.zeros((MOD_ROWS - 1 - DEC_BATCH, D_MODEL), F32)], axis=0)
    mods = _ada_call(cond, w_ada, b_ada)

    st_gdn, st_k, st_v, st_ckv, st_krope = [], [], [], [], []
    for l in range(DEPTH):
        kind, li = l % N_MIXERS, l // N_MIXERS
        mods_l = mods[l]
        if kind == 0:
            o, st = _gdn_layer(x, mods_l, li, state_gdn, gdn_w_in, gdn_conv, gdn_a_log, gdn_dt_bias, gdn_norm_g)
            st_gdn.append(st)
            w_out = gdn_w_out
        elif kind == 1:
            o, nk, nv = _gqa_layer(x, mods_l, li, cache_gqa_k, cache_gqa_v, gqa_w_in, gqa_q_norm, gqa_k_norm)
            st_k.append(nk)
            st_v.append(nv)
            w_out = gqa_w_out
        else:
            o, nckv, nkr = _mla_layer(x, mods_l, li, cache_mla_ckv, cache_mla_krope, mla_w_in, mla_q_norm,
                                      mla_kv_norm, mla_w_q_up, mla_w_kv_up)
            st_ckv.append(nckv)
            st_krope.append(nkr)
            w_out = mla_w_out
        x, comb = _outln_call(o, w_out, li, x, mods_l, ln_mix_g[l], ln_mix_b[l], router_w, router_b,
                              name="mix_out_ln")
        x = _moe_call(x, mods_l, comb, moe_w_gate, moe_w_up, moe_w_down, l, ln_ffn_g[l], ln_ffn_b[l])

    y_prompt = x[0].reshape(BATCH, SEQ, D_MODEL)
    y_sample = x[1].reshape(DEC_BATCH, DEC_SEQ, D_MODEL)
    return (y_prompt, y_sample, jnp.stack(st_gdn, axis=1), jnp.stack(st_k, axis=1), jnp.stack(st_v, axis=1),
            jnp.stack(st_ckv, axis=1), jnp.stack(st_krope, axis=1))
```

```python
import functools
import math

import jax
import jax.numpy as jnp
from jax import lax
from jax.experimental import pallas as pl
from jax.experimental.pallas import tpu as pltpu

F32 = jnp.float32
BF16 = jnp.bfloat16

D_MODEL = 1024
BATCH = 32
SEQ = 256
DEPTH = 4
DEC_BATCH = 2
DEC_SEQ = 2048
PAST_LEN = 512
GRID_W = 64
N_MIXERS = 3

GDN_HEADS = 8
GDN_DK = 128
GDN_DV = 128
GDN_CONV = 5
GDN_CHUNK = 64
GDN_QKV = GDN_HEADS * (2 * GDN_DK + GDN_DV)
GDN_QKVZ = GDN_QKV + GDN_HEADS * GDN_DV

GQA_HEADS = 8
GQA_KV_HEADS = 2
GQA_GROUP = GQA_HEADS // GQA_KV_HEADS
GQA_HEAD_DIM = 128
GQA_IN = (GQA_HEADS + 2 * GQA_KV_HEADS) * GQA_HEAD_DIM

MLA_HEADS = 8
MLA_Q_LORA = 512
MLA_KV_LORA = 256
MLA_NOPE = 128
MLA_ROPE = 64
MLA_V = 128
MLA_IN = MLA_Q_LORA + MLA_KV_LORA + MLA_ROPE

N_EXPERTS = 16
N_GROUPS = 4
EXPERTS_PER_GROUP = N_EXPERTS // N_GROUPS
D_EXPERT = 256

ROPE_THETA = 10000.0
ALPHA = (2 * DEPTH) ** 0.25
NORM_EPS = 1e-6

T_CTX = BATCH * SEQ
T_LAT = DEC_BATCH * DEC_SEQ
T_ALL = T_CTX + T_LAT
SEG_ROWS = DEC_SEQ
N_CTX_SEGS = T_CTX // SEG_ROWS
MOD_ROWS = 8

V7X_LANES = 128
V7X_SUBLANES = 8
V7X_VMEM_BYTES = 64 * 1024 * 1024
VMEM_LIMIT = V7X_VMEM_BYTES * 7 // 8

GDN_SUPER = 256
CHUNKS_PER_SUPER = GDN_SUPER // GDN_CHUNK


def _cparams(*sem):
    return pltpu.CompilerParams(dimension_semantics=sem, vmem_limit_bytes=VMEM_LIMIT)


def _mm(a, b):
    return jnp.dot(a.astype(BF16), b.astype(BF16), preferred_element_type=F32)


def _mm_nt(a, b):
    return lax.dot_general(a.astype(BF16), b.astype(BF16), (((1,), (1,)), ((), ())),
                           preferred_element_type=F32)


def _silu(x):
    return x * (1.0 / (1.0 + jnp.exp(-x)))


def _sigmoid(x):
    return 1.0 / (1.0 + jnp.exp(-x))


def _chunk_of(idx):
    return idx >> int(math.log2(GDN_CHUNK))


def _seg_row(i, tiles_per_seg):
    return jnp.maximum(lax.div(i, tiles_per_seg) - (N_CTX_SEGS - 1), 0)


def _layer_norm(y, g, b):
    mu = jnp.mean(y, axis=-1, keepdims=True)
    yc = y - mu
    var = jnp.mean(yc * yc, axis=-1, keepdims=True)
    return yc * lax.rsqrt(var + NORM_EPS) * g + b


def _rms(x, g):
    return x * lax.rsqrt(jnp.mean(x * x, axis=-1, keepdims=True) + NORM_EPS) * g


def _slab_specs(tm, n_grid_axes, **mode):
    ctx_tiles = T_CTX // tm
    if n_grid_axes == 1:
        return [pl.BlockSpec((tm, D_MODEL), lambda i: (jnp.minimum(i, ctx_tiles - 1), 0), **mode),
                pl.BlockSpec((tm, D_MODEL), lambda i: (jnp.maximum(i - ctx_tiles, 0), 0), **mode)]
    return [pl.BlockSpec((tm, D_MODEL), lambda i, j: (jnp.minimum(i, ctx_tiles - 1), 0), **mode),
            pl.BlockSpec((tm, D_MODEL), lambda i, j: (jnp.maximum(i - ctx_tiles, 0), 0), **mode)]


def _slab_shapes():
    return [jax.ShapeDtypeStruct((T_CTX, D_MODEL), F32), jax.ShapeDtypeStruct((T_LAT, D_MODEL), F32)]


def _slab_read(i, xc_ref, xl_ref):
    return jnp.where(i < T_CTX // xc_ref.shape[0], xc_ref[...], xl_ref[...])


def _slab_write(i, when, yc_ref, yl_ref, value_fn):
    ctx_tiles = T_CTX // yc_ref.shape[0]

    @pl.when(when & (i < ctx_tiles))
    def _():
        yc_ref[...] = value_fn()

    @pl.when(when & (i >= ctx_tiles))
    def _():
        yl_ref[...] = value_fn()


def _ada_kernel(c_ref, w_ref, b_ref, o_ref):
    o_ref[...] = _mm(_silu(c_ref[...]), w_ref[...]) + b_ref[...]


def _ada_call(cond, w_ada, b_ada):
    tn = 1536
    n = w_ada.shape[-1]
    return pl.pallas_call(
        _ada_kernel,
        grid=(DEPTH, n // tn),
        in_specs=[
            pl.BlockSpec((MOD_ROWS, D_MODEL), lambda l, j: (0, 0)),
            pl.BlockSpec((None, D_MODEL, tn), lambda l, j: (l, 0, j)),
            pl.BlockSpec((None, 1, tn), lambda l, j: (l, 0, j)),
        ],
        out_specs=pl.BlockSpec((None, MOD_ROWS, tn), lambda l, j: (l, 0, j)),
        out_shape=jax.ShapeDtypeStruct((DEPTH, MOD_ROWS, n), F32),
        compiler_params=_cparams("arbitrary", "arbitrary"),
        name="ada_mod",
    )(cond, w_ada, b_ada.reshape(DEPTH, 1, n))


def _modproj_kernel(x_ref, sh_ref, sc_ref, w_ref, *rest, tiles_per_seg, has_extra):
    if has_extra:
        wx_ref, o_ref, ox_ref, h_scr, wb_scr = rest
    else:
        o_ref, h_scr, wb_scr = rest
    i, j = pl.program_id(0), pl.program_id(1)

    @pl.when(i == 0)
    def _():
        wb_scr[j] = w_ref[...].astype(BF16)

    @pl.when(j == 0)
    def _():
        r = _seg_row(i, tiles_per_seg)
        sh = sh_ref[pl.ds(r, 1), :]
        sc = sc_ref[pl.ds(r, 1), :]
        h = (x_ref[...] * (1.0 + sc) + sh).astype(BF16)
        h_scr[...] = h
        if has_extra:
            ox_ref[...] = _mm(h, wx_ref[...])

    o_ref[...] = jnp.dot(h_scr[...], wb_scr[j], preferred_element_type=F32).astype(o_ref.dtype)


def _modproj_call(x, mods_l, shift_idx, scale_idx, w, n_out, w_extra=None, *, w_layer=None, tm=1024, tn=512,
                  out_dtype=F32, name):
    t = T_ALL
    has_extra = w_extra is not None
    nj = n_out // tn
    if w_layer is None:
        w_spec = pl.BlockSpec((D_MODEL, tn), lambda i, j: (0, jnp.where(i == 0, j, nj - 1)))
    else:
        w_spec = pl.BlockSpec((None, D_MODEL, tn), lambda i, j: (w_layer, 0, jnp.where(i == 0, j, nj - 1)))
    in_specs = [
        pl.BlockSpec((tm, D_MODEL), lambda i, j: (i, 0)),
        pl.BlockSpec((MOD_ROWS, D_MODEL), lambda i, j: (0, shift_idx)),
        pl.BlockSpec((MOD_ROWS, D_MODEL), lambda i, j: (0, scale_idx)),
        w_spec,
    ]
    out_specs = [pl.BlockSpec((tm, tn), lambda i, j: (i, j))]
    out_shape = [jax.ShapeDtypeStruct((t, n_out), out_dtype)]
    args = [x, mods_l, mods_l, w]
    if has_extra:
        nx = w_extra.shape[1]
        in_specs.append(pl.BlockSpec((D_MODEL, nx), lambda i, j: (0, 0)))
        out_specs.append(pl.BlockSpec((tm, nx), lambda i, j: (i, 0)))
        out_shape.append(jax.ShapeDtypeStruct((t, nx), F32))
        args.append(w_extra)
    res = pl.pallas_call(
        functools.partial(_modproj_kernel, tiles_per_seg=SEG_ROWS // tm, has_extra=has_extra),
        grid=(t // tm, nj),
        in_specs=in_specs,
        out_specs=out_specs,
        out_shape=out_shape,
        scratch_shapes=[pltpu.VMEM((tm, D_MODEL), BF16), pltpu.VMEM((nj, D_MODEL, tn), BF16)],
        compiler_params=_cparams("arbitrary", "arbitrary"),
        name=name,
    )(*args)
    return res if has_extra else res[0]


def _rmsproj_kernel(x_ref, g_ref, w_ref, *rest, normalize, emit_xn):
    if emit_xn:
        o_ref, xn_ref, h_scr = rest
    else:
        o_ref, h_scr = rest
    j = pl.program_id(1)

    @pl.when(j == 0)
    def _():
        x = x_ref[...]
        xn = _rms(x, g_ref[...]) if normalize else x
        h_scr[...] = xn.astype(BF16)
        if emit_xn:
            xn_ref[...] = xn

    o_ref[...] = _mm(h_scr[...], w_ref[...])


def _rmsproj_call(x, col_block, k, g, w, *, normalize=True, emit_xn=False, tm=1024, tn=512, name):
    t = x.shape[0]
    n_out = w.shape[1]
    out_specs = [pl.BlockSpec((tm, tn), lambda i, j: (i, j))]
    out_shape = [jax.ShapeDtypeStruct((t, n_out), F32)]
    if emit_xn:
        out_specs.append(pl.BlockSpec((tm, k), lambda i, j: (i, 0)))
        out_shape.append(jax.ShapeDtypeStruct((t, k), F32))
    res = pl.pallas_call(
        functools.partial(_rmsproj_kernel, normalize=normalize, emit_xn=emit_xn),
        grid=(t // tm, n_out // tn),
        in_specs=[
            pl.BlockSpec((tm, k), lambda i, j: (i, col_block)),
            pl.BlockSpec((1, k), lambda i, j: (0, 0)),
            pl.BlockSpec((k, tn), lambda i, j: (0, j)),
        ],
        out_specs=out_specs,
        out_shape=out_shape,
        scratch_shapes=[pltpu.VMEM((tm, k), BF16)],
        compiler_params=_cparams("arbitrary", "arbitrary"),
        name=name,
    )(x, g.reshape(1, k), w)
    return res if emit_xn else res[0]


def _outln_kernel(oc_ref, ol_ref, w_ref, x_ref, gate_ref, sh_ref, sc_ref, g_ref, b_ref, rw_ref, rb_ref,
                  y_ref, comb_ref, wb_scr, *, tiles_per_seg):
    i = pl.program_id(0)

    @pl.when(i == 0)
    def _():
        wb_scr[...] = w_ref[...].astype(BF16)

    r = _seg_row(i, tiles_per_seg)
    gate = gate_ref[pl.ds(r, 1), :]
    f = jnp.dot(_slab_read(i, oc_ref, ol_ref).astype(BF16), wb_scr[...], preferred_element_type=F32)
    y = _layer_norm(ALPHA * x_ref[...] + (1.0 + gate) * f, g_ref[...], b_ref[...])
    y_ref[...] = y
    comb_ref[...] = _route(y * (1.0 + sc_ref[pl.ds(r, 1), :]) + sh_ref[pl.ds(r, 1), :], rw_ref, rb_ref)


def _outln_call(o, w, w_layer, x, mods_l, g, b, router_w, router_b, *, tm=512, name):
    k = o[0].shape[1]
    t = T_ALL
    rw = jnp.pad(router_w.T, ((0, V7X_LANES - N_EXPERTS), (0, 0)))
    rb = jnp.pad(router_b.reshape(N_EXPERTS, 1), ((0, V7X_LANES - N_EXPERTS), (0, 0)))
    o_specs = [pl.BlockSpec((tm, k), s.index_map) for s in _slab_specs(tm, 1)]

    def mod(idx):
        return pl.BlockSpec((MOD_ROWS, D_MODEL), lambda i: (0, idx))

    return pl.pallas_call(
        functools.partial(_outln_kernel, tiles_per_seg=SEG_ROWS // tm),
        grid=(t // tm,),
        in_specs=o_specs + [
            pl.BlockSpec((None, k, D_MODEL), lambda i: (w_layer, 0, 0)),
            pl.BlockSpec((tm, D_MODEL), lambda i: (i, 0)),
            mod(2), mod(3), mod(4),
            pl.BlockSpec((1, D_MODEL), lambda i: (0, 0)),
            pl.BlockSpec((1, D_MODEL), lambda i: (0, 0)),
            pl.BlockSpec((V7X_LANES, D_MODEL), lambda i: (0, 0)),
            pl.BlockSpec((V7X_LANES, 1), lambda i: (0, 0)),
        ],
        out_specs=[pl.BlockSpec((tm, D_MODEL), lambda i: (i, 0)), pl.BlockSpec((tm, V7X_LANES), lambda i: (i, 0))],
        out_shape=[jax.ShapeDtypeStruct((t, D_MODEL), F32), jax.ShapeDtypeStruct((t, V7X_LANES), F32)],
        scratch_shapes=[pltpu.VMEM((k, D_MODEL), BF16)],
        compiler_params=_cparams("arbitrary"),
        name=name,
    )(o[0], o[1], w, x, mods_l, mods_l, mods_l, g.reshape(1, D_MODEL), b.reshape(1, D_MODEL), rw, rb)


def _gdn_gates_kernel(ab_ref, alog_ref, dt_ref, o_ref):
    ab = ab_ref[...]
    tm = ab.shape[0]
    x = ab + dt_ref[...]
    softplus = jnp.maximum(x, 0.0) + jnp.log(1.0 + jnp.exp(-jnp.abs(x)))
    g = -jnp.exp(alog_ref[...]) * softplus
    ri = lax.broadcasted_iota(jnp.int32, (tm, tm), 0)
    ci = lax.broadcasted_iota(jnp.int32, (tm, tm), 1)
    same = _chunk_of(ri) == _chunk_of(ci)
    lower = jnp.where(same & (ri >= ci), 1.0, 0.0).astype(F32)
    upper = jnp.where(same & (ri <= ci), 1.0, 0.0).astype(F32)
    pre = jnp.dot(lower, g, precision=lax.Precision.HIGHEST, preferred_element_type=F32)
    suf = jnp.dot(upper, g, precision=lax.Precision.HIGHEST, preferred_element_type=F32)
    lane = lax.broadcasted_iota(jnp.int32, ab.shape, 1)
    o_ref[...] = jnp.where(lane < GDN_HEADS, pre, jnp.where(lane < 2 * GDN_HEADS, suf, _sigmoid(ab)))


def _gdn_gates_call(ab, a_log, dt_bias):
    t, n = ab.shape
    tm = GDN_SUPER
    pad = n - 2 * GDN_HEADS
    alog_row = jnp.pad(a_log.reshape(1, -1), ((0, 0), (0, pad)))
    dt_row = jnp.pad(dt_bias.reshape(1, -1), ((0, 0), (0, pad)))
    return pl.pallas_call(
        _gdn_gates_kernel,
        grid=(t // tm,),
        in_specs=[
            pl.BlockSpec((tm, n), lambda i: (i, 0)),
            pl.BlockSpec((1, n), lambda i: (0, 0)),
            pl.BlockSpec((1, n), lambda i: (0, 0)),
        ],
        out_specs=pl.BlockSpec((tm, n), lambda i: (i, 0)),
        out_shape=jax.ShapeDtypeStruct((t, n), F32),
        compiler_params=_cparams("arbitrary"),
        name="gdn_gates",
    )(ab, alog_row, dt_row)


def _loop(n, body, max_static=4):
    if n <= max_static:
        for i in range(n):
            body(i)
    else:
        def wrapped(i, carry):
            body(i)
            return carry
        lax.fori_loop(0, n, wrapped, 0)


def _gdn_core_kernel(q_ref, k_ref, v_ref, z_ref, cq_ref, ck_ref, cv_ref, gc_ref, gr_ref, ng_ref, *rest,
                     n_tok, has_s0, hps):
    if has_s0:
        s0_ref, o_ref, st_ref = rest[:3]
        scr = rest[3:]
    else:
        o_ref, st_ref = rest[:2]
        scr = rest[2:]
    pad_scr, kn_scr, u_scr, w_scr, qe_scr, qk_scr, oo_scr, s_scr = scr
    n_super = n_tok // GDN_SUPER
    n_chunk = n_tok // GDN_CHUNK
    halo = V7X_SUBLANES
    sb = GDN_SUPER
    ck = GDN_CHUNK
    dk = GDN_DK

    zeros_halo = jnp.zeros((halo, dk), F32)
    for hd in range(hps):
        for a, src in enumerate((q_ref, k_ref, v_ref)):
            pad_scr[3 * hd + a, 0:halo, :] = zeros_halo
            pad_scr[3 * hd + a, halo:halo + n_tok, :] = src[:, hd * dk:(hd + 1) * dk].astype(F32)
            pad_scr[3 * hd + a, halo + n_tok:2 * halo + n_tok, :] = zeros_halo
        for d in range(2):
            s_scr[2 * hd + d] = s0_ref[d, hd] if has_s0 else jnp.zeros((dk, GDN_DV), F32)

    ri = lax.broadcasted_iota(jnp.int32, (sb, sb), 0)
    ci = lax.broadcasted_iota(jnp.int32, (sb, sb), 1)
    same_chunk = _chunk_of(ri) == _chunk_of(ci)
    eye = jnp.where(ri == ci, 1.0, 0.0).astype(F32)
    n_levels = int(math.log2(ck))

    def level_mask(l):
        return ((ri >> (l + 1)) == (ci >> (l + 1))) & ((ri >> l) != (ci >> l))

    def conv_silu(a, w_ref, hd, r0):
        xp = pad_scr[a, pl.ds(r0, sb + 2 * halo), :]
        acc = None
        for j in range(GDN_CONV):
            shift = (GDN_CONV // 2 - j) % (sb + 2 * halo)
            tap = xp if shift == 0 else pltpu.roll(xp, shift, 0)
            term = w_ref[j:j + 1, hd * dk:(hd + 1) * dk] * tap[halo:halo + sb, :]
            acc = term if acc is None else acc + term
        return _silu(acc)

    chains = [(hd, d) for hd in range(hps) for d in range(2)]

    a_unroll = max(1, min(n_super, 8 // (2 * hps)))

    def phase_a(it):
        units = [(it * a_unroll + u, hd) for u in range(a_unroll) for hd in range(hps)]
        r0s = [pl.multiple_of(s * sb, sb) for s, _ in units]
        qs, ks, vs, ps, qks = [], [], [], [], []
        for (s, hd), r0 in zip(units, r0s):
            q = conv_silu(3 * hd, cq_ref, hd, r0)
            k = conv_silu(3 * hd + 1, ck_ref, hd, r0)
            v = conv_silu(3 * hd + 2, cv_ref, hd, r0)
            q = q * lax.rsqrt(jnp.sum(q * q, axis=-1, keepdims=True) + NORM_EPS) * (dk ** -0.5)
            k = k * lax.rsqrt(jnp.sum(k * k, axis=-1, keepdims=True) + NORM_EPS)
            kn_scr[hd, pl.ds(r0, sb), :] = k
            kb = k.astype(BF16)
            qs.append(q)
            ks.append(k)
            vs.append(v)
            ps.append(_mm_nt(kb, kb))
            qks.append(_mm_nt(q, kb))
        a_chains = [(n, d) for n in range(len(units)) for d in range(2)]
        a_mats, ts, rhs = [], [], []
        for n, d in a_chains:
            (s, hd), r0 = units[n], r0s[n]
            hdd = 2 * hd + d
            gcol = gc_ref[hd, pl.ds(r0, sb), d:d + 1]
            bcol = gc_ref[hd, pl.ds(r0, sb), 2 + d:3 + d]
            grow = gr_ref[hd, s, d:d + 1, :]
            if d == 0:
                mask, strict = same_chunk & (ri >= ci), same_chunk & (ri > ci)
            else:
                mask, strict = same_chunk & (ri <= ci), same_chunk & (ri < ci)
            dec = jnp.exp(jnp.where(mask, gcol - grow, -jnp.inf))
            a = jnp.where(strict, ps[n] * bcol * dec, 0.0)
            a_mats.append(a)
            ts.append(eye - jnp.where(level_mask(0), a, 0.0))
            eg = jnp.exp(gcol)
            rhs.append(jnp.concatenate([vs[n] * bcol, ks[n] * (bcol * eg)], axis=1).astype(BF16))
            qe_scr[hdd, pl.ds(r0, sb), :] = qs[n] * eg
            qkd = qks[n] * dec
            for c in range(CHUNKS_PER_SUPER):
                qk_scr[hdd, pl.ds(r0 + c * ck, ck), :] = qkd[c * ck:(c + 1) * ck, c * ck:(c + 1) * ck]
        tbs = [t.astype(BF16) for t in ts]
        for l in range(1, n_levels):
            lm = level_mask(l)
            ys = [_mm(jnp.where(lm, a_mats[i], 0.0), tbs[i]) for i in range(len(a_chains))]
            ts = [ts[i] - _mm(tbs[i], ys[i]) for i in range(len(a_chains))]
            tbs = [t.astype(BF16) for t in ts]
        for i, (n, d) in enumerate(a_chains):
            (s, hd), r0 = units[n], r0s[n]
            uw = _mm(tbs[i], rhs[i])
            u_scr[2 * hd + d, pl.ds(r0, sb), :] = uw[:, :GDN_DV]
            w_scr[2 * hd + d, pl.ds(r0, sb), :] = uw[:, GDN_DV:]

    _loop(n_super // a_unroll, phase_a, max_static=1)

    def phase_b(i):
        rows, ws_qs, v_new, kws, gls = [], [], [], [], []
        for hd, d in chains:
            hdd = 2 * hd + d
            c = i if d == 0 else n_chunk - 1 - i
            r = pl.multiple_of(c * ck, ck)
            r_last = r + (ck - 1) if d == 0 else r
            rows.append(r)
            gcol = gc_ref[hd, pl.ds(r, ck), d:d + 1]
            gl = gc_ref[hd, pl.ds(r_last, 1), d:d + 1]
            gls.append(gl)
            kws.append((kn_scr[hd, pl.ds(r, ck), :] * jnp.exp(gl - gcol)).T)
            wq = jnp.concatenate([w_scr[hdd, pl.ds(r, ck), :], qe_scr[hdd, pl.ds(r, ck), :]], axis=0)
            ws_qs.append(_mm(wq, s_scr[hdd]))
        for j, (hd, d) in enumerate(chains):
            hdd = 2 * hd + d
            vn = u_scr[hdd, pl.ds(rows[j], ck), :] - ws_qs[j][:ck]
            v_new.append(vn)
            oo_scr[hdd, pl.ds(rows[j], ck), :] = ws_qs[j][ck:] + _mm(qk_scr[hdd, pl.ds(rows[j], ck), :], vn)
        for j, (hd, d) in enumerate(chains):
            hdd = 2 * hd + d
            s_scr[hdd] = s_scr[hdd] * jnp.exp(gls[j]) + _mm(kws[j], v_new[j])

    _loop(n_chunk, phase_b)

    def finish(s):
        r0 = pl.multiple_of(s * sb, sb)
        for hd in range(hps):
            o = oo_scr[2 * hd, pl.ds(r0, sb), :] + oo_scr[2 * hd + 1, pl.ds(r0, sb), :]
            z = z_ref[pl.ds(r0, sb), hd * GDN_DV:(hd + 1) * GDN_DV].astype(F32)
            o_ref[pl.ds(r0, sb), hd * GDN_DV:(hd + 1) * GDN_DV] = _rms(o, ng_ref[...]) * _silu(z)

    _loop(n_super, finish)
    for hd in range(hps):
        for d in range(2):
            st_ref[d, hd] = s_scr[2 * hd + d]


def _gdn_core_call(qkvz, conv_w, gc, gr, norm_g, s0, *, batch, n_tok, row_block_off, hps, name):
    has_s0 = s0 is not None
    h = GDN_HEADS
    dk = GDN_DK
    n_super = n_tok // GDN_SUPER
    hb = h // hps

    mode = dict(pipeline_mode=pl.Buffered(1)) if n_tok > GDN_SUPER else {}

    def col(sec):
        return pl.BlockSpec((n_tok, dk * hps), lambda b, hh: (b + row_block_off, sec * hb + hh), **mode)

    def cw(sec):
        return pl.BlockSpec((GDN_CONV, dk * hps), lambda b, hh: (0, sec * hb + hh))

    in_specs = [col(0), col(1), col(2), col(3), cw(0), cw(1), cw(2),
                pl.BlockSpec((None, hps, n_tok, 4), lambda b, hh: (b, hh, 0, 0)),
                pl.BlockSpec((None, hps, n_super, 4, GDN_SUPER), lambda b, hh: (b, hh, 0, 0, 0)),
                pl.BlockSpec((1, GDN_DV), lambda b, hh: (0, 0))]
    args = [qkvz, qkvz, qkvz, qkvz, conv_w, conv_w, conv_w, gc, gr, norm_g.reshape(1, GDN_DV)]
    st_spec = pl.BlockSpec((None, 2, hps, dk, GDN_DV), lambda b, hh: (b, 0, hh, 0, 0))
    if has_s0:
        in_specs.append(st_spec)
        args.append(s0)
    return pl.pallas_call(
        functools.partial(_gdn_core_kernel, n_tok=n_tok, has_s0=has_s0, hps=hps),
        grid=(batch, hb),
        in_specs=in_specs,
        out_specs=[pl.BlockSpec((n_tok, GDN_DV * hps), lambda b, hh: (b, hh)), st_spec],
        out_shape=[jax.ShapeDtypeStruct((batch * n_tok, h * GDN_DV), F32),
                   jax.ShapeDtypeStruct((batch, 2, h, dk, GDN_DV), F32)],
        scratch_shapes=[
            pltpu.VMEM((3 * hps, n_tok + 2 * V7X_SUBLANES, dk), F32),
            pltpu.VMEM((hps, n_tok, dk), F32),
            pltpu.VMEM((2 * hps, n_tok, GDN_DV), F32),
            pltpu.VMEM((2 * hps, n_tok, dk), F32),
            pltpu.VMEM((2 * hps, n_tok, dk), F32),
            pltpu.VMEM((2 * hps, n_tok, GDN_CHUNK), F32),
            pltpu.VMEM((2 * hps, n_tok, GDN_DV), F32),
            pltpu.VMEM((2 * hps, dk, GDN_DV), F32),
        ],
        compiler_params=_cparams("arbitrary", "arbitrary"),
        name=name,
    )(*args)


def _gdn_layer(x, mods_l, li, state_gdn, gdn_w_in, gdn_conv, gdn_a_log, gdn_dt_bias, gdn_norm_g):
    w_ab = jnp.pad(gdn_w_in[li, :, GDN_QKVZ:], ((0, 0), (0, V7X_LANES - 4 * GDN_HEADS)))
    qkvz, ab = _modproj_call(x, mods_l, 0, 1, gdn_w_in, GDN_QKVZ, w_ab, w_layer=li, tn=1024, out_dtype=BF16,
                             name="gdn_in")
    gates = _gdn_gates_call(ab, gdn_a_log[li], gdn_dt_bias[li])[:, :4 * GDN_HEADS]

    def head_layouts(g, batch, n_tok):
        g = g.reshape(batch, n_tok, 4, GDN_HEADS)
        gc = jnp.transpose(g, (0, 3, 1, 2))
        gr = jnp.transpose(g, (0, 3, 2, 1)).reshape(batch, GDN_HEADS, 4, n_tok // GDN_SUPER, GDN_SUPER)
        return gc, jnp.transpose(gr, (0, 1, 3, 2, 4))

    gc_p, gr_p = head_layouts(gates[:T_CTX], BATCH, SEQ)
    gc_s, gr_s = head_layouts(gates[T_CTX:], DEC_BATCH, DEC_SEQ)
    o_p, st_p = _gdn_core_call(qkvz, gdn_conv[li], gc_p, gr_p, gdn_norm_g[li], None,
                               batch=BATCH, n_tok=SEQ, row_block_off=0, hps=4, name="gdn_core_ctx")
    o_s, _ = _gdn_core_call(qkvz, gdn_conv[li], gc_s, gr_s, gdn_norm_g[li], state_gdn[:, li],
                            batch=DEC_BATCH, n_tok=DEC_SEQ, row_block_off=T_CTX // DEC_SEQ, hps=2,
                            name="gdn_core_lat")
    return (o_p, o_s), st_p


def _rope_tables(n_tokens, rot_dim, reps):
    n_rows = n_tokens // GRID_W
    row = jnp.repeat(jnp.arange(n_rows, dtype=F32), GRID_W)
    colp = jnp.tile(jnp.arange(GRID_W, dtype=F32), n_rows)
    n_freq = rot_dim // 4
    inv_freq = ROPE_THETA ** (-jnp.arange(n_freq, dtype=F32) / n_freq)
    ra, ca = row[:, None] * inv_freq, colp[:, None] * inv_freq
    cos = jnp.concatenate([jnp.cos(ra), jnp.cos(ra), jnp.cos(ca), jnp.cos(ca)], axis=-1)
    sin = jnp.concatenate([-jnp.sin(ra), jnp.sin(ra), -jnp.sin(ca), jnp.sin(ca)], axis=-1)
    return jnp.tile(cos, (1, reps)), jnp.tile(sin, (1, reps))


def _rope(x, cos, sin, quarter):
    n = x.shape[-1]
    lane = lax.broadcasted_iota(jnp.int32, x.shape, x.ndim - 1)
    first = (lane & (2 * quarter - 1)) < quarter
    partner = jnp.where(first, pltpu.roll(x, n - quarter, x.ndim - 1), pltpu.roll(x, quarter, x.ndim - 1))
    return x * cos + partner * sin


def _gqa_prep_kernel(kv_ref, g_ref, *rest, rope):
    if rope:
        cos_ref, sin_ref, k_ref, v_ref = rest
    else:
        k_ref, v_ref = rest
    kv = kv_ref[...]
    nk = GQA_KV_HEADS * GQA_HEAD_DIM
    for hh in range(GQA_KV_HEADS):
        k = _rms(kv[:, hh * GQA_HEAD_DIM:(hh + 1) * GQA_HEAD_DIM], g_ref[...])
        if rope:
            k = _rope(k, cos_ref[...], sin_ref[...], GQA_HEAD_DIM // 4)
        k_ref[:, hh * GQA_HEAD_DIM:(hh + 1) * GQA_HEAD_DIM] = k
    v_ref[...] = kv[:, nk:]


def _gqa_prep_call(proj, k_norm, row_off, n_rows, tables, *, tm=512, name):
    rope = tables is not None
    nkv = 2 * GQA_KV_HEADS * GQA_HEAD_DIM
    kv_col = GQA_HEADS * GQA_HEAD_DIM // nkv
    off = row_off // tm
    in_specs = [pl.BlockSpec((tm, nkv), lambda i: (i + off, kv_col)),
                pl.BlockSpec((1, GQA_HEAD_DIM), lambda i: (0, 0))]
    args = [proj, k_norm.reshape(1, GQA_HEAD_DIM)]
    if rope:
        per_seq = DEC_SEQ // tm
        in_specs += [pl.BlockSpec((tm, GQA_HEAD_DIM), lambda i: (i % per_seq, 0))] * 2
        args += list(tables)
    half = nkv // 2
    return pl.pallas_call(
        functools.partial(_gqa_prep_kernel, rope=rope),
        grid=(n_rows // tm,),
        in_specs=in_specs,
        out_specs=[pl.BlockSpec((tm, half), lambda i: (i, 0))] * 2,
        out_shape=[jax.ShapeDtypeStruct((n_rows, half), F32)] * 2,
        compiler_params=_cparams("arbitrary"),
        name=name,
    )(*args)


def _attend(chains, scale):
    scores = [[_mm_nt(q, k) * scale for k in ks] for q, ks, _ in chains]
    probs, dens = [], []
    for sc in scores:
        m = None
        for s in sc:
            sm = jnp.max(s, axis=-1, keepdims=True)
            m = sm if m is None else jnp.maximum(m, sm)
        ps = [jnp.exp(s - m) for s in sc]
        den = None
        for p in ps:
            r = jnp.sum(p, axis=-1, keepdims=True)
            den = r if den is None else den + r
        probs.append([p.astype(BF16) for p in ps])
        dens.append(den)
    outs = []
    for (_, _, vs), ps, den in zip(chains, probs, dens):
        acc = None
        for p, v in zip(ps, vs):
            pv = _mm(p, v)
            acc = pv if acc is None else acc + pv
        outs.append(acc / den)
    return outs


def _gqa_attn_kernel(q_ref, qg_ref, *rest, n_seg, rope, kv_per_step):
    if rope:
        cos_ref, sin_ref = rest[:2]
        rest = rest[2:]
    seg_refs = rest[:2 * n_seg]
    o_ref = rest[2 * n_seg]
    dh = GQA_HEAD_DIM
    tq = q_ref.shape[0]
    heads_per_chain = GQA_GROUP // 2
    chains, heads = [], []
    for kh in range(kv_per_step):
        ks = [seg_refs[2 * sgi][:, kh * dh:(kh + 1) * dh].astype(BF16) for sgi in range(n_seg)]
        vs = [seg_refs[2 * sgi + 1][:, kh * dh:(kh + 1) * dh].astype(BF16) for sgi in range(n_seg)]
        for g0 in range(0, GQA_GROUP, heads_per_chain):
            qs = []
            for g in range(g0, g0 + heads_per_chain):
                c0 = (kh * GQA_GROUP + g) * dh
                qh = _rms(q_ref[:, c0:c0 + dh], qg_ref[...])
                if rope:
                    qh = _rope(qh, cos_ref[...], sin_ref[...], dh // 4)
                qs.append(qh.astype(BF16))
            chains.append((jnp.concatenate(qs, axis=0), ks, vs))
            heads.append(kh * GQA_GROUP + g0)
    for o, h0 in zip(_attend(chains, dh ** -0.5), heads):
        for g in range(heads_per_chain):
            o_ref[:, (h0 + g) * dh:(h0 + g + 1) * dh] = o[g * tq:(g + 1) * tq]


def _gqa_attn_call(proj, q_norm, segs, tables, *, batch, n_q, row_off, tq, kv_per_step, name):
    rope = tables is not None
    dh = GQA_HEAD_DIM
    qw = GQA_GROUP * dh * kv_per_step
    kw = dh * kv_per_step
    nqb = n_q // tq
    off = row_off // tq
    in_specs = [pl.BlockSpec((tq, qw), lambda b, h, i: (b * nqb + i + off, h)),
                pl.BlockSpec((1, dh), lambda b, h, i: (0, 0))]
    args = [proj, q_norm.reshape(1, dh)]
    if rope:
        in_specs += [pl.BlockSpec((tq, dh), lambda b, h, i: (i, 0))] * 2
        args += list(tables)
    for k2d, v2d, rows in segs:
        in_specs += [pl.BlockSpec((rows, kw), lambda b, h, i: (b, h))] * 2
        args += [k2d, v2d]
    return pl.pallas_call(
        functools.partial(_gqa_attn_kernel, n_seg=len(segs), rope=rope, kv_per_step=kv_per_step),
        grid=(batch, GQA_KV_HEADS // kv_per_step, nqb),
        in_specs=in_specs,
        out_specs=pl.BlockSpec((tq, qw), lambda b, h, i: (b * nqb + i, h)),
        out_shape=jax.ShapeDtypeStruct((batch * n_q, GQA_HEADS * dh), F32),
        compiler_params=_cparams("arbitrary", "arbitrary", "arbitrary"),
        name=name,
    )(*args)


def _gqa_layer(x, mods_l, li, cache_k, cache_v, gqa_w_in, gqa_q_norm, gqa_k_norm):
    proj = _modproj_call(x, mods_l, 0, 1, gqa_w_in, GQA_IN, w_layer=li, name="gqa_in")
    tables = _rope_tables(DEC_SEQ, GQA_HEAD_DIM, 1)
    nkv = GQA_KV_HEADS * GQA_HEAD_DIM
    k_p, v_p = _gqa_prep_call(proj, gqa_k_norm[li], 0, T_CTX, None, name="gqa_prep_ctx")
    k_s, v_s = _gqa_prep_call(proj, gqa_k_norm[li], T_CTX, T_LAT, tables, name="gqa_prep_lat")
    o_p = _gqa_attn_call(proj, gqa_q_norm[li], [(k_p, v_p, SEQ)], None, batch=BATCH, n_q=SEQ, row_off=0,
                         tq=SEQ, kv_per_step=GQA_KV_HEADS, name="gqa_attn_ctx")
    ck = cache_k[:, li].reshape(DEC_BATCH * PAST_LEN, nkv)
    cv = cache_v[:, li].reshape(DEC_BATCH * PAST_LEN, nkv)
    o_s = _gqa_attn_call(proj, gqa_q_norm[li], [(ck, cv, PAST_LEN), (k_s, v_s, DEC_SEQ)], tables,
                         batch=DEC_BATCH, n_q=DEC_SEQ, row_off=T_CTX, tq=256, kv_per_step=1, name="gqa_attn_lat")
    new_k = k_p.reshape(BATCH, SEQ, GQA_KV_HEADS, GQA_HEAD_DIM)
    new_v = v_p.reshape(BATCH, SEQ, GQA_KV_HEADS, GQA_HEAD_DIM)
    return (o_p, o_s), new_k, new_v


def _krope_kernel(x_ref, *rest, rope):
    if rope:
        cos_ref, sin_ref, o_ref = rest
    else:
        (o_ref,) = rest
    x = x_ref[...]
    x2 = jnp.concatenate([x, x], axis=1)
    if rope:
        x2 = _rope(x2, cos_ref[...], sin_ref[...], MLA_ROPE // 4)
    o_ref[...] = x2


def _krope_call(krope, tables, *, tm=512, name):
    rope = tables is not None
    n = krope.shape[0]
    in_specs = [pl.BlockSpec((tm, MLA_ROPE), lambda i: (i, 0))]
    args = [krope]
    if rope:
        per_seq = DEC_SEQ // tm
        in_specs += [pl.BlockSpec((tm, 2 * MLA_ROPE), lambda i: (i % per_seq, 0))] * 2
        args += list(tables)
    return pl.pallas_call(
        functools.partial(_krope_kernel, rope=rope),
        grid=(n // tm,),
        in_specs=in_specs,
        out_specs=pl.BlockSpec((tm, 2 * MLA_ROPE), lambda i: (i, 0)),
        out_shape=jax.ShapeDtypeStruct((n, 2 * MLA_ROPE), F32),
        compiler_params=_cparams("arbitrary"),
        name=name,
    )(*args)


def _mla_attn_kernel(qn_ref, qr_ref, *rest, n_seg, rope, heads_per_step):
    if rope:
        cos_ref, sin_ref = rest[:2]
        rest = rest[2:]
    seg_refs = rest[:2 * n_seg]
    o_ref = rest[2 * n_seg]
    scale = (MLA_NOPE + MLA_ROPE) ** -0.5
    h0 = pl.program_id(1) * heads_per_step
    pair_w = 2 * MLA_ROPE
    tq = qn_ref.shape[0]
    row_splits = 2 if heads_per_step == 1 else 1
    rows = tq // row_splits
    chains = []
    for j in range(heads_per_step):
        qn = qn_ref[:, j * MLA_NOPE:(j + 1) * MLA_NOPE]
        pair = qr_ref[:, (j // 2) * pair_w:(j // 2 + 1) * pair_w]
        if rope:
            pair = _rope(pair, cos_ref[...], sin_ref[...], MLA_ROPE // 4)
        lane = lax.broadcasted_iota(jnp.int32, pair.shape, 1)
        mine = (lane >> int(math.log2(MLA_ROPE))) == ((h0 + j) & 1)
        q = jnp.concatenate([qn, jnp.where(mine, pair, 0.0)], axis=1).astype(BF16)
        ks, vs = [], []
        for sgi in range(n_seg):
            kv = seg_refs[2 * sgi][:, j * (MLA_NOPE + MLA_V):(j + 1) * (MLA_NOPE + MLA_V)]
            ks.append(jnp.concatenate([kv[:, :MLA_NOPE], seg_refs[2 * sgi + 1][...]], axis=1).astype(BF16))
            vs.append(kv[:, MLA_NOPE:].astype(BF16))
        for c in range(row_splits):
            chains.append((q[c * rows:(c + 1) * rows], ks, vs))
    outs = _attend(chains, scale)
    for j in range(heads_per_step):
        for c in range(row_splits):
            o_ref[c * rows:(c + 1) * rows, j * MLA_V:(j + 1) * MLA_V] = outs[j * row_splits + c]


def _mla_attn_call(q, segs, tables, *, batch, n_q, row_off, tq, heads_per_step, name):
    rope = tables is not None
    hps = heads_per_step
    nqb = n_q // tq
    off = row_off // tq
    rope_w = max(2 * MLA_ROPE, MLA_ROPE * hps)
    rope_base = MLA_HEADS * MLA_NOPE // rope_w

    def rope_idx(b, h, i):
        return (b * nqb + i + off, rope_base + (h * hps * MLA_ROPE) // rope_w)

    in_specs = [pl.BlockSpec((tq, MLA_NOPE * hps), lambda b, h, i: (b * nqb + i + off, h)),
                pl.BlockSpec((tq, rope_w), rope_idx)]
    args = [q, q]
    if rope:
        in_specs += [pl.BlockSpec((tq, 2 * MLA_ROPE), lambda b, h, i: (i, 0))] * 2
        args += list(tables)
    for kv2d, kr2, rows, kv_row_off in segs:
        kv_off = kv_row_off // rows
        in_specs += [pl.BlockSpec((rows, (MLA_NOPE + MLA_V) * hps), lambda b, h, i, kv_off=kv_off: (b + kv_off, h)),
                     pl.BlockSpec((rows, 2 * MLA_ROPE), lambda b, h, i: (b, 0))]
        args += [kv2d, kr2]
    return pl.pallas_call(
        functools.partial(_mla_attn_kernel, n_seg=len(segs), rope=rope, heads_per_step=hps),
        grid=(batch, MLA_HEADS // hps, nqb),
        in_specs=in_specs,
        out_specs=pl.BlockSpec((tq, MLA_V * hps), lambda b, h, i: (b * nqb + i, h)),
        out_shape=jax.ShapeDtypeStruct((batch * n_q, MLA_HEADS * MLA_V), F32),
        compiler_params=_cparams("arbitrary", "arbitrary", "arbitrary"),
        name=name,
    )(*args)


def _mla_layer(x, mods_l, li, cache_ckv, cache_krope, mla_w_in, mla_q_norm, mla_kv_norm, mla_w_q_up, mla_w_kv_up):
    w_in = jnp.pad(mla_w_in[li], ((0, 0), (0, 7 * V7X_LANES - MLA_IN)))
    proj = _modproj_call(x, mods_l, 0, 1, w_in, 7 * V7X_LANES, tn=7 * V7X_LANES, name="mla_in")
    wq = mla_w_q_up[li].reshape(MLA_Q_LORA, MLA_HEADS, MLA_NOPE + MLA_ROPE)
    wq = jnp.concatenate([wq[:, :, :MLA_NOPE].reshape(MLA_Q_LORA, -1), wq[:, :, MLA_NOPE:].reshape(MLA_Q_LORA, -1)], axis=1)
    q = _rmsproj_call(proj, 0, MLA_Q_LORA, mla_q_norm[li], wq, name="mla_q_up")
    kv, ckv = _rmsproj_call(proj, MLA_Q_LORA // MLA_KV_LORA, MLA_KV_LORA, mla_kv_norm[li], mla_w_kv_up[li],
                            emit_xn=True, name="mla_kv_up")
    krope = proj[:, MLA_Q_LORA + MLA_KV_LORA:MLA_IN]
    tables = _rope_tables(DEC_SEQ, MLA_ROPE, 2)
    kr2_p = _krope_call(krope[:T_CTX], None, name="mla_krope_ctx")
    kr2_s = _krope_call(krope[T_CTX:], tables, name="mla_krope_lat")
    ctx_ckv = cache_ckv[:, li].reshape(DEC_BATCH * PAST_LEN, MLA_KV_LORA)
    kv_c = _rmsproj_call(ctx_ckv, 0, MLA_KV_LORA, jnp.ones((MLA_KV_LORA,), F32), mla_w_kv_up[li], normalize=False,
                         name="mla_kv_up_cache")
    kr2_c = _krope_call(cache_krope[:, li].reshape(DEC_BATCH * PAST_LEN, MLA_ROPE), None, name="mla_krope_cache")
    o_p = _mla_attn_call(q, [(kv, kr2_p, SEQ, 0)], None, batch=BATCH, n_q=SEQ, row_off=0, tq=SEQ,
                         heads_per_step=MLA_HEADS, name="mla_attn_ctx")
    o_s = _mla_attn_call(q, [(kv_c, kr2_c, PAST_LEN, 0), (kv, kr2_s, DEC_SEQ, T_CTX)], tables, batch=DEC_BATCH,
                         n_q=DEC_SEQ, row_off=T_CTX, tq=512, heads_per_step=1, name="mla_attn_lat")
    new_ckv = ckv[:T_CTX].reshape(BATCH, SEQ, MLA_KV_LORA)
    new_krope = krope[:T_CTX].reshape(BATCH, SEQ, MLA_ROPE)
    return (o_p, o_s), new_ckv, new_krope


def _route(h, rw_ref, rb_ref):
    logits = lax.dot_general(rw_ref[...], h, (((1,), (1,)), ((), ())), precision=lax.Precision.HIGHEST,
                             preferred_element_type=F32)[:N_EXPERTS]
    scores = _sigmoid(logits)
    sel = scores + rb_ref[0:N_EXPERTS, :]
    tm = sel.shape[1]
    neg = -jnp.inf
    gs = []
    for g in range(N_GROUPS):
        a, b, c, d = (sel[EXPERTS_PER_GROUP * g + j:EXPERTS_PER_GROUP * g + j + 1, :] for j in range(4))
        gs.append(jnp.maximum(jnp.maximum(jnp.maximum(a + b, a + c), jnp.maximum(a + d, b + c)),
                              jnp.maximum(b + d, c + d)))
    gmax = jnp.maximum(jnp.maximum(gs[0], gs[1]), jnp.maximum(gs[2], gs[3]))
    best = jnp.where(gs[0] == gmax, 0, jnp.where(gs[1] == gmax, 1, jnp.where(gs[2] == gmax, 2, 3)))
    eid = lax.broadcasted_iota(jnp.int32, (N_EXPERTS, tm), 0)
    masked = jnp.where((eid >> int(math.log2(EXPERTS_PER_GROUP))) == best, sel, neg)
    m1 = jnp.max(masked, axis=0, keepdims=True)
    i1 = jnp.min(jnp.where(masked == m1, eid, N_EXPERTS), axis=0, keepdims=True)
    masked2 = jnp.where(eid == i1, neg, masked)
    m2 = jnp.max(masked2, axis=0, keepdims=True)
    i2 = jnp.min(jnp.where(masked2 == m2, eid, N_EXPERTS), axis=0, keepdims=True)
    w1 = jnp.sum(jnp.where(eid == i1, scores, 0.0), axis=0, keepdims=True)
    w2 = jnp.sum(jnp.where(eid == i2, scores, 0.0), axis=0, keepdims=True)
    tot = w1 + w2
    comb = jnp.where(eid == i1, w1 / tot, 0.0) + jnp.where(eid == i2, w2 / tot, 0.0)
    comb = jnp.concatenate([comb, jnp.zeros((V7X_LANES - N_EXPERTS, tm), F32)], axis=0)
    return comb.T


def _moe_kernel(x_ref, sh_ref, sc_ref, gate_ref, comb_ref, wg_ref, wu_ref, wd_ref, g_ref, b_ref, *rest,
                tiles_per_seg, split_out):
    if split_out:
        yc_ref, yl_ref, h_scr, acc_scr = rest
    else:
        y_ref, h_scr, acc_scr = rest
    i, e = pl.program_id(0), pl.program_id(1)
    r = _seg_row(i, tiles_per_seg)

    @pl.when(e == 0)
    def _():
        h_scr[...] = (x_ref[...] * (1.0 + sc_ref[pl.ds(r, 1), :]) + sh_ref[pl.ds(r, 1), :]).astype(BF16)
        acc_scr[...] = jnp.zeros_like(acc_scr)

    comb = comb_ref[...]
    lane = lax.broadcasted_iota(jnp.int32, comb.shape, 1)
    h = h_scr[...]
    epb = wg_ref.shape[0]
    f = None
    for j in range(epb):
        c_e = jnp.sum(jnp.where(lane == e * epb + j, comb, 0.0), axis=1, keepdims=True)
        hid = _silu(_mm(h, wg_ref[j])) * _mm(h, wu_ref[j]) * c_e
        fj = _mm(hid, wd_ref[j])
        f = fj if f is None else f + fj
    acc_scr[...] += f

    def result():
        gate = gate_ref[pl.ds(r, 1), :]
        return _layer_norm(ALPHA * x_ref[...] + (1.0 + gate) * acc_scr[...], g_ref[...], b_ref[...])

    last = e == pl.num_programs(1) - 1
    if split_out:
        _slab_write(i, last, yc_ref, yl_ref, result)
    else:
        @pl.when(last)
        def _():
            y_ref[...] = result()


MOE_EXPERTS_PER_STEP = 2


def _moe_call(x, mods_l, comb, w_gate, w_up, w_down, layer, g, b, *, split_out, tm=1024):
    t = T_ALL
    epb = MOE_EXPERTS_PER_STEP
    if split_out:
        out_specs, out_shape = _slab_specs(tm, 2), _slab_shapes()
    else:
        out_specs = pl.BlockSpec((tm, D_MODEL), lambda i, e: (i, 0))
        out_shape = jax.ShapeDtypeStruct((t, D_MODEL), F32)
    return pl.pallas_call(
        functools.partial(_moe_kernel, tiles_per_seg=SEG_ROWS // tm, split_out=split_out),
        grid=(t // tm, N_EXPERTS // epb),
        in_specs=[
            pl.BlockSpec((tm, D_MODEL), lambda i, e: (i, 0)),
            pl.BlockSpec((MOD_ROWS, D_MODEL), lambda i, e: (0, 3)),
            pl.BlockSpec((MOD_ROWS, D_MODEL), lambda i, e: (0, 4)),
            pl.BlockSpec((MOD_ROWS, D_MODEL), lambda i, e: (0, 5)),
            pl.BlockSpec((tm, V7X_LANES), lambda i, e: (i, 0)),
            pl.BlockSpec((None, epb, D_MODEL, D_EXPERT), lambda i, e: (layer, e, 0, 0)),
            pl.BlockSpec((None, epb, D_MODEL, D_EXPERT), lambda i, e: (layer, e, 0, 0)),
            pl.BlockSpec((None, epb, D_EXPERT, D_MODEL), lambda i, e: (layer, e, 0, 0)),
            pl.BlockSpec((1, D_MODEL), lambda i, e: (0, 0)),
            pl.BlockSpec((1, D_MODEL), lambda i, e: (0, 0)),
        ],
        out_specs=out_specs,
        out_shape=out_shape,
        scratch_shapes=[pltpu.VMEM((tm, D_MODEL), BF16), pltpu.VMEM((tm, D_MODEL), F32)],
        compiler_params=_cparams("arbitrary", "arbitrary"),
        name="moe_experts",
    )(x, mods_l, mods_l, mods_l, comb, w_gate, w_up, w_down, g.reshape(1, D_MODEL), b.reshape(1, D_MODEL))


def kernel(x_prompt, x_sample, state_gdn, cache_gqa_k, cache_gqa_v, cache_mla_ckv, cache_mla_krope, c, c_ctx, w_ada, b_ada, ln_mix_g, ln_mix_b, ln_ffn_g, ln_ffn_b, gdn_w_in, gdn_conv, gdn_a_log, gdn_dt_bias, gdn_norm_g, gdn_w_out, gqa_w_in, gqa_q_norm, gqa_k_norm, gqa_w_out, mla_w_in, mla_q_norm, mla_kv_norm, mla_w_q_up, mla_w_kv_up, mla_w_out, router_w, router_b, moe_w_gate, moe_w_up, moe_w_down):
    x = jnp.concatenate([x_prompt.reshape(T_CTX, D_MODEL), x_sample.reshape(T_LAT, D_MODEL)], axis=0)
    cond = jnp.concatenate([c_ctx[None], c, jnp.zeros((MOD_ROWS - 1 - DEC_BATCH, D_MODEL), F32)], axis=0)
    mods = _ada_call(cond, w_ada, b_ada)

    st_gdn, st_k, st_v, st_ckv, st_krope = [], [], [], [], []
    for l in range(DEPTH):
        kind, li = l % N_MIXERS, l // N_MIXERS
        mods_l = mods[l]
        if kind == 0:
            o, st = _gdn_layer(x, mods_l, li, state_gdn, gdn_w_in, gdn_conv, gdn_a_log, gdn_dt_bias, gdn_norm_g)
            st_gdn.append(st)
            w_out = gdn_w_out
        elif kind == 1:
            o, nk, nv = _gqa_layer(x, mods_l, li, cache_gqa_k, cache_gqa_v, gqa_w_in, gqa_q_norm, gqa_k_norm)
            st_k.append(nk)
            st_v.append(nv)
            w_out = gqa_w_out
        else:
            o, nckv, nkr = _mla_layer(x, mods_l, li, cache_mla_ckv, cache_mla_krope, mla_w_in, mla_q_norm,
                                      mla_kv_norm, mla_w_q_up, mla_w_kv_up)
            st_ckv.append(nckv)
            st_krope.append(nkr)
            w_out = mla_w_out
        x, comb = _outln_call(o, w_out, li, x, mods_l, ln_mix_g[l], ln_mix_b[l], router_w, router_b,
                              name="mix_out_ln")
        x = _moe_call(x, mods_l, comb, moe_w_gate, moe_w_up, moe_w_down, l, ln_ffn_g[l], ln_ffn_b[l],
                      split_out=(l == DEPTH - 1))

    y_prompt = x[0].reshape(BATCH, SEQ, D_MODEL)
    y_sample = x[1].reshape(DEC_BATCH, DEC_SEQ, D_MODEL)
    return (y_prompt, y_sample, jnp.stack(st_gdn, axis=1), jnp.stack(st_k, axis=1), jnp.stack(st_v, axis=1),
            jnp.stack(st_ckv, axis=1), jnp.stack(st_krope, axis=1))
```

```python
import functools
import math

import jax
import jax.numpy as jnp
from jax import lax
from jax.experimental import pallas as pl
from jax.experimental.pallas import tpu as pltpu

F32 = jnp.float32
BF16 = jnp.bfloat16

D_MODEL = 1024
BATCH = 32
SEQ = 256
DEPTH = 4
DEC_BATCH = 2
DEC_SEQ = 2048
PAST_LEN = 512
GRID_W = 64
N_MIXERS = 3

GDN_HEADS = 8
GDN_DK = 128
GDN_DV = 128
GDN_CONV = 5
GDN_CHUNK = 64
GDN_QKV = GDN_HEADS * (2 * GDN_DK + GDN_DV)
GDN_QKVZ = GDN_QKV + GDN_HEADS * GDN_DV

GQA_HEADS = 8
GQA_KV_HEADS = 2
GQA_GROUP = GQA_HEADS // GQA_KV_HEADS
GQA_HEAD_DIM = 128
GQA_IN = (GQA_HEADS + 2 * GQA_KV_HEADS) * GQA_HEAD_DIM

MLA_HEADS = 8
MLA_Q_LORA = 512
MLA_KV_LORA = 256
MLA_NOPE = 128
MLA_ROPE = 64
MLA_V = 128
MLA_IN = MLA_Q_LORA + MLA_KV_LORA + MLA_ROPE

N_EXPERTS = 16
N_GROUPS = 4
EXPERTS_PER_GROUP = N_EXPERTS // N_GROUPS
D_EXPERT = 256

ROPE_THETA = 10000.0
ALPHA = (2 * DEPTH) ** 0.25
NORM_EPS = 1e-6

T_CTX = BATCH * SEQ
T_LAT = DEC_BATCH * DEC_SEQ
T_ALL = T_CTX + T_LAT
SEG_ROWS = DEC_SEQ
N_CTX_SEGS = T_CTX // SEG_ROWS
MOD_ROWS = 8

V7X_LANES = 128
V7X_SUBLANES = 8
V7X_VMEM_BYTES = 64 * 1024 * 1024
VMEM_LIMIT = V7X_VMEM_BYTES * 7 // 8

GDN_SUPER = 256
CHUNKS_PER_SUPER = GDN_SUPER // GDN_CHUNK


def _cparams(*sem):
    return pltpu.CompilerParams(dimension_semantics=sem, vmem_limit_bytes=VMEM_LIMIT)


def _mm(a, b):
    return jnp.dot(a.astype(BF16), b.astype(BF16), preferred_element_type=F32)


def _mm_nt(a, b):
    return lax.dot_general(a.astype(BF16), b.astype(BF16), (((1,), (1,)), ((), ())),
                           preferred_element_type=F32)


def _silu(x):
    return x * (1.0 / (1.0 + jnp.exp(-x)))


def _sigmoid(x):
    return 1.0 / (1.0 + jnp.exp(-x))


def _chunk_of(idx):
    return idx >> int(math.log2(GDN_CHUNK))


def _seg_row(i, tiles_per_seg):
    return jnp.maximum(lax.div(i, tiles_per_seg) - (N_CTX_SEGS - 1), 0)


def _layer_norm(y, g, b):
    mu = jnp.mean(y, axis=-1, keepdims=True)
    yc = y - mu
    var = jnp.mean(yc * yc, axis=-1, keepdims=True)
    return yc * lax.rsqrt(var + NORM_EPS) * g + b


def _rms(x, g):
    return x * lax.rsqrt(jnp.mean(x * x, axis=-1, keepdims=True) + NORM_EPS) * g


def _slab_specs(tm, n_grid_axes, **mode):
    ctx_tiles = T_CTX // tm
    if n_grid_axes == 1:
        return [pl.BlockSpec((tm, D_MODEL), lambda i: (jnp.minimum(i, ctx_tiles - 1), 0), **mode),
                pl.BlockSpec((tm, D_MODEL), lambda i: (jnp.maximum(i - ctx_tiles, 0), 0), **mode)]
    return [pl.BlockSpec((tm, D_MODEL), lambda i, j: (jnp.minimum(i, ctx_tiles - 1), 0), **mode),
            pl.BlockSpec((tm, D_MODEL), lambda i, j: (jnp.maximum(i - ctx_tiles, 0), 0), **mode)]


def _slab_shapes():
    return [jax.ShapeDtypeStruct((T_CTX, D_MODEL), F32), jax.ShapeDtypeStruct((T_LAT, D_MODEL), F32)]


def _slab_read(i, xc_ref, xl_ref):
    return jnp.where(i < T_CTX // xc_ref.shape[0], xc_ref[...], xl_ref[...])


def _slab_write(i, when, yc_ref, yl_ref, value_fn):
    ctx_tiles = T_CTX // yc_ref.shape[0]

    @pl.when(when & (i < ctx_tiles))
    def _():
        yc_ref[...] = value_fn()

    @pl.when(when & (i >= ctx_tiles))
    def _():
        yl_ref[...] = value_fn()


def _ada_kernel(c_ref, w_ref, b_ref, o_ref):
    o_ref[...] = _mm(_silu(c_ref[...]), w_ref[...]) + b_ref[...]


def _ada_call(cond, w_ada, b_ada):
    tn = 1536
    n = w_ada.shape[-1]
    return pl.pallas_call(
        _ada_kernel,
        grid=(DEPTH, n // tn),
        in_specs=[
            pl.BlockSpec((MOD_ROWS, D_MODEL), lambda l, j: (0, 0)),
            pl.BlockSpec((None, D_MODEL, tn), lambda l, j: (l, 0, j)),
            pl.BlockSpec((None, 1, tn), lambda l, j: (l, 0, j)),
        ],
        out_specs=pl.BlockSpec((None, MOD_ROWS, tn), lambda l, j: (l, 0, j)),
        out_shape=jax.ShapeDtypeStruct((DEPTH, MOD_ROWS, n), F32),
        compiler_params=_cparams("arbitrary", "arbitrary"),
        name="ada_mod",
    )(cond, w_ada, b_ada.reshape(DEPTH, 1, n))


def _modproj_kernel(x_ref, sh_ref, sc_ref, w_ref, *rest, tiles_per_seg, has_extra):
    if has_extra:
        wx_ref, o_ref, ox_ref, h_scr, wb_scr = rest
    else:
        o_ref, h_scr, wb_scr = rest
    i, j = pl.program_id(0), pl.program_id(1)

    @pl.when(i == 0)
    def _():
        wb_scr[j] = w_ref[...].astype(BF16)

    @pl.when(j == 0)
    def _():
        r = _seg_row(i, tiles_per_seg)
        sh = sh_ref[pl.ds(r, 1), :]
        sc = sc_ref[pl.ds(r, 1), :]
        h = (x_ref[...] * (1.0 + sc) + sh).astype(BF16)
        h_scr[...] = h
        if has_extra:
            ox_ref[...] = _mm(h, wx_ref[...])

    o_ref[...] = jnp.dot(h_scr[...], wb_scr[j], preferred_element_type=F32).astype(o_ref.dtype)


def _modproj_call(x, mods_l, shift_idx, scale_idx, w, n_out, w_extra=None, *, w_layer=None, tm=1024, tn=512,
                  out_dtype=F32, name):
    t = T_ALL
    has_extra = w_extra is not None
    nj = n_out // tn
    if w_layer is None:
        w_spec = pl.BlockSpec((D_MODEL, tn), lambda i, j: (0, jnp.where(i == 0, j, nj - 1)))
    else:
        w_spec = pl.BlockSpec((None, D_MODEL, tn), lambda i, j: (w_layer, 0, jnp.where(i == 0, j, nj - 1)))
    in_specs = [
        pl.BlockSpec((tm, D_MODEL), lambda i, j: (i, 0)),
        pl.BlockSpec((MOD_ROWS, D_MODEL), lambda i, j: (0, shift_idx)),
        pl.BlockSpec((MOD_ROWS, D_MODEL), lambda i, j: (0, scale_idx)),
        w_spec,
    ]
    out_specs = [pl.BlockSpec((tm, tn), lambda i, j: (i, j))]
    out_shape = [jax.ShapeDtypeStruct((t, n_out), out_dtype)]
    args = [x, mods_l, mods_l, w]
    if has_extra:
        nx = w_extra.shape[1]
        in_specs.append(pl.BlockSpec((D_MODEL, nx), lambda i, j: (0, 0)))
        out_specs.append(pl.BlockSpec((tm, nx), lambda i, j: (i, 0)))
        out_shape.append(jax.ShapeDtypeStruct((t, nx), F32))
        args.append(w_extra)
    res = pl.pallas_call(
        functools.partial(_modproj_kernel, tiles_per_seg=SEG_ROWS // tm, has_extra=has_extra),
        grid=(t // tm, nj),
        in_specs=in_specs,
        out_specs=out_specs,
        out_shape=out_shape,
        scratch_shapes=[pltpu.VMEM((tm, D_MODEL), BF16), pltpu.VMEM((nj, D_MODEL, tn), BF16)],
        compiler_params=_cparams("arbitrary", "arbitrary"),
        name=name,
    )(*args)
    return res if has_extra else res[0]


def _rmsproj_kernel(x_ref, g_ref, w_ref, *rest, normalize, emit_xn):
    if emit_xn:
        o_ref, xn_ref, h_scr = rest
    else:
        o_ref, h_scr = rest
    j = pl.program_id(1)

    @pl.when(j == 0)
    def _():
        x = x_ref[...]
        xn = _rms(x, g_ref[...]) if normalize else x
        h_scr[...] = xn.astype(BF16)
        if emit_xn:
            xn_ref[...] = xn

    o_ref[...] = _mm(h_scr[...], w_ref[...]).astype(o_ref.dtype)


def _rmsproj_call(x, col_block, k, g, w, *, normalize=True, emit_xn=False, tm=1024, tn=512, out_dtype=F32, name):
    t = x.shape[0]
    n_out = w.shape[1]
    out_specs = [pl.BlockSpec((tm, tn), lambda i, j: (i, j))]
    out_shape = [jax.ShapeDtypeStruct((t, n_out), out_dtype)]
    if emit_xn:
        out_specs.append(pl.BlockSpec((tm, k), lambda i, j: (i, 0)))
        out_shape.append(jax.ShapeDtypeStruct((t, k), F32))
    res = pl.pallas_call(
        functools.partial(_rmsproj_kernel, normalize=normalize, emit_xn=emit_xn),
        grid=(t // tm, n_out // tn),
        in_specs=[
            pl.BlockSpec((tm, k), lambda i, j: (i, col_block)),
            pl.BlockSpec((1, k), lambda i, j: (0, 0)),
            pl.BlockSpec((k, tn), lambda i, j: (0, j)),
        ],
        out_specs=out_specs,
        out_shape=out_shape,
        scratch_shapes=[pltpu.VMEM((tm, k), BF16)],
        compiler_params=_cparams("arbitrary", "arbitrary"),
        name=name,
    )(x, g.reshape(1, k), w)
    return res if emit_xn else res[0]


def _outln_kernel(oc_ref, ol_ref, w_ref, x_ref, gate_ref, sh_ref, sc_ref, g_ref, b_ref, rw_ref, rb_ref,
                  y_ref, comb_ref, wb_scr, *, tiles_per_seg):
    i = pl.program_id(0)

    @pl.when(i == 0)
    def _():
        wb_scr[...] = w_ref[...].astype(BF16)

    r = _seg_row(i, tiles_per_seg)
    gate = gate_ref[pl.ds(r, 1), :]
    f = jnp.dot(_slab_read(i, oc_ref, ol_ref).astype(BF16), wb_scr[...], preferred_element_type=F32)
    y = _layer_norm(ALPHA * x_ref[...] + (1.0 + gate) * f, g_ref[...], b_ref[...])
    y_ref[...] = y
    comb_ref[...] = _route(y * (1.0 + sc_ref[pl.ds(r, 1), :]) + sh_ref[pl.ds(r, 1), :], rw_ref, rb_ref)


def _outln_call(o, w, w_layer, x, mods_l, g, b, router_w, router_b, *, tm=512, name):
    k = o[0].shape[1]
    t = T_ALL
    rw = jnp.pad(router_w.T, ((0, V7X_LANES - N_EXPERTS), (0, 0)))
    rb = jnp.pad(router_b.reshape(N_EXPERTS, 1), ((0, V7X_LANES - N_EXPERTS), (0, 0)))
    o_specs = [pl.BlockSpec((tm, k), s.index_map) for s in _slab_specs(tm, 1)]

    def mod(idx):
        return pl.BlockSpec((MOD_ROWS, D_MODEL), lambda i: (0, idx))

    return pl.pallas_call(
        functools.partial(_outln_kernel, tiles_per_seg=SEG_ROWS // tm),
        grid=(t // tm,),
        in_specs=o_specs + [
            pl.BlockSpec((None, k, D_MODEL), lambda i: (w_layer, 0, 0)),
            pl.BlockSpec((tm, D_MODEL), lambda i: (i, 0)),
            mod(2), mod(3), mod(4),
            pl.BlockSpec((1, D_MODEL), lambda i: (0, 0)),
            pl.BlockSpec((1, D_MODEL), lambda i: (0, 0)),
            pl.BlockSpec((V7X_LANES, D_MODEL), lambda i: (0, 0)),
            pl.BlockSpec((V7X_LANES, 1), lambda i: (0, 0)),
        ],
        out_specs=[pl.BlockSpec((tm, D_MODEL), lambda i: (i, 0)), pl.BlockSpec((tm, V7X_LANES), lambda i: (i, 0))],
        out_shape=[jax.ShapeDtypeStruct((t, D_MODEL), F32), jax.ShapeDtypeStruct((t, V7X_LANES), F32)],
        scratch_shapes=[pltpu.VMEM((k, D_MODEL), BF16)],
        compiler_params=_cparams("arbitrary"),
        name=name,
    )(o[0], o[1], w, x, mods_l, mods_l, mods_l, g.reshape(1, D_MODEL), b.reshape(1, D_MODEL), rw, rb)


def _gdn_gates_kernel(ab_ref, alog_ref, dt_ref, o_ref):
    ab = ab_ref[...]
    tm = ab.shape[0]
    x = ab + dt_ref[...]
    softplus = jnp.maximum(x, 0.0) + jnp.log(1.0 + jnp.exp(-jnp.abs(x)))
    g = -jnp.exp(alog_ref[...]) * softplus
    ri = lax.broadcasted_iota(jnp.int32, (tm, tm), 0)
    ci = lax.broadcasted_iota(jnp.int32, (tm, tm), 1)
    same = _chunk_of(ri) == _chunk_of(ci)
    lower = jnp.where(same & (ri >= ci), 1.0, 0.0).astype(F32)
    upper = jnp.where(same & (ri <= ci), 1.0, 0.0).astype(F32)
    pre = jnp.dot(lower, g, precision=lax.Precision.HIGHEST, preferred_element_type=F32)
    suf = jnp.dot(upper, g, precision=lax.Precision.HIGHEST, preferred_element_type=F32)
    lane = lax.broadcasted_iota(jnp.int32, ab.shape, 1)
    o_ref[...] = jnp.where(lane < GDN_HEADS, pre, jnp.where(lane < 2 * GDN_HEADS, suf, _sigmoid(ab)))


def _gdn_gates_call(ab, a_log, dt_bias):
    t, n = ab.shape
    tm = GDN_SUPER
    pad = n - 2 * GDN_HEADS
    alog_row = jnp.pad(a_log.reshape(1, -1), ((0, 0), (0, pad)))
    dt_row = jnp.pad(dt_bias.reshape(1, -1), ((0, 0), (0, pad)))
    return pl.pallas_call(
        _gdn_gates_kernel,
        grid=(t // tm,),
        in_specs=[
            pl.BlockSpec((tm, n), lambda i: (i, 0)),
            pl.BlockSpec((1, n), lambda i: (0, 0)),
            pl.BlockSpec((1, n), lambda i: (0, 0)),
        ],
        out_specs=pl.BlockSpec((tm, n), lambda i: (i, 0)),
        out_shape=jax.ShapeDtypeStruct((t, n), F32),
        compiler_params=_cparams("arbitrary"),
        name="gdn_gates",
    )(ab, alog_row, dt_row)


def _loop(n, body, max_static=4):
    if n <= max_static:
        for i in range(n):
            body(i)
    else:
        def wrapped(i, carry):
            body(i)
            return carry
        lax.fori_loop(0, n, wrapped, 0)


def _gdn_core_kernel(q_ref, k_ref, v_ref, z_ref, cq_ref, ck_ref, cv_ref, gc_ref, gr_ref, ng_ref, *rest,
                     n_tok, has_s0, hps):
    if has_s0:
        s0_ref, o_ref, st_ref = rest[:3]
        scr = rest[3:]
    else:
        o_ref, st_ref = rest[:2]
        scr = rest[2:]
    pad_scr, kn_scr, u_scr, w_scr, qe_scr, qk_scr, oo_scr, s_scr = scr
    n_super = n_tok // GDN_SUPER
    n_chunk = n_tok // GDN_CHUNK
    halo = V7X_SUBLANES
    sb = GDN_SUPER
    ck = GDN_CHUNK
    dk = GDN_DK

    zeros_halo = jnp.zeros((halo, dk), F32)
    for hd in range(hps):
        for a, src in enumerate((q_ref, k_ref, v_ref)):
            pad_scr[3 * hd + a, 0:halo, :] = zeros_halo
            pad_scr[3 * hd + a, halo:halo + n_tok, :] = src[:, hd * dk:(hd + 1) * dk].astype(F32)
            pad_scr[3 * hd + a, halo + n_tok:2 * halo + n_tok, :] = zeros_halo
        for d in range(2):
            s_scr[2 * hd + d] = s0_ref[d, hd] if has_s0 else jnp.zeros((dk, GDN_DV), F32)

    ri = lax.broadcasted_iota(jnp.int32, (sb, sb), 0)
    ci = lax.broadcasted_iota(jnp.int32, (sb, sb), 1)
    same_chunk = _chunk_of(ri) == _chunk_of(ci)
    eye = jnp.where(ri == ci, 1.0, 0.0).astype(F32)
    n_levels = int(math.log2(ck))

    def level_mask(l):
        return ((ri >> (l + 1)) == (ci >> (l + 1))) & ((ri >> l) != (ci >> l))

    def conv_silu(a, w_ref, hd, r0):
        xp = pad_scr[a, pl.ds(r0, sb + 2 * halo), :]
        acc = None
        for j in range(GDN_CONV):
            shift = (GDN_CONV // 2 - j) % (sb + 2 * halo)
            tap = xp if shift == 0 else pltpu.roll(xp, shift, 0)
            term = w_ref[j:j + 1, hd * dk:(hd + 1) * dk] * tap[halo:halo + sb, :]
            acc = term if acc is None else acc + term
        return _silu(acc)

    chains = [(hd, d) for hd in range(hps) for d in range(2)]

    a_unroll = max(1, min(n_super, 8 // (2 * hps)))

    def phase_a(it):
        units = [(it * a_unroll + u, hd) for u in range(a_unroll) for hd in range(hps)]
        r0s = [pl.multiple_of(s * sb, sb) for s, _ in units]
        qs, ks, vs, ps, qks = [], [], [], [], []
        for (s, hd), r0 in zip(units, r0s):
            q = conv_silu(3 * hd, cq_ref, hd, r0)
            k = conv_silu(3 * hd + 1, ck_ref, hd, r0)
            v = conv_silu(3 * hd + 2, cv_ref, hd, r0)
            q = q * lax.rsqrt(jnp.sum(q * q, axis=-1, keepdims=True) + NORM_EPS) * (dk ** -0.5)
            k = k * lax.rsqrt(jnp.sum(k * k, axis=-1, keepdims=True) + NORM_EPS)
            kn_scr[hd, pl.ds(r0, sb), :] = k
            kb = k.astype(BF16)
            qs.append(q)
            ks.append(k)
            vs.append(v)
            ps.append(_mm_nt(kb, kb))
            qks.append(_mm_nt(q, kb))
        a_chains = [(n, d) for n in range(len(units)) for d in range(2)]
        a_mats, ts, rhs = [], [], []
        for n, d in a_chains:
            (s, hd), r0 = units[n], r0s[n]
            hdd = 2 * hd + d
            gcol = gc_ref[hd, pl.ds(r0, sb), d:d + 1]
            bcol = gc_ref[hd, pl.ds(r0, sb), 2 + d:3 + d]
            grow = gr_ref[hd, s, d:d + 1, :]
            if d == 0:
                mask, strict = same_chunk & (ri >= ci), same_chunk & (ri > ci)
            else:
                mask, strict = same_chunk & (ri <= ci), same_chunk & (ri < ci)
            dec = jnp.exp(jnp.where(mask, gcol - grow, -jnp.inf))
            a = jnp.where(strict, ps[n] * bcol * dec, 0.0)
            a_mats.append(a)
            ts.append(eye - jnp.where(level_mask(0), a, 0.0))
            eg = jnp.exp(gcol)
            rhs.append(jnp.concatenate([vs[n] * bcol, ks[n] * (bcol * eg)], axis=1).astype(BF16))
            qe_scr[hdd, pl.ds(r0, sb), :] = qs[n] * eg
            qkd = qks[n] * dec
            for c in range(CHUNKS_PER_SUPER):
                qk_scr[hdd, pl.ds(r0 + c * ck, ck), :] = qkd[c * ck:(c + 1) * ck, c * ck:(c + 1) * ck]
        tbs = [t.astype(BF16) for t in ts]
        for l in range(1, n_levels):
            lm = level_mask(l)
            ys = [_mm(jnp.where(lm, a_mats[i], 0.0), tbs[i]) for i in range(len(a_chains))]
            ts = [ts[i] - _mm(tbs[i], ys[i]) for i in range(len(a_chains))]
            tbs = [t.astype(BF16) for t in ts]
        for i, (n, d) in enumerate(a_chains):
            (s, hd), r0 = units[n], r0s[n]
            uw = _mm(tbs[i], rhs[i])
            u_scr[2 * hd + d, pl.ds(r0, sb), :] = uw[:, :GDN_DV]
            w_scr[2 * hd + d, pl.ds(r0, sb), :] = uw[:, GDN_DV:]

    _loop(n_super // a_unroll, phase_a, max_static=1)

    def phase_b(i):
        rows, ws_qs, v_new, kws, gls = [], [], [], [], []
        for hd, d in chains:
            hdd = 2 * hd + d
            c = i if d == 0 else n_chunk - 1 - i
            r = pl.multiple_of(c * ck, ck)
            r_last = r + (ck - 1) if d == 0 else r
            rows.append(r)
            gcol = gc_ref[hd, pl.ds(r, ck), d:d + 1]
            gl = gc_ref[hd, pl.ds(r_last, 1), d:d + 1]
            gls.append(gl)
            kws.append((kn_scr[hd, pl.ds(r, ck), :] * jnp.exp(gl - gcol)).T)
            wq = jnp.concatenate([w_scr[hdd, pl.ds(r, ck), :], qe_scr[hdd, pl.ds(r, ck), :]], axis=0)
            ws_qs.append(_mm(wq, s_scr[hdd]))
        for j, (hd, d) in enumerate(chains):
            hdd = 2 * hd + d
            vn = u_scr[hdd, pl.ds(rows[j], ck), :] - ws_qs[j][:ck]
            v_new.append(vn)
            oo_scr[hdd, pl.ds(rows[j], ck), :] = ws_qs[j][ck:] + _mm(qk_scr[hdd, pl.ds(rows[j], ck), :], vn)
        for j, (hd, d) in enumerate(chains):
            hdd = 2 * hd + d
            s_scr[hdd] = s_scr[hdd] * jnp.exp(gls[j]) + _mm(kws[j], v_new[j])

    _loop(n_chunk, phase_b)

    def finish(s):
        r0 = pl.multiple_of(s * sb, sb)
        for hd in range(hps):
            o = oo_scr[2 * hd, pl.ds(r0, sb), :] + oo_scr[2 * hd + 1, pl.ds(r0, sb), :]
            z = z_ref[pl.ds(r0, sb), hd * GDN_DV:(hd + 1) * GDN_DV].astype(F32)
            o_ref[pl.ds(r0, sb), hd * GDN_DV:(hd + 1) * GDN_DV] = _rms(o, ng_ref[...]) * _silu(z)

    _loop(n_super, finish)
    for hd in range(hps):
        for d in range(2):
            st_ref[d, hd] = s_scr[2 * hd + d]


def _gdn_core_call(qkvz, conv_w, gc, gr, norm_g, s0, *, batch, n_tok, row_block_off, hps, name):
    has_s0 = s0 is not None
    h = GDN_HEADS
    dk = GDN_DK
    n_super = n_tok // GDN_SUPER
    hb = h // hps

    mode = dict(pipeline_mode=pl.Buffered(1)) if n_tok > GDN_SUPER else {}

    def col(sec):
        return pl.BlockSpec((n_tok, dk * hps), lambda b, hh: (b + row_block_off, sec * hb + hh), **mode)

    def cw(sec):
        return pl.BlockSpec((GDN_CONV, dk * hps), lambda b, hh: (0, sec * hb + hh))

    in_specs = [col(0), col(1), col(2), col(3), cw(0), cw(1), cw(2),
                pl.BlockSpec((None, hps, n_tok, 4), lambda b, hh: (b, hh, 0, 0)),
                pl.BlockSpec((None, hps, n_super, 4, GDN_SUPER), lambda b, hh: (b, hh, 0, 0, 0)),
                pl.BlockSpec((1, GDN_DV), lambda b, hh: (0, 0))]
    args = [qkvz, qkvz, qkvz, qkvz, conv_w, conv_w, conv_w, gc, gr, norm_g.reshape(1, GDN_DV)]
    st_spec = pl.BlockSpec((None, 2, hps, dk, GDN_DV), lambda b, hh: (b, 0, hh, 0, 0))
    if has_s0:
        in_specs.append(st_spec)
        args.append(s0)
    return pl.pallas_call(
        functools.partial(_gdn_core_kernel, n_tok=n_tok, has_s0=has_s0, hps=hps),
        grid=(batch, hb),
        in_specs=in_specs,
        out_specs=[pl.BlockSpec((n_tok, GDN_DV * hps), lambda b, hh: (b, hh)), st_spec],
        out_shape=[jax.ShapeDtypeStruct((batch * n_tok, h * GDN_DV), F32),
                   jax.ShapeDtypeStruct((batch, 2, h, dk, GDN_DV), F32)],
        scratch_shapes=[
            pltpu.VMEM((3 * hps, n_tok + 2 * V7X_SUBLANES, dk), F32),
            pltpu.VMEM((hps, n_tok, dk), F32),
            pltpu.VMEM((2 * hps, n_tok, GDN_DV), F32),
            pltpu.VMEM((2 * hps, n_tok, dk), F32),
            pltpu.VMEM((2 * hps, n_tok, dk), F32),
            pltpu.VMEM((2 * hps, n_tok, GDN_CHUNK), F32),
            pltpu.VMEM((2 * hps, n_tok, GDN_DV), F32),
            pltpu.VMEM((2 * hps, dk, GDN_DV), F32),
        ],
        compiler_params=_cparams("arbitrary", "arbitrary"),
        name=name,
    )(*args)


def _gdn_layer(x, mods_l, li, state_gdn, gdn_w_in, gdn_conv, gdn_a_log, gdn_dt_bias, gdn_norm_g):
    w_ab = jnp.pad(gdn_w_in[li, :, GDN_QKVZ:], ((0, 0), (0, V7X_LANES - 4 * GDN_HEADS)))
    qkvz, ab = _modproj_call(x, mods_l, 0, 1, gdn_w_in, GDN_QKVZ, w_ab, w_layer=li, tn=1024, out_dtype=BF16,
                             name="gdn_in")
    gates = _gdn_gates_call(ab, gdn_a_log[li], gdn_dt_bias[li])[:, :4 * GDN_HEADS]

    def head_layouts(g, batch, n_tok):
        g = g.reshape(batch, n_tok, 4, GDN_HEADS)
        gc = jnp.transpose(g, (0, 3, 1, 2))
        gr = jnp.transpose(g, (0, 3, 2, 1)).reshape(batch, GDN_HEADS, 4, n_tok // GDN_SUPER, GDN_SUPER)
        return gc, jnp.transpose(gr, (0, 1, 3, 2, 4))

    gc_p, gr_p = head_layouts(gates[:T_CTX], BATCH, SEQ)
    gc_s, gr_s = head_layouts(gates[T_CTX:], DEC_BATCH, DEC_SEQ)
    o_p, st_p = _gdn_core_call(qkvz, gdn_conv[li], gc_p, gr_p, gdn_norm_g[li], None,
                               batch=BATCH, n_tok=SEQ, row_block_off=0, hps=4, name="gdn_core_ctx")
    o_s, _ = _gdn_core_call(qkvz, gdn_conv[li], gc_s, gr_s, gdn_norm_g[li], state_gdn[:, li],
                            batch=DEC_BATCH, n_tok=DEC_SEQ, row_block_off=T_CTX // DEC_SEQ, hps=2,
                            name="gdn_core_lat")
    return (o_p, o_s), st_p


def _rope_tables(n_tokens, rot_dim, reps):
    n_rows = n_tokens // GRID_W
    row = jnp.repeat(jnp.arange(n_rows, dtype=F32), GRID_W)
    colp = jnp.tile(jnp.arange(GRID_W, dtype=F32), n_rows)
    n_freq = rot_dim // 4
    inv_freq = ROPE_THETA ** (-jnp.arange(n_freq, dtype=F32) / n_freq)
    ra, ca = row[:, None] * inv_freq, colp[:, None] * inv_freq
    cos = jnp.concatenate([jnp.cos(ra), jnp.cos(ra), jnp.cos(ca), jnp.cos(ca)], axis=-1)
    sin = jnp.concatenate([-jnp.sin(ra), jnp.sin(ra), -jnp.sin(ca), jnp.sin(ca)], axis=-1)
    return jnp.tile(cos, (1, reps)), jnp.tile(sin, (1, reps))


def _rope(x, cos, sin, quarter):
    n = x.shape[-1]
    lane = lax.broadcasted_iota(jnp.int32, x.shape, x.ndim - 1)
    first = (lane & (2 * quarter - 1)) < quarter
    partner = jnp.where(first, pltpu.roll(x, n - quarter, x.ndim - 1), pltpu.roll(x, quarter, x.ndim - 1))
    return x * cos + partner * sin


def _gqa_prep_kernel(kv_ref, g_ref, *rest, rope):
    if rope:
        cos_ref, sin_ref, k_ref, v_ref = rest
    else:
        k_ref, v_ref = rest
    kv = kv_ref[...]
    nk = GQA_KV_HEADS * GQA_HEAD_DIM
    for hh in range(GQA_KV_HEADS):
        k = _rms(kv[:, hh * GQA_HEAD_DIM:(hh + 1) * GQA_HEAD_DIM], g_ref[...])
        if rope:
            k = _rope(k, cos_ref[...], sin_ref[...], GQA_HEAD_DIM // 4)
        k_ref[:, hh * GQA_HEAD_DIM:(hh + 1) * GQA_HEAD_DIM] = k
    v_ref[...] = kv[:, nk:]


def _gqa_prep_call(proj, k_norm, row_off, n_rows, tables, *, tm=512, name):
    rope = tables is not None
    nkv = 2 * GQA_KV_HEADS * GQA_HEAD_DIM
    kv_col = GQA_HEADS * GQA_HEAD_DIM // nkv
    off = row_off // tm
    in_specs = [pl.BlockSpec((tm, nkv), lambda i: (i + off, kv_col)),
                pl.BlockSpec((1, GQA_HEAD_DIM), lambda i: (0, 0))]
    args = [proj, k_norm.reshape(1, GQA_HEAD_DIM)]
    if rope:
        per_seq = DEC_SEQ // tm
        in_specs += [pl.BlockSpec((tm, GQA_HEAD_DIM), lambda i: (i % per_seq, 0))] * 2
        args += list(tables)
    half = nkv // 2
    return pl.pallas_call(
        functools.partial(_gqa_prep_kernel, rope=rope),
        grid=(n_rows // tm,),
        in_specs=in_specs,
        out_specs=[pl.BlockSpec((tm, half), lambda i: (i, 0))] * 2,
        out_shape=[jax.ShapeDtypeStruct((n_rows, half), F32)] * 2,
        compiler_params=_cparams("arbitrary"),
        name=name,
    )(*args)


def _attend(chains, scale):
    scores = [[_mm_nt(q, k) * scale for k in ks] for q, ks, _ in chains]
    probs, dens = [], []
    for sc in scores:
        m = None
        for s in sc:
            sm = jnp.max(s, axis=-1, keepdims=True)
            m = sm if m is None else jnp.maximum(m, sm)
        ps = [jnp.exp(s - m) for s in sc]
        den = None
        for p in ps:
            r = jnp.sum(p, axis=-1, keepdims=True)
            den = r if den is None else den + r
        probs.append([p.astype(BF16) for p in ps])
        dens.append(den)
    outs = []
    for (_, _, vs), ps, den in zip(chains, probs, dens):
        acc = None
        for p, v in zip(ps, vs):
            pv = _mm(p, v)
            acc = pv if acc is None else acc + pv
        outs.append(acc / den)
    return outs


def _gqa_attn_kernel(q_ref, qg_ref, *rest, n_seg, rope, kv_per_step):
    if rope:
        cos_ref, sin_ref = rest[:2]
        rest = rest[2:]
    seg_refs = rest[:2 * n_seg]
    o_ref = rest[2 * n_seg]
    dh = GQA_HEAD_DIM
    tq = q_ref.shape[0]
    heads_per_chain = GQA_GROUP // 2
    chains, heads = [], []
    for kh in range(kv_per_step):
        ks = [seg_refs[2 * sgi][:, kh * dh:(kh + 1) * dh].astype(BF16) for sgi in range(n_seg)]
        vs = [seg_refs[2 * sgi + 1][:, kh * dh:(kh + 1) * dh].astype(BF16) for sgi in range(n_seg)]
        for g0 in range(0, GQA_GROUP, heads_per_chain):
            qs = []
            for g in range(g0, g0 + heads_per_chain):
                c0 = (kh * GQA_GROUP + g) * dh
                qh = _rms(q_ref[:, c0:c0 + dh], qg_ref[...])
                if rope:
                    qh = _rope(qh, cos_ref[...], sin_ref[...], dh // 4)
                qs.append(qh.astype(BF16))
            chains.append((jnp.concatenate(qs, axis=0), ks, vs))
            heads.append(kh * GQA_GROUP + g0)
    for o, h0 in zip(_attend(chains, dh ** -0.5), heads):
        for g in range(heads_per_chain):
            o_ref[:, (h0 + g) * dh:(h0 + g + 1) * dh] = o[g * tq:(g + 1) * tq]


def _gqa_attn_call(proj, q_norm, segs, tables, *, batch, n_q, row_off, tq, kv_per_step, name):
    rope = tables is not None
    dh = GQA_HEAD_DIM
    qw = GQA_GROUP * dh * kv_per_step
    kw = dh * kv_per_step
    nqb = n_q // tq
    off = row_off // tq
    in_specs = [pl.BlockSpec((tq, qw), lambda b, h, i: (b * nqb + i + off, h)),
                pl.BlockSpec((1, dh), lambda b, h, i: (0, 0))]
    args = [proj, q_norm.reshape(1, dh)]
    if rope:
        in_specs += [pl.BlockSpec((tq, dh), lambda b, h, i: (i, 0))] * 2
        args += list(tables)
    for k2d, v2d, rows in segs:
        in_specs += [pl.BlockSpec((rows, kw), lambda b, h, i: (b, h))] * 2
        args += [k2d, v2d]
    return pl.pallas_call(
        functools.partial(_gqa_attn_kernel, n_seg=len(segs), rope=rope, kv_per_step=kv_per_step),
        grid=(batch, GQA_KV_HEADS // kv_per_step, nqb),
        in_specs=in_specs,
        out_specs=pl.BlockSpec((tq, qw), lambda b, h, i: (b * nqb + i, h)),
        out_shape=jax.ShapeDtypeStruct((batch * n_q, GQA_HEADS * dh), F32),
        compiler_params=_cparams("arbitrary", "arbitrary", "arbitrary"),
        name=name,
    )(*args)


def _gqa_layer(x, mods_l, li, cache_k, cache_v, gqa_w_in, gqa_q_norm, gqa_k_norm):
    proj = _modproj_call(x, mods_l, 0, 1, gqa_w_in, GQA_IN, w_layer=li, name="gqa_in")
    tables = _rope_tables(DEC_SEQ, GQA_HEAD_DIM, 1)
    nkv = GQA_KV_HEADS * GQA_HEAD_DIM
    k_p, v_p = _gqa_prep_call(proj, gqa_k_norm[li], 0, T_CTX, None, name="gqa_prep_ctx")
    k_s, v_s = _gqa_prep_call(proj, gqa_k_norm[li], T_CTX, T_LAT, tables, name="gqa_prep_lat")
    o_p = _gqa_attn_call(proj, gqa_q_norm[li], [(k_p, v_p, SEQ)], None, batch=BATCH, n_q=SEQ, row_off=0,
                         tq=SEQ, kv_per_step=GQA_KV_HEADS, name="gqa_attn_ctx")
    ck = cache_k[:, li].reshape(DEC_BATCH * PAST_LEN, nkv)
    cv = cache_v[:, li].reshape(DEC_BATCH * PAST_LEN, nkv)
    o_s = _gqa_attn_call(proj, gqa_q_norm[li], [(ck, cv, PAST_LEN), (k_s, v_s, DEC_SEQ)], tables,
                         batch=DEC_BATCH, n_q=DEC_SEQ, row_off=T_CTX, tq=256, kv_per_step=1, name="gqa_attn_lat")
    new_k = k_p.reshape(BATCH, SEQ, GQA_KV_HEADS, GQA_HEAD_DIM)
    new_v = v_p.reshape(BATCH, SEQ, GQA_KV_HEADS, GQA_HEAD_DIM)
    return (o_p, o_s), new_k, new_v


def _krope_kernel(x_ref, *rest, rope):
    if rope:
        cos_ref, sin_ref, o_ref = rest
    else:
        (o_ref,) = rest
    x = x_ref[...]
    x2 = jnp.concatenate([x, x], axis=1)
    if rope:
        x2 = _rope(x2, cos_ref[...], sin_ref[...], MLA_ROPE // 4)
    o_ref[...] = x2


def _krope_call(krope, tables, *, tm=512, name):
    rope = tables is not None
    n = krope.shape[0]
    in_specs = [pl.BlockSpec((tm, MLA_ROPE), lambda i: (i, 0))]
    args = [krope]
    if rope:
        per_seq = DEC_SEQ // tm
        in_specs += [pl.BlockSpec((tm, 2 * MLA_ROPE), lambda i: (i % per_seq, 0))] * 2
        args += list(tables)
    return pl.pallas_call(
        functools.partial(_krope_kernel, rope=rope),
        grid=(n // tm,),
        in_specs=in_specs,
        out_specs=pl.BlockSpec((tm, 2 * MLA_ROPE), lambda i: (i, 0)),
        out_shape=jax.ShapeDtypeStruct((n, 2 * MLA_ROPE), F32),
        compiler_params=_cparams("arbitrary"),
        name=name,
    )(*args)


def _mla_attn_kernel(qn_ref, qr_ref, *rest, n_seg, rope, heads_per_step):
    if rope:
        cos_ref, sin_ref = rest[:2]
        rest = rest[2:]
    seg_refs = rest[:2 * n_seg]
    o_ref = rest[2 * n_seg]
    scale = (MLA_NOPE + MLA_ROPE) ** -0.5
    h0 = pl.program_id(1) * heads_per_step
    pair_w = 2 * MLA_ROPE
    tq = qn_ref.shape[0]
    row_splits = 2 if heads_per_step == 1 else 1
    rows = tq // row_splits
    chains = []
    for j in range(heads_per_step):
        qn = qn_ref[:, j * MLA_NOPE:(j + 1) * MLA_NOPE]
        pair = qr_ref[:, (j // 2) * pair_w:(j // 2 + 1) * pair_w]
        if rope:
            pair = _rope(pair, cos_ref[...], sin_ref[...], MLA_ROPE // 4)
        lane = lax.broadcasted_iota(jnp.int32, pair.shape, 1)
        mine = (lane >> int(math.log2(MLA_ROPE))) == ((h0 + j) & 1)
        q = jnp.concatenate([qn, jnp.where(mine, pair, 0.0)], axis=1).astype(BF16)
        ks, vs = [], []
        for sgi in range(n_seg):
            kv = seg_refs[2 * sgi][:, j * (MLA_NOPE + MLA_V):(j + 1) * (MLA_NOPE + MLA_V)]
            ks.append(jnp.concatenate([kv[:, :MLA_NOPE].astype(BF16), seg_refs[2 * sgi + 1][...].astype(BF16)],
                                      axis=1))
            vs.append(kv[:, MLA_NOPE:].astype(BF16))
        for c in range(row_splits):
            chains.append((q[c * rows:(c + 1) * rows], ks, vs))
    outs = _attend(chains, scale)
    for j in range(heads_per_step):
        for c in range(row_splits):
            o_ref[c * rows:(c + 1) * rows, j * MLA_V:(j + 1) * MLA_V] = outs[j * row_splits + c]


def _mla_attn_call(q, segs, tables, *, batch, n_q, row_off, tq, heads_per_step, name):
    rope = tables is not None
    hps = heads_per_step
    nqb = n_q // tq
    off = row_off // tq
    rope_w = max(2 * MLA_ROPE, MLA_ROPE * hps)
    rope_base = MLA_HEADS * MLA_NOPE // rope_w

    def rope_idx(b, h, i):
        return (b * nqb + i + off, rope_base + (h * hps * MLA_ROPE) // rope_w)

    in_specs = [pl.BlockSpec((tq, MLA_NOPE * hps), lambda b, h, i: (b * nqb + i + off, h)),
                pl.BlockSpec((tq, rope_w), rope_idx)]
    args = [q, q]
    if rope:
        in_specs += [pl.BlockSpec((tq, 2 * MLA_ROPE), lambda b, h, i: (i, 0))] * 2
        args += list(tables)
    for kv2d, kr2, rows, kv_row_off in segs:
        kv_off = kv_row_off // rows
        in_specs += [pl.BlockSpec((rows, (MLA_NOPE + MLA_V) * hps), lambda b, h, i, kv_off=kv_off: (b + kv_off, h)),
                     pl.BlockSpec((rows, 2 * MLA_ROPE), lambda b, h, i: (b, 0))]
        args += [kv2d, kr2]
    return pl.pallas_call(
        functools.partial(_mla_attn_kernel, n_seg=len(segs), rope=rope, heads_per_step=hps),
        grid=(batch, MLA_HEADS // hps, nqb),
        in_specs=in_specs,
        out_specs=pl.BlockSpec((tq, MLA_V * hps), lambda b, h, i: (b * nqb + i, h)),
        out_shape=jax.ShapeDtypeStruct((batch * n_q, MLA_HEADS * MLA_V), F32),
        compiler_params=_cparams("arbitrary", "arbitrary", "arbitrary"),
        name=name,
    )(*args)


def _mla_layer(x, mods_l, li, cache_ckv, cache_krope, mla_w_in, mla_q_norm, mla_kv_norm, mla_w_q_up, mla_w_kv_up):
    w_in = jnp.pad(mla_w_in[li], ((0, 0), (0, 7 * V7X_LANES - MLA_IN)))
    proj = _modproj_call(x, mods_l, 0, 1, w_in, 7 * V7X_LANES, tn=7 * V7X_LANES, name="mla_in")
    wq = mla_w_q_up[li].reshape(MLA_Q_LORA, MLA_HEADS, MLA_NOPE + MLA_ROPE)
    wq = jnp.concatenate([wq[:, :, :MLA_NOPE].reshape(MLA_Q_LORA, -1), wq[:, :, MLA_NOPE:].reshape(MLA_Q_LORA, -1)], axis=1)
    q = _rmsproj_call(proj, 0, MLA_Q_LORA, mla_q_norm[li], wq, name="mla_q_up")
    kv, ckv = _rmsproj_call(proj, MLA_Q_LORA // MLA_KV_LORA, MLA_KV_LORA, mla_kv_norm[li], mla_w_kv_up[li],
                            emit_xn=True, out_dtype=BF16, name="mla_kv_up")
    krope = proj[:, MLA_Q_LORA + MLA_KV_LORA:MLA_IN]
    tables = _rope_tables(DEC_SEQ, MLA_ROPE, 2)
    kr2_p = _krope_call(krope[:T_CTX], None, name="mla_krope_ctx")
    kr2_s = _krope_call(krope[T_CTX:], tables, name="mla_krope_lat")
    ctx_ckv = cache_ckv[:, li].reshape(DEC_BATCH * PAST_LEN, MLA_KV_LORA)
    kv_c = _rmsproj_call(ctx_ckv, 0, MLA_KV_LORA, jnp.ones((MLA_KV_LORA,), F32), mla_w_kv_up[li], normalize=False,
                         out_dtype=BF16, name="mla_kv_up_cache")
    kr2_c = _krope_call(cache_krope[:, li].reshape(DEC_BATCH * PAST_LEN, MLA_ROPE), None, name="mla_krope_cache")
    o_p = _mla_attn_call(q, [(kv, kr2_p, SEQ, 0)], None, batch=BATCH, n_q=SEQ, row_off=0, tq=SEQ,
                         heads_per_step=MLA_HEADS, name="mla_attn_ctx")
    o_s = _mla_attn_call(q, [(kv_c, kr2_c, PAST_LEN, 0), (kv, kr2_s, DEC_SEQ, T_CTX)], tables, batch=DEC_BATCH,
                         n_q=DEC_SEQ, row_off=T_CTX, tq=512, heads_per_step=1, name="mla_attn_lat")
    new_ckv = ckv[:T_CTX].reshape(BATCH, SEQ, MLA_KV_LORA)
    new_krope = krope[:T_CTX].reshape(BATCH, SEQ, MLA_ROPE)
    return (o_p, o_s), new_ckv, new_krope


def _route(h, rw_ref, rb_ref):
    logits = lax.dot_general(rw_ref[...], h, (((1,), (1,)), ((), ())), precision=lax.Precision.HIGHEST,
                             preferred_element_type=F32)[:N_EXPERTS]
    scores = _sigmoid(logits)
    sel = scores + rb_ref[0:N_EXPERTS, :]
    tm = sel.shape[1]
    neg = -jnp.inf
    gs = []
    for g in range(N_GROUPS):
        a, b, c, d = (sel[EXPERTS_PER_GROUP * g + j:EXPERTS_PER_GROUP * g + j + 1, :] for j in range(4))
        gs.append(jnp.maximum(jnp.maximum(jnp.maximum(a + b, a + c), jnp.maximum(a + d, b + c)),
                              jnp.maximum(b + d, c + d)))
    gmax = jnp.maximum(jnp.maximum(gs[0], gs[1]), jnp.maximum(gs[2], gs[3]))
    best = jnp.where(gs[0] == gmax, 0, jnp.where(gs[1] == gmax, 1, jnp.where(gs[2] == gmax, 2, 3)))
    eid = lax.broadcasted_iota(jnp.int32, (N_EXPERTS, tm), 0)
    masked = jnp.where((eid >> int(math.log2(EXPERTS_PER_GROUP))) == best, sel, neg)
    m1 = jnp.max(masked, axis=0, keepdims=True)
    i1 = jnp.min(jnp.where(masked == m1, eid, N_EXPERTS), axis=0, keepdims=True)
    masked2 = jnp.where(eid == i1, neg, masked)
    m2 = jnp.max(masked2, axis=0, keepdims=True)
    i2 = jnp.min(jnp.where(masked2 == m2, eid, N_EXPERTS), axis=0, keepdims=True)
    w1 = jnp.sum(jnp.where(eid == i1, scores, 0.0), axis=0, keepdims=True)
    w2 = jnp.sum(jnp.where(eid == i2, scores, 0.0), axis=0, keepdims=True)
    tot = w1 + w2
    comb = jnp.where(eid == i1, w1 / tot, 0.0) + jnp.where(eid == i2, w2 / tot, 0.0)
    comb = jnp.concatenate([comb, jnp.zeros((V7X_LANES - N_EXPERTS, tm), F32)], axis=0)
    return comb.T


def _moe_kernel(x_ref, sh_ref, sc_ref, gate_ref, comb_ref, wg_ref, wu_ref, wd_ref, g_ref, b_ref, *rest,
                tiles_per_seg, split_out):
    if split_out:
        yc_ref, yl_ref, h_scr, acc_scr = rest
    else:
        y_ref, h_scr, acc_scr = rest
    i, e = pl.program_id(0), pl.program_id(1)
    r = _seg_row(i, tiles_per_seg)

    @pl.when(e == 0)
    def _():
        h_scr[...] = (x_ref[...] * (1.0 + sc_ref[pl.ds(r, 1), :]) + sh_ref[pl.ds(r, 1), :]).astype(BF16)
        acc_scr[...] = jnp.zeros_like(acc_scr)

    comb = comb_ref[...]
    lane = lax.broadcasted_iota(jnp.int32, comb.shape, 1)
    h = h_scr[...]
    epb = wg_ref.shape[0]
    f = None
    for j in range(epb):
        c_e = jnp.sum(jnp.where(lane == e * epb + j, comb, 0.0), axis=1, keepdims=True)
        hid = _silu(_mm(h, wg_ref[j])) * _mm(h, wu_ref[j]) * c_e
        fj = _mm(hid, wd_ref[j])
        f = fj if f is None else f + fj
    acc_scr[...] += f

    def result():
        gate = gate_ref[pl.ds(r, 1), :]
        return _layer_norm(ALPHA * x_ref[...] + (1.0 + gate) * acc_scr[...], g_ref[...], b_ref[...])

    last = e == pl.num_programs(1) - 1
    if split_out:
        _slab_write(i, last, yc_ref, yl_ref, result)
    else:
        @pl.when(last)
        def _():
            y_ref[...] = result()


MOE_EXPERTS_PER_STEP = 2


def _moe_call(x, mods_l, comb, w_gate, w_up, w_down, layer, g, b, *, split_out, tm=1024):
    t = T_ALL
    epb = MOE_EXPERTS_PER_STEP
    if split_out:
        out_specs, out_shape = _slab_specs(tm, 2), _slab_shapes()
    else:
        out_specs = pl.BlockSpec((tm, D_MODEL), lambda i, e: (i, 0))
        out_shape = jax.ShapeDtypeStruct((t, D_MODEL), F32)
    return pl.pallas_call(
        functools.partial(_moe_kernel, tiles_per_seg=SEG_ROWS // tm, split_out=split_out),
        grid=(t // tm, N_EXPERTS // epb),
        in_specs=[
            pl.BlockSpec((tm, D_MODEL), lambda i, e: (i, 0)),
            pl.BlockSpec((MOD_ROWS, D_MODEL), lambda i, e: (0, 3)),
            pl.BlockSpec((MOD_ROWS, D_MODEL), lambda i, e: (0, 4)),
            pl.BlockSpec((MOD_ROWS, D_MODEL), lambda i, e: (0, 5)),
            pl.BlockSpec((tm, V7X_LANES), lambda i, e: (i, 0)),
            pl.BlockSpec((None, epb, D_MODEL, D_EXPERT), lambda i, e: (layer, e, 0, 0)),
            pl.BlockSpec((None, epb, D_MODEL, D_EXPERT), lambda i, e: (layer, e, 0, 0)),
            pl.BlockSpec((None, epb, D_EXPERT, D_MODEL), lambda i, e: (layer, e, 0, 0)),
            pl.BlockSpec((1, D_MODEL), lambda i, e: (0, 0)),
            pl.BlockSpec((1, D_MODEL), lambda i, e: (0, 0)),
        ],
        out_specs=out_specs,
        out_shape=out_shape,
        scratch_shapes=[pltpu.VMEM((tm, D_MODEL), BF16), pltpu.VMEM((tm, D_MODEL), F32)],
        compiler_params=_cparams("arbitrary", "arbitrary"),
        name="moe_experts",
    )(x, mods_l, mods_l, mods_l, comb, w_gate, w_up, w_down, g.reshape(1, D_MODEL), b.reshape(1, D_MODEL))


def kernel(x_prompt, x_sample, state_gdn, cache_gqa_k, cache_gqa_v, cache_mla_ckv, cache_mla_krope, c, c_ctx, w_ada, b_ada, ln_mix_g, ln_mix_b, ln_ffn_g, ln_ffn_b, gdn_w_in, gdn_conv, gdn_a_log, gdn_dt_bias, gdn_norm_g, gdn_w_out, gqa_w_in, gqa_q_norm, gqa_k_norm, gqa_w_out, mla_w_in, mla_q_norm, mla_kv_norm, mla_w_q_up, mla_w_kv_up, mla_w_out, router_w, router_b, moe_w_gate, moe_w_up, moe_w_down):
    x = jnp.concatenate([x_prompt.reshape(T_CTX, D_MODEL), x_sample.reshape(T_LAT, D_MODEL)], axis=0)
    cond = jnp.concatenate([c_ctx[None], c, jnp.zeros((MOD_ROWS - 1 - DEC_BATCH, D_MODEL), F32)], axis=0)
    mods = _ada_call(cond, w_ada, b_ada)

    st_gdn, st_k, st_v, st_ckv, st_krope = [], [], [], [], []
    for l in range(DEPTH):
        kind, li = l % N_MIXERS, l // N_MIXERS
        mods_l = mods[l]
        if kind == 0:
            o, st = _gdn_layer(x, mods_l, li, state_gdn, gdn_w_in, gdn_conv, gdn_a_log, gdn_dt_bias, gdn_norm_g)
            st_gdn.append(st)
            w_out = gdn_w_out
        elif kind == 1:
            o, nk, nv = _gqa_layer(x, mods_l, li, cache_gqa_k, cache_gqa_v, gqa_w_in, gqa_q_norm, gqa_k_norm)
            st_k.append(nk)
            st_v.append(nv)
            w_out = gqa_w_out
        else:
            o, nckv, nkr = _mla_layer(x, mods_l, li, cache_mla_ckv, cache_mla_krope, mla_w_in, mla_q_norm,
                                      mla_kv_norm, mla_w_q_up, mla_w_kv_up)
            st_ckv.append(nckv)
            st_krope.append(nkr)
            w_out = mla_w_out
        x, comb = _outln_call(o, w_out, li, x, mods_l, ln_mix_g[l], ln_mix_b[l], router_w, router_b,
                              name="mix_out_ln")
        x = _moe_call(x, mods_l, comb, moe_w_gate, moe_w_up, moe_w_down, l, ln_ffn_g[l], ln_ffn_b[l],
                      split_out=(l == DEPTH - 1))

    y_prompt = x[0].reshape(BATCH, SEQ, D_MODEL)
    y_sample = x[1].reshape(DEC_BATCH, DEC_SEQ, D_MODEL)
    return (y_prompt, y_sample, jnp.stack(st_gdn, axis=1), jnp.stack(st_k, axis=1), jnp.stack(st_v, axis=1),
            jnp.stack(st_ckv, axis=1), jnp.stack(st_krope, axis=1))
```

```python
import functools
import math

import jax
import jax.numpy as jnp
from jax import lax
from jax.experimental import pallas as pl
from jax.experimental.pallas import tpu as pltpu

F32 = jnp.float32
BF16 = jnp.bfloat16

D_MODEL = 1024
BATCH = 32
SEQ = 256
DEPTH = 4
DEC_BATCH = 2
DEC_SEQ = 2048
PAST_LEN = 512
GRID_W = 64
N_MIXERS = 3

GDN_HEADS = 8
GDN_DK = 128
GDN_DV = 128
GDN_CONV = 5
GDN_CHUNK = 64
GDN_QKV = GDN_HEADS * (2 * GDN_DK + GDN_DV)
GDN_QKVZ = GDN_QKV + GDN_HEADS * GDN_DV

GQA_HEADS = 8
GQA_KV_HEADS = 2
GQA_GROUP = GQA_HEADS // GQA_KV_HEADS
GQA_HEAD_DIM = 128
GQA_IN = (GQA_HEADS + 2 * GQA_KV_HEADS) * GQA_HEAD_DIM

MLA_HEADS = 8
MLA_Q_LORA = 512
MLA_KV_LORA = 256
MLA_NOPE = 128
MLA_ROPE = 64
MLA_V = 128
MLA_IN = MLA_Q_LORA + MLA_KV_LORA + MLA_ROPE

N_EXPERTS = 16
N_GROUPS = 4
EXPERTS_PER_GROUP = N_EXPERTS // N_GROUPS
D_EXPERT = 256

ROPE_THETA = 10000.0
ALPHA = (2 * DEPTH) ** 0.25
NORM_EPS = 1e-6

T_CTX = BATCH * SEQ
T_LAT = DEC_BATCH * DEC_SEQ
T_ALL = T_CTX + T_LAT
SEG_ROWS = DEC_SEQ
N_CTX_SEGS = T_CTX // SEG_ROWS
MOD_ROWS = 8

V7X_LANES = 128
V7X_SUBLANES = 8
V7X_VMEM_BYTES = 64 * 1024 * 1024
VMEM_LIMIT = V7X_VMEM_BYTES * 7 // 8

GDN_SUPER = 256
CHUNKS_PER_SUPER = GDN_SUPER // GDN_CHUNK


def _cparams(*sem):
    return pltpu.CompilerParams(dimension_semantics=sem, vmem_limit_bytes=VMEM_LIMIT)


def _mm(a, b):
    return jnp.dot(a.astype(BF16), b.astype(BF16), preferred_element_type=F32)


def _mm_nt(a, b):
    return lax.dot_general(a.astype(BF16), b.astype(BF16), (((1,), (1,)), ((), ())),
                           preferred_element_type=F32)


def _silu(x):
    return x * (1.0 / (1.0 + jnp.exp(-x)))


def _sigmoid(x):
    return 1.0 / (1.0 + jnp.exp(-x))


def _chunk_of(idx):
    return idx >> int(math.log2(GDN_CHUNK))


def _seg_row(i, tiles_per_seg):
    return jnp.maximum(lax.div(i, tiles_per_seg) - (N_CTX_SEGS - 1), 0)


def _layer_norm(y, g, b):
    mu = jnp.mean(y, axis=-1, keepdims=True)
    yc = y - mu
    var = jnp.mean(yc * yc, axis=-1, keepdims=True)
    return yc * lax.rsqrt(var + NORM_EPS) * g + b


def _rms(x, g):
    return x * lax.rsqrt(jnp.mean(x * x, axis=-1, keepdims=True) + NORM_EPS) * g


def _slab_specs(tm, n_grid_axes, **mode):
    ctx_tiles = T_CTX // tm
    if n_grid_axes == 1:
        return [pl.BlockSpec((tm, D_MODEL), lambda i: (jnp.minimum(i, ctx_tiles - 1), 0), **mode),
                pl.BlockSpec((tm, D_MODEL), lambda i: (jnp.maximum(i - ctx_tiles, 0), 0), **mode)]
    return [pl.BlockSpec((tm, D_MODEL), lambda i, j: (jnp.minimum(i, ctx_tiles - 1), 0), **mode),
            pl.BlockSpec((tm, D_MODEL), lambda i, j: (jnp.maximum(i - ctx_tiles, 0), 0), **mode)]


def _slab_shapes():
    return [jax.ShapeDtypeStruct((T_CTX, D_MODEL), F32), jax.ShapeDtypeStruct((T_LAT, D_MODEL), F32)]


def _slab_read(i, xc_ref, xl_ref):
    return jnp.where(i < T_CTX // xc_ref.shape[0], xc_ref[...], xl_ref[...])


def _slab_write(i, when, yc_ref, yl_ref, value_fn):
    ctx_tiles = T_CTX // yc_ref.shape[0]

    @pl.when(when & (i < ctx_tiles))
    def _():
        yc_ref[...] = value_fn()

    @pl.when(when & (i >= ctx_tiles))
    def _():
        yl_ref[...] = value_fn()


def _ada_kernel(c_ref, w_ref, b_ref, o_ref):
    o_ref[...] = _mm(_silu(c_ref[...]), w_ref[...]) + b_ref[...]


def _ada_call(cond, w_ada, b_ada):
    tn = 1536
    n = w_ada.shape[-1]
    return pl.pallas_call(
        _ada_kernel,
        grid=(DEPTH, n // tn),
        in_specs=[
            pl.BlockSpec((MOD_ROWS, D_MODEL), lambda l, j: (0, 0)),
            pl.BlockSpec((None, D_MODEL, tn), lambda l, j: (l, 0, j)),
            pl.BlockSpec((None, 1, tn), lambda l, j: (l, 0, j)),
        ],
        out_specs=pl.BlockSpec((None, MOD_ROWS, tn), lambda l, j: (l, 0, j)),
        out_shape=jax.ShapeDtypeStruct((DEPTH, MOD_ROWS, n), F32),
        compiler_params=_cparams("arbitrary", "arbitrary"),
        name="ada_mod",
    )(cond, w_ada, b_ada.reshape(DEPTH, 1, n))


def _modproj_kernel(*refs, tiles_per_seg, has_extra, x_pair):
    n_x = 2 if x_pair else 1
    x_refs, (sh_ref, sc_ref, w_ref), rest = refs[:n_x], refs[n_x:n_x + 3], refs[n_x + 3:]
    if has_extra:
        wx_ref, o_ref, ox_ref, h_scr, wb_scr = rest
    else:
        o_ref, h_scr, wb_scr = rest
    i, j = pl.program_id(0), pl.program_id(1)

    @pl.when(i == 0)
    def _():
        wb_scr[j] = w_ref[...].astype(BF16)

    @pl.when(j == 0)
    def _():
        r = _seg_row(i, tiles_per_seg)
        sh = sh_ref[pl.ds(r, 1), :]
        sc = sc_ref[pl.ds(r, 1), :]
        x = _slab_read(i, *x_refs) if x_pair else x_refs[0][...]
        h = (x * (1.0 + sc) + sh).astype(BF16)
        h_scr[...] = h
        if has_extra:
            ox_ref[...] = _mm(h, wx_ref[...])

    o_ref[...] = jnp.dot(h_scr[...], wb_scr[j], preferred_element_type=F32).astype(o_ref.dtype)


def _modproj_call(x, mods_l, shift_idx, scale_idx, w, n_out, w_extra=None, *, w_layer=None, tm=1024, tn=512,
                  out_dtype=F32, name):
    t = T_ALL
    has_extra = w_extra is not None
    nj = n_out // tn
    if w_layer is None:
        w_spec = pl.BlockSpec((D_MODEL, tn), lambda i, j: (0, jnp.where(i == 0, j, nj - 1)))
    else:
        w_spec = pl.BlockSpec((None, D_MODEL, tn), lambda i, j: (w_layer, 0, jnp.where(i == 0, j, nj - 1)))
    x_pair = isinstance(x, tuple)
    x_specs = _slab_specs(tm, 2) if x_pair else [pl.BlockSpec((tm, D_MODEL), lambda i, j: (i, 0))]
    in_specs = x_specs + [
        pl.BlockSpec((MOD_ROWS, D_MODEL), lambda i, j: (0, shift_idx)),
        pl.BlockSpec((MOD_ROWS, D_MODEL), lambda i, j: (0, scale_idx)),
        w_spec,
    ]
    out_specs = [pl.BlockSpec((tm, tn), lambda i, j: (i, j))]
    out_shape = [jax.ShapeDtypeStruct((t, n_out), out_dtype)]
    args = (list(x) if x_pair else [x]) + [mods_l, mods_l, w]
    if has_extra:
        nx = w_extra.shape[1]
        in_specs.append(pl.BlockSpec((D_MODEL, nx), lambda i, j: (0, 0)))
        out_specs.append(pl.BlockSpec((tm, nx), lambda i, j: (i, 0)))
        out_shape.append(jax.ShapeDtypeStruct((t, nx), F32))
        args.append(w_extra)
    res = pl.pallas_call(
        functools.partial(_modproj_kernel, tiles_per_seg=SEG_ROWS // tm, has_extra=has_extra, x_pair=x_pair),
        grid=(t // tm, nj),
        in_specs=in_specs,
        out_specs=out_specs,
        out_shape=out_shape,
        scratch_shapes=[pltpu.VMEM((tm, D_MODEL), BF16), pltpu.VMEM((nj, D_MODEL, tn), BF16)],
        compiler_params=_cparams("arbitrary", "arbitrary"),
        name=name,
    )(*args)
    return res if has_extra else res[0]


def _rmsproj_kernel(x_ref, g_ref, w_ref, *rest, normalize, emit_xn):
    if emit_xn:
        o_ref, xn_ref, h_scr = rest
    else:
        o_ref, h_scr = rest
    j = pl.program_id(1)

    @pl.when(j == 0)
    def _():
        x = x_ref[...]
        xn = _rms(x, g_ref[...]) if normalize else x
        h_scr[...] = xn.astype(BF16)
        if emit_xn:
            xn_ref[...] = xn

    o_ref[...] = _mm(h_scr[...], w_ref[...])


def _rmsproj_call(x, col_block, k, g, w, *, normalize=True, emit_xn=False, tm=1024, tn=512, name):
    t = x.shape[0]
    n_out = w.shape[1]
    out_specs = [pl.BlockSpec((tm, tn), lambda i, j: (i, j))]
    out_shape = [jax.ShapeDtypeStruct((t, n_out), F32)]
    if emit_xn:
        out_specs.append(pl.BlockSpec((tm, k), lambda i, j: (i, 0)))
        out_shape.append(jax.ShapeDtypeStruct((t, k), F32))
    res = pl.pallas_call(
        functools.partial(_rmsproj_kernel, normalize=normalize, emit_xn=emit_xn),
        grid=(t // tm, n_out // tn),
        in_specs=[
            pl.BlockSpec((tm, k), lambda i, j: (i, col_block)),
            pl.BlockSpec((1, k), lambda i, j: (0, 0)),
            pl.BlockSpec((k, tn), lambda i, j: (0, j)),
        ],
        out_specs=out_specs,
        out_shape=out_shape,
        scratch_shapes=[pltpu.VMEM((tm, k), BF16)],
        compiler_params=_cparams("arbitrary", "arbitrary"),
        name=name,
    )(x, g.reshape(1, k), w)
    return res if emit_xn else res[0]


def _outln_kernel(oc_ref, ol_ref, w_ref, *refs, tiles_per_seg, x_pair):
    n_x = 2 if x_pair else 1
    x_refs = refs[:n_x]
    gate_ref, sh_ref, sc_ref, g_ref, b_ref, rw_ref, rb_ref, y_ref, comb_ref, wb_scr = refs[n_x:]
    i = pl.program_id(0)

    @pl.when(i == 0)
    def _():
        wb_scr[...] = w_ref[...].astype(BF16)

    r = _seg_row(i, tiles_per_seg)
    gate = gate_ref[pl.ds(r, 1), :]
    f = jnp.dot(_slab_read(i, oc_ref, ol_ref).astype(BF16), wb_scr[...], preferred_element_type=F32)
    x = _slab_read(i, *x_refs) if x_pair else x_refs[0][...]
    y = _layer_norm(ALPHA * x + (1.0 + gate) * f, g_ref[...], b_ref[...])
    y_ref[...] = y
    comb_ref[...] = _route(y * (1.0 + sc_ref[pl.ds(r, 1), :]) + sh_ref[pl.ds(r, 1), :], rw_ref, rb_ref)


def _outln_call(o, w, w_layer, x, mods_l, g, b, router_w, router_b, *, tm=512, name):
    k = o[0].shape[1]
    t = T_ALL
    rw = jnp.pad(router_w.T, ((0, V7X_LANES - N_EXPERTS), (0, 0)))
    rb = jnp.pad(router_b.reshape(N_EXPERTS, 1), ((0, V7X_LANES - N_EXPERTS), (0, 0)))
    o_specs = [pl.BlockSpec((tm, k), s.index_map) for s in _slab_specs(tm, 1)]

    def mod(idx):
        return pl.BlockSpec((MOD_ROWS, D_MODEL), lambda i: (0, idx))

    x_pair = isinstance(x, tuple)
    x_specs = _slab_specs(tm, 1) if x_pair else [pl.BlockSpec((tm, D_MODEL), lambda i: (i, 0))]
    return pl.pallas_call(
        functools.partial(_outln_kernel, tiles_per_seg=SEG_ROWS // tm, x_pair=x_pair),
        grid=(t // tm,),
        in_specs=o_specs + [pl.BlockSpec((None, k, D_MODEL), lambda i: (w_layer, 0, 0))] + x_specs + [
            mod(2), mod(3), mod(4),
            pl.BlockSpec((1, D_MODEL), lambda i: (0, 0)),
            pl.BlockSpec((1, D_MODEL), lambda i: (0, 0)),
            pl.BlockSpec((V7X_LANES, D_MODEL), lambda i: (0, 0)),
            pl.BlockSpec((V7X_LANES, 1), lambda i: (0, 0)),
        ],
        out_specs=[pl.BlockSpec((tm, D_MODEL), lambda i: (i, 0)), pl.BlockSpec((tm, V7X_LANES), lambda i: (i, 0))],
        out_shape=[jax.ShapeDtypeStruct((t, D_MODEL), F32), jax.ShapeDtypeStruct((t, V7X_LANES), F32)],
        scratch_shapes=[pltpu.VMEM((k, D_MODEL), BF16)],
        compiler_params=_cparams("arbitrary"),
        name=name,
    )(o[0], o[1], w, *(x if x_pair else (x,)), mods_l, mods_l, mods_l, g.reshape(1, D_MODEL),
      b.reshape(1, D_MODEL), rw, rb)


def _gdn_gates_kernel(ab_ref, alog_ref, dt_ref, o_ref):
    ab = ab_ref[...]
    tm = ab.shape[0]
    x = ab + dt_ref[...]
    softplus = jnp.maximum(x, 0.0) + jnp.log(1.0 + jnp.exp(-jnp.abs(x)))
    g = -jnp.exp(alog_ref[...]) * softplus
    ri = lax.broadcasted_iota(jnp.int32, (tm, tm), 0)
    ci = lax.broadcasted_iota(jnp.int32, (tm, tm), 1)
    same = _chunk_of(ri) == _chunk_of(ci)
    lower = jnp.where(same & (ri >= ci), 1.0, 0.0).astype(F32)
    upper = jnp.where(same & (ri <= ci), 1.0, 0.0).astype(F32)
    pre = jnp.dot(lower, g, precision=lax.Precision.HIGHEST, preferred_element_type=F32)
    suf = jnp.dot(upper, g, precision=lax.Precision.HIGHEST, preferred_element_type=F32)
    lane = lax.broadcasted_iota(jnp.int32, ab.shape, 1)
    o_ref[...] = jnp.where(lane < GDN_HEADS, pre, jnp.where(lane < 2 * GDN_HEADS, suf, _sigmoid(ab)))


def _gdn_gates_call(ab, a_log, dt_bias):
    t, n = ab.shape
    tm = GDN_SUPER
    pad = n - 2 * GDN_HEADS
    alog_row = jnp.pad(a_log.reshape(1, -1), ((0, 0), (0, pad)))
    dt_row = jnp.pad(dt_bias.reshape(1, -1), ((0, 0), (0, pad)))
    return pl.pallas_call(
        _gdn_gates_kernel,
        grid=(t // tm,),
        in_specs=[
            pl.BlockSpec((tm, n), lambda i: (i, 0)),
            pl.BlockSpec((1, n), lambda i: (0, 0)),
            pl.BlockSpec((1, n), lambda i: (0, 0)),
        ],
        out_specs=pl.BlockSpec((tm, n), lambda i: (i, 0)),
        out_shape=jax.ShapeDtypeStruct((t, n), F32),
        compiler_params=_cparams("arbitrary"),
        name="gdn_gates",
    )(ab, alog_row, dt_row)


def _loop(n, body, max_static=4, unroll=1):
    if n <= max_static:
        for i in range(n):
            body(i)
    else:
        def wrapped(i, carry):
            body(i)
            return carry
        lax.fori_loop(0, n, wrapped, 0, unroll=unroll)


def _gdn_core_kernel(q_ref, k_ref, v_ref, z_ref, cq_ref, ck_ref, cv_ref, gc_ref, gr_ref, ng_ref, *rest,
                     n_tok, has_s0, hps):
    if has_s0:
        s0_ref, o_ref, st_ref = rest[:3]
        scr = rest[3:]
    else:
        o_ref, st_ref = rest[:2]
        scr = rest[2:]
    pad_scr, kn_scr, u_scr, w_scr, qe_scr, qk_scr, oo_scr, s_scr = scr
    n_super = n_tok // GDN_SUPER
    n_chunk = n_tok // GDN_CHUNK
    halo = V7X_SUBLANES
    sb = GDN_SUPER
    ck = GDN_CHUNK
    dk = GDN_DK

    zeros_halo = jnp.zeros((halo, dk), F32)
    for hd in range(hps):
        for a, src in enumerate((q_ref, k_ref, v_ref)):
            pad_scr[3 * hd + a, 0:halo, :] = zeros_halo
            pad_scr[3 * hd + a, halo:halo + n_tok, :] = src[:, hd * dk:(hd + 1) * dk].astype(F32)
            pad_scr[3 * hd + a, halo + n_tok:2 * halo + n_tok, :] = zeros_halo
        for d in range(2):
            s_scr[2 * hd + d] = s0_ref[d, hd] if has_s0 else jnp.zeros((dk, GDN_DV), F32)

    ri = lax.broadcasted_iota(jnp.int32, (sb, sb), 0)
    ci = lax.broadcasted_iota(jnp.int32, (sb, sb), 1)
    same_chunk = _chunk_of(ri) == _chunk_of(ci)
    eye = jnp.where(ri == ci, 1.0, 0.0).astype(F32)
    n_levels = int(math.log2(ck))

    def level_mask(l):
        return ((ri >> (l + 1)) == (ci >> (l + 1))) & ((ri >> l) != (ci >> l))

    def conv_silu(a, w_ref, hd, r0):
        xp = pad_scr[a, pl.ds(r0, sb + 2 * halo), :]
        acc = None
        for j in range(GDN_CONV):
            shift = (GDN_CONV // 2 - j) % (sb + 2 * halo)
            tap = xp if shift == 0 else pltpu.roll(xp, shift, 0)
            term = w_ref[j:j + 1, hd * dk:(hd + 1) * dk] * tap[halo:halo + sb, :]
            acc = term if acc is None else acc + term
        return _silu(acc)

    chains = [(hd, d) for hd in range(hps) for d in range(2)]

    a_unroll = max(1, min(n_super, 8 // (2 * hps)))

    def phase_a(it):
        units = [(it * a_unroll + u, hd) for u in range(a_unroll) for hd in range(hps)]
        r0s = [pl.multiple_of(s * sb, sb) for s, _ in units]
        qs, ks, vs, ps, qks = [], [], [], [], []
        for (s, hd), r0 in zip(units, r0s):
            q = conv_silu(3 * hd, cq_ref, hd, r0)
            k = conv_silu(3 * hd + 1, ck_ref, hd, r0)
            v = conv_silu(3 * hd + 2, cv_ref, hd, r0)
            q = q * lax.rsqrt(jnp.sum(q * q, axis=-1, keepdims=True) + NORM_EPS) * (dk ** -0.5)
            k = k * lax.rsqrt(jnp.sum(k * k, axis=-1, keepdims=True) + NORM_EPS)
            kn_scr[hd, pl.ds(r0, sb), :] = k
            kb = k.astype(BF16)
            qs.append(q)
            ks.append(k)
            vs.append(v)
            ps.append(_mm_nt(kb, kb))
            qks.append(_mm_nt(q, kb))
        a_chains = [(n, d) for n in range(len(units)) for d in range(2)]
        a_mats, ts, rhs = [], [], []
        for n, d in a_chains:
            (s, hd), r0 = units[n], r0s[n]
            hdd = 2 * hd + d
            gcol = gc_ref[hd, pl.ds(r0, sb), d:d + 1]
            bcol = gc_ref[hd, pl.ds(r0, sb), 2 + d:3 + d]
            grow = gr_ref[hd, s, d:d + 1, :]
            if d == 0:
                mask, strict = same_chunk & (ri >= ci), same_chunk & (ri > ci)
            else:
                mask, strict = same_chunk & (ri <= ci), same_chunk & (ri < ci)
            dec = jnp.exp(jnp.where(mask, gcol - grow, -jnp.inf))
            a = jnp.where(strict, ps[n] * bcol * dec, 0.0)
            a_mats.append(a)
            ts.append(eye - jnp.where(level_mask(0), a, 0.0))
            eg = jnp.exp(gcol)
            rhs.append(jnp.concatenate([vs[n] * bcol, ks[n] * (bcol * eg)], axis=1).astype(BF16))
            qe_scr[hdd, pl.ds(r0, sb), :] = qs[n] * eg
            qkd = qks[n] * dec
            for c in range(CHUNKS_PER_SUPER):
                qk_scr[hdd, pl.ds(r0 + c * ck, ck), :] = qkd[c * ck:(c + 1) * ck, c * ck:(c + 1) * ck]
        tbs = [t.astype(BF16) for t in ts]
        for l in range(1, n_levels):
            lm = level_mask(l)
            ys = [_mm(jnp.where(lm, a_mats[i], 0.0), tbs[i]) for i in range(len(a_chains))]
            ts = [ts[i] - _mm(tbs[i], ys[i]) for i in range(len(a_chains))]
            tbs = [t.astype(BF16) for t in ts]
        for i, (n, d) in enumerate(a_chains):
            (s, hd), r0 = units[n], r0s[n]
            uw = _mm(tbs[i], rhs[i])
            u_scr[2 * hd + d, pl.ds(r0, sb), :] = uw[:, :GDN_DV]
            w_scr[2 * hd + d, pl.ds(r0, sb), :] = uw[:, GDN_DV:]

    _loop(n_super // a_unroll, phase_a, max_static=1)

    def phase_b(i):
        rows, ws_qs, v_new, kws, gls = [], [], [], [], []
        for hd, d in chains:
            hdd = 2 * hd + d
            c = i if d == 0 else n_chunk - 1 - i
            r = pl.multiple_of(c * ck, ck)
            r_last = r + (ck - 1) if d == 0 else r
            rows.append(r)
            gcol = gc_ref[hd, pl.ds(r, ck), d:d + 1]
            gl = gc_ref[hd, pl.ds(r_last, 1), d:d + 1]
            gls.append(gl)
            kws.append((kn_scr[hd, pl.ds(r, ck), :] * jnp.exp(gl - gcol)).T)
            wq = jnp.concatenate([w_scr[hdd, pl.ds(r, ck), :], qe_scr[hdd, pl.ds(r, ck), :]], axis=0)
            ws_qs.append(_mm(wq, s_scr[hdd]))
        for j, (hd, d) in enumerate(chains):
            hdd = 2 * hd + d
            vn = u_scr[hdd, pl.ds(rows[j], ck), :] - ws_qs[j][:ck]
            v_new.append(vn)
            s_scr[hdd] = s_scr[hdd] * jnp.exp(gls[j]) + _mm(kws[j], vn)
        for j, (hd, d) in enumerate(chains):
            hdd = 2 * hd + d
            oo_scr[hdd, pl.ds(rows[j], ck), :] = ws_qs[j][ck:] + _mm(qk_scr[hdd, pl.ds(rows[j], ck), :], v_new[j])

    _loop(n_chunk, phase_b, unroll=2)

    def finish(s):
        r0 = pl.multiple_of(s * sb, sb)
        for hd in range(hps):
            o = oo_scr[2 * hd, pl.ds(r0, sb), :] + oo_scr[2 * hd + 1, pl.ds(r0, sb), :]
            z = z_ref[pl.ds(r0, sb), hd * GDN_DV:(hd + 1) * GDN_DV].astype(F32)
            o_ref[pl.ds(r0, sb), hd * GDN_DV:(hd + 1) * GDN_DV] = _rms(o, ng_ref[...]) * _silu(z)

    _loop(n_super, finish)
    for hd in range(hps):
        for d in range(2):
            st_ref[d, hd] = s_scr[2 * hd + d]


def _gdn_core_call(qkvz, conv_w, gc, gr, norm_g, s0, *, batch, n_tok, row_block_off, hps, name):
    has_s0 = s0 is not None
    h = GDN_HEADS
    dk = GDN_DK
    n_super = n_tok // GDN_SUPER
    hb = h // hps

    mode = dict(pipeline_mode=pl.Buffered(1)) if n_tok > GDN_SUPER else {}

    def col(sec):
        return pl.BlockSpec((n_tok, dk * hps), lambda b, hh: (b + row_block_off, sec * hb + hh), **mode)

    def cw(sec):
        return pl.BlockSpec((GDN_CONV, dk * hps), lambda b, hh: (0, sec * hb + hh))

    in_specs = [col(0), col(1), col(2), col(3), cw(0), cw(1), cw(2),
                pl.BlockSpec((None, hps, n_tok, 4), lambda b, hh: (b, hh, 0, 0)),
                pl.BlockSpec((None, hps, n_super, 4, GDN_SUPER), lambda b, hh: (b, hh, 0, 0, 0)),
                pl.BlockSpec((1, GDN_DV), lambda b, hh: (0, 0))]
    args = [qkvz, qkvz, qkvz, qkvz, conv_w, conv_w, conv_w, gc, gr, norm_g.reshape(1, GDN_DV)]
    st_spec = pl.BlockSpec((None, 2, hps, dk, GDN_DV), lambda b, hh: (b, 0, hh, 0, 0))
    if has_s0:
        in_specs.append(st_spec)
        args.append(s0)
    return pl.pallas_call(
        functools.partial(_gdn_core_kernel, n_tok=n_tok, has_s0=has_s0, hps=hps),
        grid=(batch, hb),
        in_specs=in_specs,
        out_specs=[pl.BlockSpec((n_tok, GDN_DV * hps), lambda b, hh: (b, hh)), st_spec],
        out_shape=[jax.ShapeDtypeStruct((batch * n_tok, h * GDN_DV), F32),
                   jax.ShapeDtypeStruct((batch, 2, h, dk, GDN_DV), F32)],
        scratch_shapes=[
            pltpu.VMEM((3 * hps, n_tok + 2 * V7X_SUBLANES, dk), F32),
            pltpu.VMEM((hps, n_tok, dk), F32),
            pltpu.VMEM((2 * hps, n_tok, GDN_DV), F32),
            pltpu.VMEM((2 * hps, n_tok, dk), F32),
            pltpu.VMEM((2 * hps, n_tok, dk), F32),
            pltpu.VMEM((2 * hps, n_tok, GDN_CHUNK), F32),
            pltpu.VMEM((2 * hps, n_tok, GDN_DV), F32),
            pltpu.VMEM((2 * hps, dk, GDN_DV), F32),
        ],
        compiler_params=_cparams("arbitrary", "arbitrary"),
        name=name,
    )(*args)


def _gdn_layer(x, mods_l, li, state_gdn, gdn_w_in, gdn_conv, gdn_a_log, gdn_dt_bias, gdn_norm_g):
    w_ab = jnp.pad(gdn_w_in[li, :, GDN_QKVZ:], ((0, 0), (0, V7X_LANES - 4 * GDN_HEADS)))
    qkvz, ab = _modproj_call(x, mods_l, 0, 1, gdn_w_in, GDN_QKVZ, w_ab, w_layer=li, tn=1024, out_dtype=BF16,
                             name="gdn_in")
    gates = _gdn_gates_call(ab, gdn_a_log[li], gdn_dt_bias[li])[:, :4 * GDN_HEADS]

    def head_layouts(g, batch, n_tok):
        g = g.reshape(batch, n_tok, 4, GDN_HEADS)
        gc = jnp.transpose(g, (0, 3, 1, 2))
        gr = jnp.transpose(g, (0, 3, 2, 1)).reshape(batch, GDN_HEADS, 4, n_tok // GDN_SUPER, GDN_SUPER)
        return gc, jnp.transpose(gr, (0, 1, 3, 2, 4))

    gc_p, gr_p = head_layouts(gates[:T_CTX], BATCH, SEQ)
    gc_s, gr_s = head_layouts(gates[T_CTX:], DEC_BATCH, DEC_SEQ)
    o_p, st_p = _gdn_core_call(qkvz, gdn_conv[li], gc_p, gr_p, gdn_norm_g[li], None,
                               batch=BATCH, n_tok=SEQ, row_block_off=0, hps=4, name="gdn_core_ctx")
    o_s, _ = _gdn_core_call(qkvz, gdn_conv[li], gc_s, gr_s, gdn_norm_g[li], state_gdn[:, li],
                            batch=DEC_BATCH, n_tok=DEC_SEQ, row_block_off=T_CTX // DEC_SEQ, hps=2,
                            name="gdn_core_lat")
    return (o_p, o_s), st_p


def _rope_tables(n_tokens, rot_dim, reps):
    n_rows = n_tokens // GRID_W
    row = jnp.repeat(jnp.arange(n_rows, dtype=F32), GRID_W)
    colp = jnp.tile(jnp.arange(GRID_W, dtype=F32), n_rows)
    n_freq = rot_dim // 4
    inv_freq = ROPE_THETA ** (-jnp.arange(n_freq, dtype=F32) / n_freq)
    ra, ca = row[:, None] * inv_freq, colp[:, None] * inv_freq
    cos = jnp.concatenate([jnp.cos(ra), jnp.cos(ra), jnp.cos(ca), jnp.cos(ca)], axis=-1)
    sin = jnp.concatenate([-jnp.sin(ra), jnp.sin(ra), -jnp.sin(ca), jnp.sin(ca)], axis=-1)
    return jnp.tile(cos, (1, reps)), jnp.tile(sin, (1, reps))


def _rope(x, cos, sin, quarter):
    n = x.shape[-1]
    lane = lax.broadcasted_iota(jnp.int32, x.shape, x.ndim - 1)
    first = (lane & (2 * quarter - 1)) < quarter
    partner = jnp.where(first, pltpu.roll(x, n - quarter, x.ndim - 1), pltpu.roll(x, quarter, x.ndim - 1))
    return x * cos + partner * sin


def _gqa_prep_kernel(kv_ref, g_ref, *rest, rope):
    if rope:
        cos_ref, sin_ref, k_ref, v_ref = rest
    else:
        k_ref, v_ref = rest
    kv = kv_ref[...]
    nk = GQA_KV_HEADS * GQA_HEAD_DIM
    for hh in range(GQA_KV_HEADS):
        k = _rms(kv[:, hh * GQA_HEAD_DIM:(hh + 1) * GQA_HEAD_DIM], g_ref[...])
        if rope:
            k = _rope(k, cos_ref[...], sin_ref[...], GQA_HEAD_DIM // 4)
        k_ref[:, hh * GQA_HEAD_DIM:(hh + 1) * GQA_HEAD_DIM] = k
    v_ref[...] = kv[:, nk:]


def _gqa_prep_call(proj, k_norm, row_off, n_rows, tables, *, tm=512, name):
    rope = tables is not None
    nkv = 2 * GQA_KV_HEADS * GQA_HEAD_DIM
    kv_col = GQA_HEADS * GQA_HEAD_DIM // nkv
    off = row_off // tm
    in_specs = [pl.BlockSpec((tm, nkv), lambda i: (i + off, kv_col)),
                pl.BlockSpec((1, GQA_HEAD_DIM), lambda i: (0, 0))]
    args = [proj, k_norm.reshape(1, GQA_HEAD_DIM)]
    if rope:
        per_seq = DEC_SEQ // tm
        in_specs += [pl.BlockSpec((tm, GQA_HEAD_DIM), lambda i: (i % per_seq, 0))] * 2
        args += list(tables)
    half = nkv // 2
    return pl.pallas_call(
        functools.partial(_gqa_prep_kernel, rope=rope),
        grid=(n_rows // tm,),
        in_specs=in_specs,
        out_specs=[pl.BlockSpec((tm, half), lambda i: (i, 0))] * 2,
        out_shape=[jax.ShapeDtypeStruct((n_rows, half), F32)] * 2,
        compiler_params=_cparams("arbitrary"),
        name=name,
    )(*args)


def _attend(chains, scale):
    scores = [[_mm_nt(q, k) * scale for k in ks] for q, ks, _ in chains]
    probs, dens = [], []
    for sc in scores:
        m = None
        for s in sc:
            sm = jnp.max(s, axis=-1, keepdims=True)
            m = sm if m is None else jnp.maximum(m, sm)
        ps = [jnp.exp(s - m) for s in sc]
        den = None
        for p in ps:
            r = jnp.sum(p, axis=-1, keepdims=True)
            den = r if den is None else den + r
        probs.append([p.astype(BF16) for p in ps])
        dens.append(den)
    outs = []
    for (_, _, vs), ps, den in zip(chains, probs, dens):
        acc = None
        for p, v in zip(ps, vs):
            pv = _mm(p, v)
            acc = pv if acc is None else acc + pv
        outs.append(acc / den)
    return outs


def _gqa_attn_kernel(q_ref, qg_ref, *rest, n_seg, rope, kv_per_step):
    if rope:
        cos_ref, sin_ref = rest[:2]
        rest = rest[2:]
    seg_refs = rest[:2 * n_seg]
    o_ref = rest[2 * n_seg]
    dh = GQA_HEAD_DIM
    tq = q_ref.shape[0]
    heads_per_chain = GQA_GROUP // 2
    chains, heads = [], []
    for kh in range(kv_per_step):
        ks = [seg_refs[2 * sgi][:, kh * dh:(kh + 1) * dh].astype(BF16) for sgi in range(n_seg)]
        vs = [seg_refs[2 * sgi + 1][:, kh * dh:(kh + 1) * dh].astype(BF16) for sgi in range(n_seg)]
        for g0 in range(0, GQA_GROUP, heads_per_chain):
            qs = []
            for g in range(g0, g0 + heads_per_chain):
                c0 = (kh * GQA_GROUP + g) * dh
                qh = _rms(q_ref[:, c0:c0 + dh], qg_ref[...])
                if rope:
                    qh = _rope(qh, cos_ref[...], sin_ref[...], dh // 4)
                qs.append(qh.astype(BF16))
            chains.append((jnp.concatenate(qs, axis=0), ks, vs))
            heads.append(kh * GQA_GROUP + g0)
    for o, h0 in zip(_attend(chains, dh ** -0.5), heads):
        for g in range(heads_per_chain):
            o_ref[:, (h0 + g) * dh:(h0 + g + 1) * dh] = o[g * tq:(g + 1) * tq]


def _gqa_attn_call(proj, q_norm, segs, tables, *, batch, n_q, row_off, tq, kv_per_step, name):
    rope = tables is not None
    dh = GQA_HEAD_DIM
    qw = GQA_GROUP * dh * kv_per_step
    kw = dh * kv_per_step
    nqb = n_q // tq
    off = row_off // tq
    in_specs = [pl.BlockSpec((tq, qw), lambda b, h, i: (b * nqb + i + off, h)),
                pl.BlockSpec((1, dh), lambda b, h, i: (0, 0))]
    args = [proj, q_norm.reshape(1, dh)]
    if rope:
        in_specs += [pl.BlockSpec((tq, dh), lambda b, h, i: (i, 0))] * 2
        args += list(tables)
    for k2d, v2d, rows in segs:
        in_specs += [pl.BlockSpec((rows, kw), lambda b, h, i: (b, h))] * 2
        args += [k2d, v2d]
    return pl.pallas_call(
        functools.partial(_gqa_attn_kernel, n_seg=len(segs), rope=rope, kv_per_step=kv_per_step),
        grid=(batch, GQA_KV_HEADS // kv_per_step, nqb),
        in_specs=in_specs,
        out_specs=pl.BlockSpec((tq, qw), lambda b, h, i: (b * nqb + i, h)),
        out_shape=jax.ShapeDtypeStruct((batch * n_q, GQA_HEADS * dh), F32),
        compiler_params=_cparams("arbitrary", "arbitrary", "arbitrary"),
        name=name,
    )(*args)


def _gqa_layer(x, mods_l, li, cache_k, cache_v, gqa_w_in, gqa_q_norm, gqa_k_norm):
    proj = _modproj_call(x, mods_l, 0, 1, gqa_w_in, GQA_IN, w_layer=li, name="gqa_in")
    tables = _rope_tables(DEC_SEQ, GQA_HEAD_DIM, 1)
    nkv = GQA_KV_HEADS * GQA_HEAD_DIM
    k_p, v_p = _gqa_prep_call(proj, gqa_k_norm[li], 0, T_CTX, None, name="gqa_prep_ctx")
    k_s, v_s = _gqa_prep_call(proj, gqa_k_norm[li], T_CTX, T_LAT, tables, name="gqa_prep_lat")
    o_p = _gqa_attn_call(proj, gqa_q_norm[li], [(k_p, v_p, SEQ)], None, batch=BATCH, n_q=SEQ, row_off=0,
                         tq=SEQ, kv_per_step=GQA_KV_HEADS, name="gqa_attn_ctx")
    ck = cache_k[:, li].reshape(DEC_BATCH * PAST_LEN, nkv)
    cv = cache_v[:, li].reshape(DEC_BATCH * PAST_LEN, nkv)
    o_s = _gqa_attn_call(proj, gqa_q_norm[li], [(ck, cv, PAST_LEN), (k_s, v_s, DEC_SEQ)], tables,
                         batch=DEC_BATCH, n_q=DEC_SEQ, row_off=T_CTX, tq=256, kv_per_step=1, name="gqa_attn_lat")
    new_k = k_p.reshape(BATCH, SEQ, GQA_KV_HEADS, GQA_HEAD_DIM)
    new_v = v_p.reshape(BATCH, SEQ, GQA_KV_HEADS, GQA_HEAD_DIM)
    return (o_p, o_s), new_k, new_v


def _krope_kernel(x_ref, *rest, rope):
    if rope:
        cos_ref, sin_ref, o_ref = rest
    else:
        (o_ref,) = rest
    x = x_ref[...]
    x2 = jnp.concatenate([x, x], axis=1)
    if rope:
        x2 = _rope(x2, cos_ref[...], sin_ref[...], MLA_ROPE // 4)
    o_ref[...] = x2


def _krope_call(krope, tables, *, tm=512, name):
    rope = tables is not None
    n = krope.shape[0]
    in_specs = [pl.BlockSpec((tm, MLA_ROPE), lambda i: (i, 0))]
    args = [krope]
    if rope:
        per_seq = DEC_SEQ // tm
        in_specs += [pl.BlockSpec((tm, 2 * MLA_ROPE), lambda i: (i % per_seq, 0))] * 2
        args += list(tables)
    return pl.pallas_call(
        functools.partial(_krope_kernel, rope=rope),
        grid=(n // tm,),
        in_specs=in_specs,
        out_specs=pl.BlockSpec((tm, 2 * MLA_ROPE), lambda i: (i, 0)),
        out_shape=jax.ShapeDtypeStruct((n, 2 * MLA_ROPE), F32),
        compiler_params=_cparams("arbitrary"),
        name=name,
    )(*args)


def _mla_attn_kernel(qn_ref, qr_ref, *rest, n_seg, rope, heads_per_step):
    if rope:
        cos_ref, sin_ref = rest[:2]
        rest = rest[2:]
    seg_refs = rest[:2 * n_seg]
    o_ref = rest[2 * n_seg]
    scale = (MLA_NOPE + MLA_ROPE) ** -0.5
    h0 = pl.program_id(1) * heads_per_step
    pair_w = 2 * MLA_ROPE
    tq = qn_ref.shape[0]
    row_splits = 2 if heads_per_step == 1 else 1
    rows = tq // row_splits
    chains = []
    for j in range(heads_per_step):
        qn = qn_ref[:, j * MLA_NOPE:(j + 1) * MLA_NOPE]
        pair = qr_ref[:, (j // 2) * pair_w:(j // 2 + 1) * pair_w]
        if rope:
            pair = _rope(pair, cos_ref[...], sin_ref[...], MLA_ROPE // 4)
        lane = lax.broadcasted_iota(jnp.int32, pair.shape, 1)
        mine = (lane >> int(math.log2(MLA_ROPE))) == ((h0 + j) & 1)
        q = jnp.concatenate([qn, jnp.where(mine, pair, 0.0)], axis=1).astype(BF16)
        ks, vs = [], []
        for sgi in range(n_seg):
            kv = seg_refs[2 * sgi][:, j * (MLA_NOPE + MLA_V):(j + 1) * (MLA_NOPE + MLA_V)]
            ks.append(jnp.concatenate([kv[:, :MLA_NOPE], seg_refs[2 * sgi + 1][...]], axis=1).astype(BF16))
            vs.append(kv[:, MLA_NOPE:].astype(BF16))
        for c in range(row_splits):
            chains.append((q[c * rows:(c + 1) * rows], ks, vs))
    outs = _attend(chains, scale)
    for j in range(heads_per_step):
        for c in range(row_splits):
            o_ref[c * rows:(c + 1) * rows, j * MLA_V:(j + 1) * MLA_V] = outs[j * row_splits + c]


def _mla_attn_call(q, segs, tables, *, batch, n_q, row_off, tq, heads_per_step, name):
    rope = tables is not None
    hps = heads_per_step
    nqb = n_q // tq
    off = row_off // tq
    rope_w = max(2 * MLA_ROPE, MLA_ROPE * hps)
    rope_base = MLA_HEADS * MLA_NOPE // rope_w

    def rope_idx(b, h, i):
        return (b * nqb + i + off, rope_base + (h * hps * MLA_ROPE) // rope_w)

    in_specs = [pl.BlockSpec((tq, MLA_NOPE * hps), lambda b, h, i: (b * nqb + i + off, h)),
                pl.BlockSpec((tq, rope_w), rope_idx)]
    args = [q, q]
    if rope:
        in_specs += [pl.BlockSpec((tq, 2 * MLA_ROPE), lambda b, h, i: (i, 0))] * 2
        args += list(tables)
    for kv2d, kr2, rows, kv_row_off in segs:
        kv_off = kv_row_off // rows
        in_specs += [pl.BlockSpec((rows, (MLA_NOPE + MLA_V) * hps), lambda b, h, i, kv_off=kv_off: (b + kv_off, h)),
                     pl.BlockSpec((rows, 2 * MLA_ROPE), lambda b, h, i: (b, 0))]
        args += [kv2d, kr2]
    return pl.pallas_call(
        functools.partial(_mla_attn_kernel, n_seg=len(segs), rope=rope, heads_per_step=hps),
        grid=(batch, MLA_HEADS // hps, nqb),
        in_specs=in_specs,
        out_specs=pl.BlockSpec((tq, MLA_V * hps), lambda b, h, i: (b * nqb + i, h)),
        out_shape=jax.ShapeDtypeStruct((batch * n_q, MLA_HEADS * MLA_V), F32),
        compiler_params=_cparams("arbitrary", "arbitrary", "arbitrary"),
        name=name,
    )(*args)


def _mla_layer(x, mods_l, li, cache_ckv, cache_krope, mla_w_in, mla_q_norm, mla_kv_norm, mla_w_q_up, mla_w_kv_up):
    w_in = jnp.pad(mla_w_in[li], ((0, 0), (0, 7 * V7X_LANES - MLA_IN)))
    proj = _modproj_call(x, mods_l, 0, 1, w_in, 7 * V7X_LANES, tn=7 * V7X_LANES, name="mla_in")
    wq = mla_w_q_up[li].reshape(MLA_Q_LORA, MLA_HEADS, MLA_NOPE + MLA_ROPE)
    wq = jnp.concatenate([wq[:, :, :MLA_NOPE].reshape(MLA_Q_LORA, -1), wq[:, :, MLA_NOPE:].reshape(MLA_Q_LORA, -1)], axis=1)
    q = _rmsproj_call(proj, 0, MLA_Q_LORA, mla_q_norm[li], wq, name="mla_q_up")
    kv, ckv = _rmsproj_call(proj, MLA_Q_LORA // MLA_KV_LORA, MLA_KV_LORA, mla_kv_norm[li], mla_w_kv_up[li],
                            emit_xn=True, name="mla_kv_up")
    krope = proj[:, MLA_Q_LORA + MLA_KV_LORA:MLA_IN]
    tables = _rope_tables(DEC_SEQ, MLA_ROPE, 2)
    kr2_p = _krope_call(krope[:T_CTX], None, name="mla_krope_ctx")
    kr2_s = _krope_call(krope[T_CTX:], tables, name="mla_krope_lat")
    ctx_ckv = cache_ckv[:, li].reshape(DEC_BATCH * PAST_LEN, MLA_KV_LORA)
    kv_c = _rmsproj_call(ctx_ckv, 0, MLA_KV_LORA, jnp.ones((MLA_KV_LORA,), F32), mla_w_kv_up[li], normalize=False,
                         name="mla_kv_up_cache")
    kr2_c = _krope_call(cache_krope[:, li].reshape(DEC_BATCH * PAST_LEN, MLA_ROPE), None, name="mla_krope_cache")
    o_p = _mla_attn_call(q, [(kv, kr2_p, SEQ, 0)], None, batch=BATCH, n_q=SEQ, row_off=0, tq=SEQ,
                         heads_per_step=MLA_HEADS, name="mla_attn_ctx")
    o_s = _mla_attn_call(q, [(kv_c, kr2_c, PAST_LEN, 0), (kv, kr2_s, DEC_SEQ, T_CTX)], tables, batch=DEC_BATCH,
                         n_q=DEC_SEQ, row_off=T_CTX, tq=512, heads_per_step=1, name="mla_attn_lat")
    new_ckv = ckv[:T_CTX].reshape(BATCH, SEQ, MLA_KV_LORA)
    new_krope = krope[:T_CTX].reshape(BATCH, SEQ, MLA_ROPE)
    return (o_p, o_s), new_ckv, new_krope


def _route(h, rw_ref, rb_ref):
    logits = lax.dot_general(rw_ref[...], h, (((1,), (1,)), ((), ())), precision=lax.Precision.HIGHEST,
                             preferred_element_type=F32)[:N_EXPERTS]
    scores = _sigmoid(logits)
    sel = scores + rb_ref[0:N_EXPERTS, :]
    tm = sel.shape[1]
    neg = -jnp.inf
    gs = []
    for g in range(N_GROUPS):
        a, b, c, d = (sel[EXPERTS_PER_GROUP * g + j:EXPERTS_PER_GROUP * g + j + 1, :] for j in range(4))
        gs.append(jnp.maximum(jnp.maximum(jnp.maximum(a + b, a + c), jnp.maximum(a + d, b + c)),
                              jnp.maximum(b + d, c + d)))
    gmax = jnp.maximum(jnp.maximum(gs[0], gs[1]), jnp.maximum(gs[2], gs[3]))
    best = jnp.where(gs[0] == gmax, 0, jnp.where(gs[1] == gmax, 1, jnp.where(gs[2] == gmax, 2, 3)))
    eid = lax.broadcasted_iota(jnp.int32, (N_EXPERTS, tm), 0)
    masked = jnp.where((eid >> int(math.log2(EXPERTS_PER_GROUP))) == best, sel, neg)
    m1 = jnp.max(masked, axis=0, keepdims=True)
    i1 = jnp.min(jnp.where(masked == m1, eid, N_EXPERTS), axis=0, keepdims=True)
    masked2 = jnp.where(eid == i1, neg, masked)
    m2 = jnp.max(masked2, axis=0, keepdims=True)
    i2 = jnp.min(jnp.where(masked2 == m2, eid, N_EXPERTS), axis=0, keepdims=True)
    w1 = jnp.sum(jnp.where(eid == i1, scores, 0.0), axis=0, keepdims=True)
    w2 = jnp.sum(jnp.where(eid == i2, scores, 0.0), axis=0, keepdims=True)
    tot = w1 + w2
    comb = jnp.where(eid == i1, w1 / tot, 0.0) + jnp.where(eid == i2, w2 / tot, 0.0)
    comb = jnp.concatenate([comb, jnp.zeros((V7X_LANES - N_EXPERTS, tm), F32)], axis=0)
    return comb.T


def _moe_kernel(x_ref, sh_ref, sc_ref, gate_ref, comb_ref, wg_ref, wu_ref, wd_ref, g_ref, b_ref, *rest,
                tiles_per_seg, split_out):
    if split_out:
        yc_ref, yl_ref, h_scr, acc_scr = rest
    else:
        y_ref, h_scr, acc_scr = rest
    i, e = pl.program_id(0), pl.program_id(1)
    r = _seg_row(i, tiles_per_seg)

    @pl.when(e == 0)
    def _():
        h_scr[...] = (x_ref[...] * (1.0 + sc_ref[pl.ds(r, 1), :]) + sh_ref[pl.ds(r, 1), :]).astype(BF16)
        acc_scr[...] = jnp.zeros_like(acc_scr)

    comb = comb_ref[...]
    lane = lax.broadcasted_iota(jnp.int32, comb.shape, 1)
    h = h_scr[...]
    epb = wg_ref.shape[0]
    f = None
    for j in range(epb):
        c_e = jnp.sum(jnp.where(lane == e * epb + j, comb, 0.0), axis=1, keepdims=True)
        hid = _silu(_mm(h, wg_ref[j])) * _mm(h, wu_ref[j]) * c_e
        fj = _mm(hid, wd_ref[j])
        f = fj if f is None else f + fj
    acc_scr[...] += f

    def result():
        gate = gate_ref[pl.ds(r, 1), :]
        return _layer_norm(ALPHA * x_ref[...] + (1.0 + gate) * acc_scr[...], g_ref[...], b_ref[...])

    last = e == pl.num_programs(1) - 1
    if split_out:
        _slab_write(i, last, yc_ref, yl_ref, result)
    else:
        @pl.when(last)
        def _():
            y_ref[...] = result()


MOE_EXPERTS_PER_STEP = 2


def _moe_call(x, mods_l, comb, w_gate, w_up, w_down, layer, g, b, *, split_out, tm=1024):
    t = T_ALL
    epb = MOE_EXPERTS_PER_STEP
    if split_out:
        out_specs, out_shape = _slab_specs(tm, 2), _slab_shapes()
    else:
        out_specs = pl.BlockSpec((tm, D_MODEL), lambda i, e: (i, 0))
        out_shape = jax.ShapeDtypeStruct((t, D_MODEL), F32)
    return pl.pallas_call(
        functools.partial(_moe_kernel, tiles_per_seg=SEG_ROWS // tm, split_out=split_out),
        grid=(t // tm, N_EXPERTS // epb),
        in_specs=[
            pl.BlockSpec((tm, D_MODEL), lambda i, e: (i, 0)),
            pl.BlockSpec((MOD_ROWS, D_MODEL), lambda i, e: (0, 3)),
            pl.BlockSpec((MOD_ROWS, D_MODEL), lambda i, e: (0, 4)),
            pl.BlockSpec((MOD_ROWS, D_MODEL), lambda i, e: (0, 5)),
            pl.BlockSpec((tm, V7X_LANES), lambda i, e: (i, 0)),
            pl.BlockSpec((None, epb, D_MODEL, D_EXPERT), lambda i, e: (layer, e, 0, 0)),
            pl.BlockSpec((None, epb, D_MODEL, D_EXPERT), lambda i, e: (layer, e, 0, 0)),
            pl.BlockSpec((None, epb, D_EXPERT, D_MODEL), lambda i, e: (layer, e, 0, 0)),
            pl.BlockSpec((1, D_MODEL), lambda i, e: (0, 0)),
            pl.BlockSpec((1, D_MODEL), lambda i, e: (0, 0)),
        ],
        out_specs=out_specs,
        out_shape=out_shape,
        scratch_shapes=[pltpu.VMEM((tm, D_MODEL), BF16), pltpu.VMEM((tm, D_MODEL), F32)],
        compiler_params=_cparams("arbitrary", "arbitrary"),
        name="moe_experts",
    )(x, mods_l, mods_l, mods_l, comb, w_gate, w_up, w_down, g.reshape(1, D_MODEL), b.reshape(1, D_MODEL))


def kernel(x_prompt, x_sample, state_gdn, cache_gqa_k, cache_gqa_v, cache_mla_ckv, cache_mla_krope, c, c_ctx, w_ada, b_ada, ln_mix_g, ln_mix_b, ln_ffn_g, ln_ffn_b, gdn_w_in, gdn_conv, gdn_a_log, gdn_dt_bias, gdn_norm_g, gdn_w_out, gqa_w_in, gqa_q_norm, gqa_k_norm, gqa_w_out, mla_w_in, mla_q_norm, mla_kv_norm, mla_w_q_up, mla_w_kv_up, mla_w_out, router_w, router_b, moe_w_gate, moe_w_up, moe_w_down):
    x = (x_prompt.reshape(T_CTX, D_MODEL), x_sample.reshape(T_LAT, D_MODEL))
    cond = jnp.concatenate([c_ctx[None], c, jnp.zeros((MOD_ROWS - 1 - DEC_BATCH, D_MODEL), F32)], axis=0)
    mods = _ada_call(cond, w_ada, b_ada)

    st_gdn, st_k, st_v, st_ckv, st_krope = [], [], [], [], []
    for l in range(DEPTH):
        kind, li = l % N_MIXERS, l // N_MIXERS
        mods_l = mods[l]
        if kind == 0:
            o, st = _gdn_layer(x, mods_l, li, state_gdn, gdn_w_in, gdn_conv, gdn_a_log, gdn_dt_bias, gdn_norm_g)
            st_gdn.append(st)
            w_out = gdn_w_out
        elif kind == 1:
            o, nk, nv = _gqa_layer(x, mods_l, li, cache_gqa_k, cache_gqa_v, gqa_w_in, gqa_q_norm, gqa_k_norm)
            st_k.append(nk)
            st_v.append(nv)
            w_out = gqa_w_out
        else:
            o, nckv, nkr = _mla_layer(x, mods_l, li, cache_mla_ckv, cache_mla_krope, mla_w_in, mla_q_norm,
                                      mla_kv_norm, mla_w_q_up, mla_w_kv_up)
            st_ckv.append(nckv)
            st_krope.append(nkr)
            w_out = mla_w_out
        x, comb = _outln_call(o, w_out, li, x, mods_l, ln_mix_g[l], ln_mix_b[l], router_w, router_b,
                              name="mix_out_ln")
        x = _moe_call(x, mods_l, comb, moe_w_gate, moe_w_up, moe_w_down, l, ln_ffn_g[l], ln_ffn_b[l],
                      split_out=(l == DEPTH - 1))

    y_prompt = x[0].reshape(BATCH, SEQ, D_MODEL)
    y_sample = x[1].reshape(DEC_BATCH, DEC_SEQ, D_MODEL)
    return (y_prompt, y_sample, jnp.stack(st_gdn, axis=1), jnp.stack(st_k, axis=1), jnp.stack(st_v, axis=1),
            jnp.stack(st_ckv, axis=1), jnp.stack(st_krope, axis=1))
```
